```python
import math
import jax
import jax.numpy as jnp
from jax import lax
import numpy as np

D_MODEL = 1024
BATCH = 4
SEQ = 4096
DEPTH = 2

CTX_LEN = 256
GRID_W = 64
N_BRANCH = 4
BRANCH_WIDTH = 256
RET_HEADS = 4
RET_DK = 64
RET_DV = 64
RET_CHUNK = 128
RET_THETA = 10000.0
GLA_HEADS = 4
GLA_DK = 64
GLA_DV = 64
GLA_RANK = 16
GLA_TAU = 16.0
GLA_CHUNK = 64
GQA_HEADS = 4
GQA_KV_HEADS = 2
GQA_DH = 64
DIFF_HEADS = 4
DIFF_DQK = 32
DIFF_DV = 64
Q_BLOCK = 128
ROPE_THETA = 10000.0
D_FF = 2816
CONV_W = 3
RMS_EPS = 1e-6

IN_SPLITS = (
    ('ret_q', RET_HEADS * RET_DK), ('ret_k', RET_HEADS * RET_DK),
    ('ret_v', RET_HEADS * RET_DV), ('ret_g', RET_HEADS * RET_DV),
    ('gla_q', GLA_HEADS * GLA_DK), ('gla_k', GLA_HEADS * GLA_DK),
    ('gla_v', GLA_HEADS * GLA_DV), ('gla_r', GLA_HEADS * GLA_DV), ('gla_a', 2 * GLA_RANK),
    ('gqa_q', GQA_HEADS * GQA_DH), ('gqa_k', GQA_KV_HEADS * GQA_DH), ('gqa_v', GQA_KV_HEADS * GQA_DH),
    ('dif_q', DIFF_HEADS * 2 * DIFF_DQK), ('dif_k', DIFF_HEADS * 2 * DIFF_DQK), ('dif_v', DIFF_HEADS * DIFF_DV),
    ('gates', N_BRANCH * D_MODEL),
)
IN_WIDTH = sum(w for _, w in IN_SPLITS)

kernel_name = 'hybrid_parallel_ret_gla_gqa_diff_convffn'


def rms_norm(x, w=None):
    xf = x.astype(jnp.float32)
    y = xf * lax.rsqrt(jnp.mean(xf * xf, axis=-1, keepdims=True) + RMS_EPS)
    if w is not None:
        y = y * w.astype(jnp.float32)
    return y.astype(x.dtype)


def modulate(h, shift, scale):
    return h * (1.0 + scale) + shift


def split_cols(z):
    out = {}
    off = 0
    for name, w in IN_SPLITS:
        out[name] = z[..., off:off + w]
        off += w
    return out


def split_heads(z, n_heads):
    b, t, w = z.shape
    return z.reshape(b, t, n_heads, w // n_heads).transpose(0, 2, 1, 3)


def merge_heads(o):
    b, h, t, d = o.shape
    return o.transpose(0, 2, 1, 3).reshape(b, t, h * d)


def seq_dir(a, d):
    return jnp.flip(a, axis=2) if d == 1 else a


def apply_rope(x, cos, sin):
    h = x.shape[-1] // 2
    x1, x2 = x[..., :h], x[..., h:]
    cos = cos.astype(x.dtype)
    sin = sin.astype(x.dtype)
    return jnp.concatenate([x1 * cos - x2 * sin, x1 * sin + x2 * cos], axis=-1)


def axial_rope(rows, dim):
    axis_dim = dim // 2
    freqs = ROPE_THETA ** (-jnp.arange(0, axis_dim, 2, dtype=jnp.float32) / axis_dim)
    row = jnp.repeat(jnp.arange(rows, dtype=jnp.float32), GRID_W)
    col = jnp.tile(jnp.arange(GRID_W, dtype=jnp.float32), rows)
    ang = jnp.concatenate([row[:, None] * freqs, col[:, None] * freqs], axis=-1)
    return jnp.cos(ang), jnp.sin(ang)


def rope_1d(n_tok, dim):
    freqs = RET_THETA ** (-jnp.linspace(0.0, 1.0, dim // 2, dtype=jnp.float32))
    ang = jnp.arange(n_tok, dtype=jnp.float32)[:, None] * freqs
    return jnp.cos(ang), jnp.sin(ang)


def to_chunks(a, chunk):
    b, h, t, d = a.shape
    return a.reshape(b, h, t // chunk, chunk, d).transpose(2, 0, 1, 3, 4)


def from_chunks(o):
    n, b, h, c, d = o.shape
    return o.transpose(1, 2, 0, 3, 4).reshape(b, h, n * c, d)


def retention_scan(q, k, v, log_gamma, s0, with_out):
    c = RET_CHUNK
    idx = jnp.arange(c, dtype=jnp.float32)
    lg = log_gamma[:, None]
    dist = idx[:, None] - idx[None, :]
    decay = jnp.where(dist >= 0, jnp.exp(lg[:, :, None] * jnp.maximum(dist, 0.0)), 0.0)
    q_decay = jnp.exp(lg * (idx + 1.0))[None, :, :, None]
    k_decay = jnp.exp(lg * (c - 1.0 - idx))[None, :, :, None]
    chunk_decay = jnp.exp(log_gamma * c)[None, :, None, None]

    def step(s, inp):
        qi, ki, vi = inp
        s_new = s * chunk_decay + jnp.einsum('bhjd,bhjv->bhdv', ki * k_decay, vi)
        if not with_out:
            return s_new, None
        scores = jnp.einsum('bhid,bhjd->bhij', qi, ki) * decay
        o = jnp.einsum('bhij,bhjv->bhiv', scores, vi) + jnp.einsum('bhid,bhdv->bhiv', qi, s) * q_decay
        return s_new, o

    s_fin, out = lax.scan(step, s0, (to_chunks(q, c), to_chunks(k, c), to_chunks(v, c)))
    return (from_chunks(out) if with_out else None), s_fin


def gla_scan(q, k, v, log_a, s0, with_out):
    c = GLA_CHUNK
    causal = jnp.tril(jnp.ones((c, c), dtype=bool))[:, :, None]

    def step(s, inp):
        qi, ki, vi, gi = inp
        b = jnp.cumsum(gi, axis=2)
        b_last = b[:, :, -1:, :]
        s_new = jnp.exp(b_last[:, :, 0, :, None]) * s + jnp.einsum('bhjd,bhjv->bhdv', ki * jnp.exp(b_last - b), vi)
        if not with_out:
            return s_new, None
        rel = jnp.exp(jnp.where(causal, b[:, :, :, None, :] - b[:, :, None, :, :], -jnp.inf))
        att = jnp.einsum('bhid,bhjd,bhijd->bhij', qi, ki, rel)
        o = jnp.einsum('bhij,bhjv->bhiv', att, vi) + jnp.einsum('bhid,bhdv->bhiv', qi * jnp.exp(b), s)
        return s_new, o

    xs = (to_chunks(q, c), to_chunks(k, c), to_chunks(v, c), to_chunks(log_a, c))
    s_fin, out = lax.scan(step, s0, xs)
    return (from_chunks(out) if with_out else None), s_fin


def blocked_attention(q, k, v, scale):
    b, hk, g, t, d = q.shape
    nb = t // Q_BLOCK
    qb = q.reshape(b, hk, g, nb, Q_BLOCK, d).transpose(3, 0, 1, 2, 4, 5)

    def attend(qi):
        s = jnp.einsum('bkgqd,bksd->bkgqs', qi, k, preferred_element_type=jnp.float32) * scale
        p = jax.nn.softmax(s, axis=-1).astype(v.dtype)
        return jnp.einsum('bkgqs,bksv->bkgqv', p, v)

    o = lax.map(attend, qb)
    return o.transpose(1, 2, 3, 0, 4, 5).reshape(b, hk, g, t, v.shape[-1])


def retention_branch(zl, zc, log_gamma, rope, need_ctx):
    cos, sin = rope
    scale = RET_DK ** -0.5
    ql = apply_rope(split_heads(zl['ret_q'], RET_HEADS), cos, sin)
    kl = apply_rope(split_heads(zl['ret_k'], RET_HEADS), cos, sin) * scale
    vl = split_heads(zl['ret_v'], RET_HEADS)
    qc = split_heads(zc['ret_q'], RET_HEADS)
    kc = split_heads(zc['ret_k'], RET_HEADS) * scale
    vc = split_heads(zc['ret_v'], RET_HEADS)
    s0 = jnp.zeros((ql.shape[0], RET_HEADS, RET_DK, RET_DV), jnp.float32)
    lat, ctx = [], []
    for d in range(2):
        oc, s_ctx = retention_scan(seq_dir(qc, d), seq_dir(kc, d), seq_dir(vc, d), log_gamma[d], s0, need_ctx)
        ol, _ = retention_scan(seq_dir(ql, d), seq_dir(kl, d), seq_dir(vl, d), log_gamma[d], s_ctx, True)
        lat.append(seq_dir(ol, d))
        if need_ctx:
            ctx.append(seq_dir(oc, d))

    def out(o, g):
        return merge_heads(rms_norm(o)).astype(g.dtype) * jax.nn.silu(g)

    y_l = out(lat[0] + lat[1], zl['ret_g'])
    y_c = out(ctx[0] + ctx[1], zc['ret_g']) if need_ctx else None
    return y_l, y_c


def gla_branch(zl, zc, gate_up, gate_b, norm_w, need_ctx):
    def prep(z):
        q = split_heads(z['gla_q'], GLA_HEADS) * GLA_DK ** -0.5
        k = split_heads(z['gla_k'], GLA_HEADS)
        v = split_heads(z['gla_v'], GLA_HEADS)
        b, t, _ = z['gla_a'].shape
        lr = z['gla_a'].reshape(b, t, 2, GLA_RANK)
        logit = jnp.einsum('btnr,nrk->btnk', lr, gate_up) + gate_b
        log_a = jax.nn.log_sigmoid(logit.astype(jnp.float32)) / GLA_TAU
        return q, k, v, [split_heads(log_a[:, :, d], GLA_HEADS) for d in range(2)]

    ql, kl, vl, al = prep(zl)
    qc, kc, vc, ac = prep(zc)
    s0 = jnp.zeros((ql.shape[0], GLA_HEADS, GLA_DK, GLA_DV), jnp.float32)
    lat, ctx = [], []
    for d in range(2):
        oc, s_ctx = gla_scan(seq_dir(qc, d), seq_dir(kc, d), seq_dir(vc, d), seq_dir(ac[d], d), s0, need_ctx)
        ol, _ = gla_scan(seq_dir(ql, d), seq_dir(kl, d), seq_dir(vl, d), seq_dir(al[d], d), s_ctx, True)
        lat.append(seq_dir(ol, d))
        if need_ctx:
            ctx.append(seq_dir(oc, d))

    def out(o, r):
        return merge_heads(rms_norm(o, norm_w)).astype(r.dtype) * jax.nn.silu(r)

    y_l = out(lat[0] + lat[1], zl['gla_r'])
    y_c = out(ctx[0] + ctx[1], zc['gla_r']) if need_ctx else None
    return y_l, y_c


def gqa_branch(zl, zc, q_norm, k_norm, rope, need_ctx):
    cos, sin = rope
    group = GQA_HEADS // GQA_KV_HEADS

    def prep(z, use_rope):
        q = rms_norm(split_heads(z['gqa_q'], GQA_HEADS), q_norm)
        k = rms_norm(split_heads(z['gqa_k'], GQA_KV_HEADS), k_norm)
        v = split_heads(z['gqa_v'], GQA_KV_HEADS)
        if use_rope:
            q = apply_rope(q, cos, sin)
            k = apply_rope(k, cos, sin)
        b, _, t, _ = q.shape
        return q.reshape(b, GQA_KV_HEADS, group, t, GQA_DH), k, v

    ql, kl, vl = prep(zl, True)
    qc, kc, vc = prep(zc, False)
    scale = GQA_DH ** -0.5
    b, _, _, t, _ = ql.shape
    ol = blocked_attention(ql, jnp.concatenate([kl, kc], axis=2), jnp.concatenate([vl, vc], axis=2), scale)
    y_l = merge_heads(ol.reshape(b, GQA_HEADS, t, GQA_DH))
    y_c = None
    if need_ctx:
        oc = blocked_attention(qc, kc, vc, scale)
        y_c = merge_heads(oc.reshape(b, GQA_HEADS, qc.shape[3], GQA_DH))
    return y_l, y_c


def diff_branch(zl, zc, q_norm, k_norm, lam_vecs, subln, layer_idx, rope, need_ctx):
    cos, sin = rope
    lam_init = 0.8 - 0.6 * math.exp(-0.3 * layer_idx)
    lv = lam_vecs.astype(jnp.float32)
    lam = jnp.exp(jnp.sum(lv[0] * lv[1])) - jnp.exp(jnp.sum(lv[2] * lv[3])) + lam_init

    def qk_maps(a):
        b, t, _ = a.shape
        return a.reshape(b, t, DIFF_HEADS, 2, DIFF_DQK).transpose(0, 3, 2, 1, 4).reshape(b, 2 * DIFF_HEADS, t, DIFF_DQK)

    def prep(z, use_rope):
        q = rms_norm(qk_maps(z['dif_q']), q_norm)
        k = rms_norm(qk_maps(z['dif_k']), k_norm)
        if use_rope:
            q = apply_rope(q, cos, sin)
            k = apply_rope(k, cos, sin)
        v = split_heads(z['dif_v'], DIFF_HEADS)
        return q[:, :, None], k, v

    ql, kl, vl = prep(zl, True)
    qc, kc, vc = prep(zc, False)
    scale = DIFF_DQK ** -0.5

    def combine(o):
        b, _, _, t, _ = o.shape
        o = o.reshape(b, 2, DIFF_HEADS, t, DIFF_DV)
        o = o[:, 0] - lam.astype(o.dtype) * o[:, 1]
        return merge_heads(rms_norm(o, subln) * (1.0 - lam_init))

    v_lat = jnp.concatenate([vl, vc], axis=2)
    ol = blocked_attention(ql, jnp.concatenate([kl, kc], axis=2), jnp.concatenate([v_lat, v_lat], axis=1), scale)
    y_l = combine(ol)
    y_c = combine(blocked_attention(qc, kc, jnp.concatenate([vc, vc], axis=1), scale)) if need_ctx else None
    return y_l, y_c


def merge_branches(ys, gate_logits, w_branch, w_out):
    b, t, _ = gate_logits.shape
    g = jax.nn.sigmoid(gate_logits.reshape(b, t, N_BRANCH, D_MODEL))
    m = g[:, :, 0] * (ys[0] @ w_branch[0])
    for i in range(1, N_BRANCH):
        m = m + g[:, :, i] * (ys[i] @ w_branch[i])
    return m @ w_out


def token_mixer(hl, hc, w_in, ret_decay, gla_gate_up, gla_gate_b, gla_norm, gqa_qnorm, gqa_knorm,
                dif_qnorm, dif_knorm, dif_lambda, dif_subln, w_branch, w_out, tables, layer_idx, need_ctx):
    ret_rope, rope64, rope32 = tables
    zl = split_cols(hl @ w_in)
    zc = split_cols(hc @ w_in)
    log_gamma = jax.nn.log_sigmoid(ret_decay.astype(jnp.float32))
    a = retention_branch(zl, zc, log_gamma, ret_rope, need_ctx)
    b = gla_branch(zl, zc, gla_gate_up, gla_gate_b, gla_norm, need_ctx)
    c = gqa_branch(zl, zc, gqa_qnorm, gqa_knorm, rope64, need_ctx)
    d = diff_branch(zl, zc, dif_qnorm, dif_knorm, dif_lambda, dif_subln, layer_idx, rope32, need_ctx)
    y_l = merge_branches([a[0], b[0], c[0], d[0]], zl['gates'], w_branch, w_out)
    y_c = merge_branches([a[1], b[1], c[1], d[1]], zc['gates'], w_branch, w_out) if need_ctx else None
    return y_l, y_c


def depthwise_conv(x, w, bias):
    pad = CONV_W // 2
    t = x.shape[1]
    xp = jnp.pad(x, ((0, 0), (pad, pad), (0, 0)))
    y = xp[:, 0:t] * w[0]
    for j in range(1, CONV_W):
        y = y + xp[:, j:j + t] * w[j]
    return y + bias


def conv_ffn(h, w_up, conv_w, conv_b, w_down):
    u = h @ w_up
    a = depthwise_conv(u[..., :D_FF], conv_w, conv_b)
    return (jax.nn.silu(a) * u[..., D_FF:]) @ w_down


def setup_inputs(seed: int = 0) -> dict:
    key = jax.random.key(seed)
    ks = iter(jax.random.split(key, 32))

    def nrm(shape, s):
        return jax.random.normal(next(ks), shape, jnp.float32) * s

    L = DEPTH
    ret_logit = jnp.log(2.0 ** (5.0 + jnp.arange(RET_HEADS, dtype=jnp.float32)) - 1.0)
    return {
        'x': nrm((BATCH, SEQ, D_MODEL), 1.0),
        'c': nrm((BATCH, D_MODEL), 1.0),
        'ctx': nrm((BATCH, CTX_LEN, D_MODEL), 1.0),
        'c_ctx': nrm((D_MODEL,), 1.0),
        'w_mod': nrm((L, D_MODEL, 6 * D_MODEL), 0.5 * D_MODEL ** -0.5),
        'b_mod': nrm((L, 6 * D_MODEL), 0.02),
        'norm1': 1.0 + nrm((L, D_MODEL), 0.02),
        'norm2': 1.0 + nrm((L, D_MODEL), 0.02),
        'w_in': nrm((L, D_MODEL, IN_WIDTH), D_MODEL ** -0.5),
        'ret_decay': ret_logit[None, None, :] + nrm((L, 2, RET_HEADS), 0.1),
        'gla_gate_up': nrm((L, 2, GLA_RANK, GLA_HEADS * GLA_DK), GLA_RANK ** -0.5),
        'gla_gate_b': nrm((L, 2, GLA_HEADS * GLA_DK), 0.1),
        'gla_norm': 1.0 + nrm((L, GLA_DV), 0.02),
        'gqa_qnorm': 1.0 + nrm((L, GQA_DH), 0.02),
        'gqa_knorm': 1.0 + nrm((L, GQA_DH), 0.02),
        'dif_qnorm': 1.0 + nrm((L, DIFF_DQK), 0.02),
        'dif_knorm': 1.0 + nrm((L, DIFF_DQK), 0.02),
        'dif_lambda': nrm((L, 4, DIFF_DQK), 0.1),
        'dif_subln': 1.0 + nrm((L, DIFF_DV), 0.02),
        'w_branch': nrm((L, N_BRANCH, BRANCH_WIDTH, D_MODEL), BRANCH_WIDTH ** -0.5),
        'w_out': nrm((L, D_MODEL, D_MODEL), D_MODEL ** -0.5),
        'w_up': nrm((L, D_MODEL, 2 * D_FF), D_MODEL ** -0.5),
        'conv_w': nrm((L, CONV_W, D_FF), CONV_W ** -0.5),
        'conv_b': nrm((L, D_FF), 0.02),
        'w_down': nrm((L, D_FF, D_MODEL), D_FF ** -0.5),
    }


def reference(x, c, ctx, c_ctx, w_mod, b_mod, norm1, norm2, w_in, ret_decay, gla_gate_up, gla_gate_b,
              gla_norm, gqa_qnorm, gqa_knorm, dif_qnorm, dif_knorm, dif_lambda, dif_subln, w_branch, w_out,
              w_up, conv_w, conv_b, w_down):
    n_tok = x.shape[1]
    rows = n_tok // GRID_W
    tables = (rope_1d(n_tok, RET_DK), axial_rope(rows, GQA_DH), axial_rope(rows, DIFF_DQK))
    xc = ctx
    for l in range(DEPTH):
        need_ctx = l < DEPTH - 1
        mod_l = (jax.nn.silu(c) @ w_mod[l] + b_mod[l])[:, None, :]
        mod_c = (jax.nn.silu(c_ctx) @ w_mod[l] + b_mod[l])[None, None, :]
        sh1, sc1, g1, sh2, sc2, g2 = jnp.split(mod_l, 6, axis=-1)
        sh1c, sc1c, g1c, sh2c, sc2c, g2c = jnp.split(mod_c, 6, axis=-1)
        hl = modulate(rms_norm(x, norm1[l]), sh1, sc1)
        hc = modulate(rms_norm(xc, norm1[l]), sh1c, sc1c)
        y_l, y_c = token_mixer(hl, hc, w_in[l], ret_decay[l], gla_gate_up[l], gla_gate_b[l], gla_norm[l],
                               gqa_qnorm[l], gqa_knorm[l], dif_qnorm[l], dif_knorm[l], dif_lambda[l],
                               dif_subln[l], w_branch[l], w_out[l], tables, l, need_ctx)
        x = x + g1 * y_l
        x = x + g2 * conv_ffn(modulate(rms_norm(x, norm2[l]), sh2, sc2), w_up[l], conv_w[l], conv_b[l], w_down[l])
        if need_ctx:
            xc = xc + g1c * y_c
            xc = xc + g2c * conv_ffn(modulate(rms_norm(xc, norm2[l]), sh2c, sc2c),
                                     w_up[l], conv_w[l], conv_b[l], w_down[l])
    return x
```

```python
import functools
import math

import jax
import jax.numpy as jnp
from jax import lax
from jax.experimental import pallas as pl
from jax.experimental.pallas import tpu as pltpu

F32 = jnp.float32
BF16 = jnp.bfloat16

N_HEADS = 4
HEAD_W = 64
BRANCH_W = N_HEADS * HEAD_W
GQA_KV_W = 2 * HEAD_W
DIF_QK = 32
RET_CHUNK_LEN = 128
GLA_CHUNK_LEN = 64
GLA_SUB = 16
GLA_RANK_W = 16
GLA_TAU_INV = 1.0 / 16.0
GRID_WIDTH = 64
ROPE_BASE = 10000.0
EPS = 1e-6
D_FF_W = 2816

LANES = 128
VMEM_LIMIT_BYTES = 56 * 1024 * 1024

RET_OFF, GLA_OFF, GQA_OFF, DIF_OFF, PROJ_W = 0, 1024, 2176, 2688, 3456
GLA_GROUP_W = GQA_OFF - GLA_OFF


def _cparams(sem):
    return pltpu.CompilerParams(dimension_semantics=sem, vmem_limit_bytes=VMEM_LIMIT_BYTES)


def _dot(a, b):
    return jnp.dot(a.astype(BF16), b.astype(BF16), preferred_element_type=F32)


def _dot_nt(a, b):
    return lax.dot_general(a.astype(BF16), b.astype(BF16), (((1,), (1,)), ((), ())),
                           preferred_element_type=F32)


def _dot_tn(a, b):
    return lax.dot_general(a.astype(BF16), b.astype(BF16), (((0,), (0,)), ((), ())),
                           preferred_element_type=F32)


def _sigmoid(x):
    return 1.0 / (1.0 + jnp.exp(-x))


def _silu(x):
    return x * _sigmoid(x)


def _adaln(x, norm_w, shift, scale):
    h = x * lax.rsqrt(jnp.mean(x * x, axis=-1, keepdims=True) + EPS) * norm_w
    return h * (1.0 + scale) + shift


def _group_rms(x, bd, weight, group):
    x2 = x * x
    hi = x2.astype(BF16)
    lo = (x2 - hi.astype(F32)).astype(BF16)
    ss = jnp.dot(hi, bd, preferred_element_type=F32) + jnp.dot(lo, bd, preferred_element_type=F32)
    y = x * lax.rsqrt(ss * (1.0 / group) + EPS)
    return y if weight is None else y * weight


def _rope(x, cos, sin_signed, half):
    w = x.shape[1]
    reps = w // LANES
    if reps > 1:
        cos = jnp.concatenate([cos] * reps, axis=1)
        sin_signed = jnp.concatenate([sin_signed] * reps, axis=1)
    lane = lax.broadcasted_iota(jnp.int32, x.shape, 1)
    lower = (lane & (2 * half - 1)) < half
    partner = jnp.where(lower, pltpu.roll(x, w - half, 1), pltpu.roll(x, half, 1))
    return x * cos + partner * sin_signed


def _mod_kernel(c_ref, w_ref, b_ref, o_ref):
    o_ref[...] = _dot(_silu(c_ref[...]), w_ref[...]) + b_ref[...]


def _modulation(cc, w_mod, b_mod):
    n_layers, d, width = w_mod.shape
    tn = 1024
    return pl.pallas_call(
        _mod_kernel,
        grid=(n_layers, width // tn),
        in_specs=[pl.BlockSpec((8, d), lambda l, j: (0, 0)),
                  pl.BlockSpec((None, d, tn), lambda l, j: (l, 0, j)),
                  pl.BlockSpec((None, 1, tn), lambda l, j: (l, 0, j))],
        out_specs=pl.BlockSpec((None, 8, tn), lambda l, j: (l, 0, j)),
        out_shape=jax.ShapeDtypeStruct((n_layers, 8, width), F32),
        compiler_params=_cparams(("parallel", "parallel")),
        name="modulation",
    )(cc, w_mod, b_mod.reshape(n_layers, 1, width))


def _inproj_kernel(x_ref, mod_ref, n1_ref, w_ref, tab_ref, bd64_ref, bd32_ref, nw_ref,
                   zr_ref, zg_ref, cq_ref, ck_ref, cv_ref, dq_ref, dk_ref, dv_ref):
    hb = _adaln(x_ref[...], n1_ref[...], mod_ref[0:1, :], mod_ref[1:2, :]).astype(BF16)
    tab = tab_ref[...]
    r_cos, r_sin = tab[:, 0:128], tab[:, 128:256]
    g_cos, g_sin = tab[:, 256:384], tab[:, 384:512]
    d_cos, d_sin = tab[:, 512:640], tab[:, 640:768]

    zr = jnp.dot(hb, w_ref[:, RET_OFF:GLA_OFF], preferred_element_type=F32)
    zr_ref[:, 0:256] = _rope(zr[:, 0:256], r_cos, r_sin, 32)
    zr_ref[:, 256:512] = _rope(zr[:, 256:512], r_cos, r_sin, 32) * (HEAD_W ** -0.5)
    zr_ref[:, 512:1024] = zr[:, 512:1024]

    zg = jnp.dot(hb, w_ref[:, GLA_OFF:GQA_OFF], preferred_element_type=F32)
    zg_ref[:, 0:256] = zg[:, 0:256] * (HEAD_W ** -0.5)
    zg_ref[:, 256:GLA_GROUP_W] = zg[:, 256:GLA_GROUP_W]

    zc = jnp.dot(hb, w_ref[:, GQA_OFF:DIF_OFF], preferred_element_type=F32)
    bd64 = bd64_ref[...]
    q = _group_rms(zc[:, 0:256], bd64, nw_ref[0:1, :], HEAD_W)
    cq_ref[...] = (_rope(q, g_cos, g_sin, 32) * (HEAD_W ** -0.5)).astype(BF16)
    k = _group_rms(zc[:, 256:384], bd64[0:128, 0:128], nw_ref[1:2, 0:128], HEAD_W)
    ck_ref[...] = _rope(k, g_cos, g_sin, 32).astype(BF16)
    cv_ref[...] = zc[:, 384:512].astype(BF16)

    zd = jnp.dot(hb, w_ref[:, DIF_OFF:PROJ_W], preferred_element_type=F32)
    bd32 = bd32_ref[...]
    q = _group_rms(zd[:, 0:256], bd32, nw_ref[2:3, :], DIF_QK)
    dq_ref[...] = (_rope(q, d_cos, d_sin, 16) * (DIF_QK ** -0.5)).astype(BF16)
    k = _group_rms(zd[:, 256:512], bd32, nw_ref[3:4, :], DIF_QK)
    dk_ref[...] = _rope(k, d_cos, d_sin, 16).astype(BF16)
    dv_ref[...] = zd[:, 512:768].astype(BF16)


def _inproj(xf, mod, norm1, w_proj, table, bd64, bd32, qk_norms, geo):
    n, d = xf.shape
    tm = geo["tm"]
    nlt, tps, nb = geo["lat_tiles"], geo["tiles_per_seq"], geo["batch"]
    row = lambda i: (i, 0)
    const = lambda i: (0, 0)
    widths = (1024, GLA_GROUP_W, 256, 128, 128, 256, 256, 256)
    dtypes = (F32, F32, BF16, BF16, BF16, BF16, BF16, BF16)
    return pl.pallas_call(
        _inproj_kernel,
        grid=(n // tm,),
        in_specs=[pl.BlockSpec((tm, d), row),
                  pl.BlockSpec((None, 6, d), lambda i: (jnp.where(i < nlt, i // tps, nb), 0, 0)),
                  pl.BlockSpec((1, d), const),
                  pl.BlockSpec((d, PROJ_W), const),
                  pl.BlockSpec((tm, 768), lambda i: (jnp.where(i < nlt, i % tps, tps), 0)),
                  pl.BlockSpec((256, 256), const),
                  pl.BlockSpec((256, 256), const),
                  pl.BlockSpec((8, 256), const)],
        out_specs=[pl.BlockSpec((tm, w), row) for w in widths],
        out_shape=[jax.ShapeDtypeStruct((n, w), t) for w, t in zip(widths, dtypes)],
        compiler_params=_cparams(("parallel",)),
        name="inproj",
    )(xf, mod, norm1, w_proj, table, bd64, bd32, qk_norms)


def _ret_kernel(lgm_ref, lgl_ref, zf_ref, zb_ref, of_ref, ob_ref, sf_ref, sb_ref):
    c = RET_CHUNK_LEN

    @pl.when(pl.program_id(1) == 0)
    def _():
        sf_ref[...] = jnp.zeros_like(sf_ref)
        sb_ref[...] = jnp.zeros_like(sb_ref)

    ii = lax.broadcasted_iota(jnp.int32, (c, c), 0)
    jj = lax.broadcasted_iota(jnp.int32, (c, c), 1)
    ri = lax.broadcasted_iota(jnp.int32, (c, BRANCH_W), 0).astype(F32)
    for rev, z_ref, o_ref, st_ref in ((False, zf_ref, of_ref, sf_ref), (True, zb_ref, ob_ref, sb_ref)):
        d = int(rev)
        q = z_ref[:, 0:256]
        k = z_ref[:, 256:512]
        v = z_ref[:, 512:768]
        dist = ((jj - ii) if rev else (ii - jj)).astype(F32)
        lgl = lgl_ref[d:d + 1, :]
        q_dec = jnp.exp(lgl * ((c - ri) if rev else (ri + 1.0)))
        k_dec = jnp.exp(lgl * (ri if rev else (c - 1.0 - ri)))
        c_dec = jnp.exp(lgl * float(c))
        kd = k * k_dec
        outs = []
        for h in range(N_HEADS):
            sl = slice(h * HEAD_W, (h + 1) * HEAD_W)
            lg_h = lgm_ref[d * N_HEADS + h:d * N_HEADS + h + 1, :]
            dec = jnp.where(dist >= 0, jnp.exp(lg_h * jnp.maximum(dist, 0.0)), 0.0)
            qh = q[:, sl].astype(BF16)
            vh = v[:, sl].astype(BF16)
            sc = _dot_nt(qh, k[:, sl]) * dec
            s_old = st_ref[h]
            outs.append(_dot(sc, vh) + _dot(qh, s_old) * q_dec[:, sl])
            st_ref[h] = s_old * c_dec[:, sl] + _dot_tn(kd[:, sl], vh)
        o_ref[...] = jnp.concatenate(outs, axis=1)


def _scan_index_maps(geo, chunk):
    nb = geo["batch"]
    nl, nc = geo["seq"] // chunk, geo["ctx"] // chunk
    fwd = lambda b, s: (jnp.where(s < nc, nb * nl + b * nc + s, b * nl + s - nc), 0)
    bwd = lambda b, s: (jnp.where(s < nc, nb * nl + b * nc + (nc - 1 - s), b * nl + nl - 1 - (s - nc)), 0)
    return fwd, bwd, nl + nc


def _retention(zr, lg_mat, lg_lane, geo):
    n = zr.shape[0]
    c = RET_CHUNK_LEN
    fwd, bwd, steps = _scan_index_maps(geo, c)
    const = lambda b, s: (0, 0)
    return pl.pallas_call(
        _ret_kernel,
        grid=(geo["batch"], steps),
        in_specs=[pl.BlockSpec((8, LANES), const), pl.BlockSpec((2, BRANCH_W), const),
                  pl.BlockSpec((c, 1024), fwd), pl.BlockSpec((c, 1024), bwd)],
        out_specs=[pl.BlockSpec((c, BRANCH_W), fwd), pl.BlockSpec((c, BRANCH_W), bwd)],
        out_shape=[jax.ShapeDtypeStruct((n, BRANCH_W), F32)] * 2,
        scratch_shapes=[pltpu.VMEM((N_HEADS, HEAD_W, HEAD_W), F32)] * 2,
        compiler_params=_cparams(("parallel", "arbitrary")),
        name="retention",
    )(lg_mat, lg_lane, zr, zr)


def _gla_kernel(gu_ref, gb_ref, tri_ref, bd64_ref, zf_ref, zb_ref, of_ref, ob_ref, sf_ref, sb_ref):
    c, sub = GLA_CHUNK_LEN, GLA_SUB
    nsub = c // sub

    @pl.when(pl.program_id(1) == 0)
    def _():
        sf_ref[...] = jnp.zeros_like(sf_ref)
        sb_ref[...] = jnp.zeros_like(sb_ref)

    bd64 = bd64_ref[...]
    rin = lax.broadcasted_iota(jnp.int32, (c, BRANCH_W), 0) & (sub - 1)
    for rev, z_ref, o_ref, st_ref in ((False, zf_ref, of_ref, sf_ref), (True, zb_ref, ob_ref, sb_ref)):
        d = int(rev)
        q = z_ref[:, 0:256]
        k = z_ref[:, 256:512]
        v = z_ref[:, 512:768]
        logit = _dot(z_ref[:, 1024:GLA_GROUP_W], gu_ref[d]) + gb_ref[d:d + 1, :]
        g = (jnp.minimum(logit, 0.0) - jnp.log1p(jnp.exp(-jnp.abs(logit)))) * GLA_TAU_INV
        g1 = g.astype(BF16)
        r1 = g - g1.astype(F32)
        g2 = r1.astype(BF16)
        g3 = (r1 - g2.astype(F32)).astype(BF16)
        tri = tri_ref[d]
        b = (jnp.dot(tri, g1, preferred_element_type=F32) + jnp.dot(tri, g2, preferred_element_type=F32)
             + jnp.dot(tri, g3, preferred_element_type=F32))
        b_last = b[0:1, :] if rev else b[c - 1:c, :]
        q_state = q * jnp.exp(b)
        k_state = k * jnp.exp(b_last - b)
        e_last = jnp.exp(b_last)

        o_acc = jnp.zeros((c, BRANCH_W), F32)
        for dl in range(sub):
            if dl == 0:
                w = q * k
                vs = v
            else:
                shift = (c - dl) if rev else dl
                ks = pltpu.roll(k, shift, 0)
                bs = pltpu.roll(b, shift, 0)
                vs = pltpu.roll(v, shift, 0)
                valid = (rin + dl <= sub - 1) if rev else (rin >= dl)
                w = jnp.where(valid, q * ks * jnp.exp(jnp.minimum(b - bs, 0.0)), 0.0)
            o_acc = o_acc + jnp.dot(w.astype(BF16), bd64, preferred_element_type=F32) * vs

        head_outs = []
        for h in range(N_HEADS):
            sl = slice(h * HEAD_W, (h + 1) * HEAD_W)
            st = st_ref[h]
            rows = []
            for blk in range(nsub):
                r0, r1_ = blk * sub, (blk + 1) * sub
                o_blk = _dot_nt(q_state[r0:r1_, sl], st)
                if rev and blk < nsub - 1:
                    anchor = b[r1_:r1_ + 1, sl]
                    qa = q[r0:r1_, sl] * jnp.exp(b[r0:r1_, sl] - anchor)
                    ka = k[r1_:c, sl] * jnp.exp(anchor - b[r1_:c, sl])
                    o_blk = o_blk + _dot(_dot_nt(qa, ka), v[r1_:c, sl])
                if (not rev) and blk > 0:
                    anchor = b[r0 - 1:r0, sl]
                    qa = q[r0:r1_, sl] * jnp.exp(b[r0:r1_, sl] - anchor)
                    ka = k[0:r0, sl] * jnp.exp(anchor - b[0:r0, sl])
                    o_blk = o_blk + _dot(_dot_nt(qa, ka), v[0:r0, sl])
                rows.append(o_blk)
            head_outs.append(jnp.concatenate(rows, axis=0))
            st_ref[h] = st * e_last[:, sl] + _dot_tn(v[:, sl], k_state[:, sl])
        o_ref[...] = o_acc + jnp.concatenate(head_outs, axis=1)


def _gla(zg, gate_up_pad, gate_b, tri, bd64, geo):
    n = zg.shape[0]
    c = GLA_CHUNK_LEN
    fwd, bwd, steps = _scan_index_maps(geo, c)
    const2 = lambda b, s: (0, 0)
    const3 = lambda b, s: (0, 0, 0)
    return pl.pallas_call(
        _gla_kernel,
        grid=(geo["batch"], steps),
        in_specs=[pl.BlockSpec((2, LANES, BRANCH_W), const3), pl.BlockSpec((2, BRANCH_W), const2),
                  pl.BlockSpec((2, c, c), const3), pl.BlockSpec((256, 256), const2),
                  pl.BlockSpec((c, GLA_GROUP_W), fwd), pl.BlockSpec((c, GLA_GROUP_W), bwd)],
        out_specs=[pl.BlockSpec((c, BRANCH_W), fwd), pl.BlockSpec((c, BRANCH_W), bwd)],
        out_shape=[jax.ShapeDtypeStruct((n, BRANCH_W), F32)] * 2,
        scratch_shapes=[pltpu.VMEM((N_HEADS, HEAD_W, HEAD_W), F32)] * 2,
        compiler_params=_cparams(("parallel", "arbitrary")),
        name="gla",
    )(gate_up_pad, gate_b, tri, bd64, zg, zg)


def _softmax_pv(q, kl, vl, kc, vc, s_ref, p_ref, latent):
    t = kl.shape[0]
    n_ctx = kc.shape[0]
    if latent:
        s_ref[:, 0:t] = _dot_nt(q, kl)
        s_ref[:, t:t + n_ctx] = _dot_nt(q, kc)
        s = s_ref[...]
        m = jnp.max(s, axis=-1, keepdims=True)
        p = jnp.exp(s - m)
        l = jnp.sum(p, axis=-1, keepdims=True)
        p_ref[...] = p.astype(BF16)
        o = (jnp.dot(p_ref[:, 0:t], vl, preferred_element_type=F32)
             + jnp.dot(p_ref[:, t:t + n_ctx], vc, preferred_element_type=F32))
    else:
        s = _dot_nt(q, kc)
        m = jnp.max(s, axis=-1, keepdims=True)
        p = jnp.exp(s - m)
        l = jnp.sum(p, axis=-1, keepdims=True)
        o = jnp.dot(p.astype(BF16), vc, preferred_element_type=F32)
    return o / l


def _gqa_kernel(q_ref, kl_ref, vl_ref, kc_ref, vc_ref, o_ref, s_ref, p_ref, *, n_lat_tiles, need_ctx):
    def run(latent):
        outs = []
        for h in range(N_HEADS):
            kv = slice((h // 2) * HEAD_W, (h // 2 + 1) * HEAD_W)
            outs.append(_softmax_pv(q_ref[:, h * HEAD_W:(h + 1) * HEAD_W], kl_ref[:, kv], vl_ref[:, kv],
                                    kc_ref[:, kv], vc_ref[:, kv], s_ref, p_ref, latent))
        o_ref[...] = jnp.concatenate(outs, axis=1).astype(o_ref.dtype)

    if need_ctx:
        t = pl.program_id(1)
        pl.when(t < n_lat_tiles)(lambda: run(True))
        pl.when(t >= n_lat_tiles)(lambda: run(False))
    else:
        run(True)


def _dif_kernel(lam_ref, sub_ref, bd64_ref, q_ref, kl_ref, vl_ref, kc_ref, vc_ref, o_ref, s_ref, p_ref,
                *, n_lat_tiles, need_ctx, out_scale):
    def run(latent):
        outs = []
        for h in range(N_HEADS):
            vs = slice(h * HEAD_W, (h + 1) * HEAD_W)
            maps = []
            for m in range(2):
                qs = slice((2 * h + m) * DIF_QK, (2 * h + m + 1) * DIF_QK)
                maps.append(_softmax_pv(q_ref[:, qs], kl_ref[:, qs], vl_ref[:, vs], kc_ref[:, qs], vc_ref[:, vs],
                                        s_ref, p_ref, latent))
            outs.append(maps[0] - lam_ref[:, vs] * maps[1])
        o = _group_rms(jnp.concatenate(outs, axis=1), bd64_ref[...], sub_ref[...], HEAD_W) * out_scale
        o_ref[...] = o.astype(o_ref.dtype)

    if need_ctx:
        t = pl.program_id(1)
        pl.when(t < n_lat_tiles)(lambda: run(True))
        pl.when(t >= n_lat_tiles)(lambda: run(False))
    else:
        run(True)


def _attention_specs(geo, need_ctx, qw, kw, vw):
    nb, t, n_ctx, tq = geo["batch"], geo["seq"], geo["ctx"], geo["tq"]
    nql, nqc = t // tq, n_ctx // tq
    qmap = lambda b, i: (jnp.where(i < nql, b * nql + i, nb * nql + b * nqc + (i - nql)), 0)
    lat = lambda b, i: (b, 0)
    ctx = lambda b, i: (nb * (t // n_ctx) + b, 0)
    in_specs = [pl.BlockSpec((tq, qw), qmap), pl.BlockSpec((t, kw), lat), pl.BlockSpec((t, vw), lat),
                pl.BlockSpec((n_ctx, kw), ctx), pl.BlockSpec((n_ctx, vw), ctx)]
    grid = (nb, nql + (nqc if need_ctx else 0))
    scratch = [pltpu.VMEM((tq, t + n_ctx), F32), pltpu.VMEM((tq, t + n_ctx), BF16)]
    return grid, in_specs, pl.BlockSpec((tq, BRANCH_W), qmap), scratch, nql


def _gqa(cq, ck, cv, geo, need_ctx):
    grid, in_specs, out_spec, scratch, nql = _attention_specs(geo, need_ctx, 256, 128, 128)
    return pl.pallas_call(
        functools.partial(_gqa_kernel, n_lat_tiles=nql, need_ctx=need_ctx),
        grid=grid, in_specs=in_specs, out_specs=out_spec,
        out_shape=jax.ShapeDtypeStruct((cq.shape[0], BRANCH_W), BF16),
        scratch_shapes=scratch,
        compiler_params=_cparams(("parallel", "parallel")),
        name="gqa",
    )(cq, ck, cv, ck, cv)


def _dif(dq, dk, dv, lam_lane, subln, bd64, geo, need_ctx, out_scale):
    grid, in_specs, out_spec, scratch, nql = _attention_specs(geo, need_ctx, 256, 256, 256)
    const = lambda b, i: (0, 0)
    pre = [pl.BlockSpec((1, BRANCH_W), const), pl.BlockSpec((1, BRANCH_W), const), pl.BlockSpec((256, 256), const)]
    return pl.pallas_call(
        functools.partial(_dif_kernel, n_lat_tiles=nql, need_ctx=need_ctx, out_scale=out_scale),
        grid=grid, in_specs=pre + in_specs, out_specs=out_spec,
        out_shape=jax.ShapeDtypeStruct((dq.shape[0], BRANCH_W), BF16),
        scratch_shapes=scratch,
        compiler_params=_cparams(("parallel", "parallel")),
        name="diffattn",
    )(lam_lane, subln, bd64, dq, dk, dv, dk, dv)


def _merge_kernel(x_ref, mod_ref, n1_ref, wg_ref, wb_ref, wo_ref, bd64_ref, gnw_ref,
                  rof_ref, rob_ref, rg_ref, gof_ref, gob_ref, gr_ref, yc_ref, yd_ref, o_ref):
    x = x_ref[...]
    d = x.shape[1]
    hb = _adaln(x, n1_ref[...], mod_ref[0:1, :], mod_ref[1:2, :]).astype(BF16)
    bd64 = bd64_ref[...]
    ya = _group_rms(rof_ref[...] + rob_ref[...], bd64, None, HEAD_W) * _silu(rg_ref[...])
    yb = _group_rms(gof_ref[...] + gob_ref[...], bd64, gnw_ref[...], HEAD_W) * _silu(gr_ref[...])
    ys = (ya.astype(BF16), yb.astype(BF16), yc_ref[...], yd_ref[...])
    m = None
    for i in range(4):
        gate = _sigmoid(jnp.dot(hb, wg_ref[:, i * d:(i + 1) * d], preferred_element_type=F32))
        term = gate * jnp.dot(ys[i], wb_ref[i], preferred_element_type=F32)
        m = term if m is None else m + term
    o_ref[...] = x + mod_ref[2:3, :] * _dot(m, wo_ref[...])


def _merge(xf, mod, norm1, w_gate, w_branch, w_out, bd64, gla_norm, rof, rob, zr, gof, gob, zg, yc, yd, geo,
           need_ctx):
    n, d = xf.shape
    tm = geo["tm"]
    nlt, tps, nb = geo["lat_tiles"], geo["tiles_per_seq"], geo["batch"]
    tiles = n // tm if need_ctx else nlt
    row = lambda i: (i, 0)
    col3 = lambda i: (i, 3)
    const = lambda i: (0, 0)
    bw = BRANCH_W
    return pl.pallas_call(
        _merge_kernel,
        grid=(tiles,),
        in_specs=[pl.BlockSpec((tm, d), row),
                  pl.BlockSpec((None, 6, d), lambda i: (jnp.where(i < nlt, i // tps, nb), 0, 0)),
                  pl.BlockSpec((1, d), const),
                  pl.BlockSpec((d, 4 * d), const),
                  pl.BlockSpec((4, bw, d), lambda i: (0, 0, 0)),
                  pl.BlockSpec((d, d), const),
                  pl.BlockSpec((256, 256), const),
                  pl.BlockSpec((1, bw), const),
                  pl.BlockSpec((tm, bw), row), pl.BlockSpec((tm, bw), row), pl.BlockSpec((tm, bw), col3),
                  pl.BlockSpec((tm, bw), row), pl.BlockSpec((tm, bw), row), pl.BlockSpec((tm, bw), col3),
                  pl.BlockSpec((tm, bw), row), pl.BlockSpec((tm, bw), row)],
        out_specs=pl.BlockSpec((tm, d), row),
        out_shape=jax.ShapeDtypeStruct((n, d), F32),
        input_output_aliases={0: 0},
        compiler_params=_cparams(("parallel",)),
        name="merge",
    )(xf, mod, norm1, w_gate, w_branch, w_out, bd64, gla_norm, rof, rob, zr, gof, gob, zg, yc, yd)


HALO = 16


def _ffn_kernel(x_ref, xp_ref, xn_ref, mod_ref, n2_ref, wa_ref, wg_ref, cw_ref, wd_ref, o_ref, h_ref, acc_ref,
                *, tm, n_lat_rows, seq, n_ctx, n_f):
    i = pl.program_id(0)
    j = pl.program_id(1)

    @pl.when(j == 0)
    def _():
        def hn(xx):
            return _adaln(xx, n2_ref[...], mod_ref[3:4, :], mod_ref[4:5, :]).astype(BF16)
        h_ref[0:HALO, :] = hn(xp_ref[...])
        h_ref[HALO:HALO + tm, :] = hn(x_ref[...])
        h_ref[HALO + tm:2 * HALO + tm, :] = hn(xn_ref[...])
        acc_ref[...] = jnp.zeros_like(acc_ref)

    u = jnp.dot(h_ref[...], wa_ref[...], preferred_element_type=F32)
    ext = tm + 2 * HALO
    up = pltpu.roll(u, 1, 0)[HALO:HALO + tm]
    mid = u[HALO:HALO + tm]
    dn = pltpu.roll(u, ext - 1, 0)[HALO:HALO + tm]
    r = i * tm + lax.broadcasted_iota(jnp.int32, (tm, 1), 0)
    is_lat = r < n_lat_rows
    pos = jnp.where(is_lat, r & (seq - 1), (r - n_lat_rows) & (n_ctx - 1))
    keep_up = jnp.where(pos == 0, 0.0, 1.0)
    keep_dn = jnp.where(pos == jnp.where(is_lat, seq - 1, n_ctx - 1), 0.0, 1.0)
    a = ((up * keep_up) * cw_ref[0:1, :] + mid * cw_ref[1:2, :]
         + (dn * keep_dn) * cw_ref[2:3, :] + cw_ref[3:4, :])
    gate = jnp.dot(h_ref[HALO:HALO + tm, :], wg_ref[...], preferred_element_type=F32)
    acc_ref[...] += _dot(_silu(a) * gate, wd_ref[...])

    @pl.when(j == n_f - 1)
    def _():
        o_ref[...] = x_ref[...] + mod_ref[5:6, :] * acc_ref[...]


def _ffn(xf, mod, norm2, w_up, conv_pack, w_down, geo, need_ctx):
    n, d = xf.shape
    tm, tf = geo["tm_ffn"], geo["tf"]
    nb, t, n_ctx = geo["batch"], geo["seq"], geo["ctx"]
    nlt = nb * t // tm
    tps = t // tm
    tiles = n // tm if need_ctx else nlt
    n_f = D_FF_W // tf
    hb = tm // HALO
    last_blk = n // HALO - 1
    row = lambda i, j: (i, 0)
    return pl.pallas_call(
        functools.partial(_ffn_kernel, tm=tm, n_lat_rows=nb * t, seq=t, n_ctx=n_ctx, n_f=n_f),
        grid=(tiles, n_f),
        in_specs=[pl.BlockSpec((tm, d), row),
                  pl.BlockSpec((HALO, d), lambda i, j: (jnp.maximum(i * hb - 1, 0), 0)),
                  pl.BlockSpec((HALO, d), lambda i, j: (jnp.minimum((i + 1) * hb, last_blk), 0)),
                  pl.BlockSpec((None, 6, d), lambda i, j: (jnp.where(i < nlt, i // tps, nb), 0, 0)),
                  pl.BlockSpec((1, d), lambda i, j: (0, 0)),
                  pl.BlockSpec((d, tf), lambda i, j: (0, j)),
                  pl.BlockSpec((d, tf), lambda i, j: (0, n_f + j)),
                  pl.BlockSpec((8, tf), lambda i, j: (0, j)),
                  pl.BlockSpec((tf, d), lambda i, j: (j, 0))],
        out_specs=pl.BlockSpec((tm, d), row),
        out_shape=jax.ShapeDtypeStruct((n, d), F32),
        scratch_shapes=[pltpu.VMEM((tm + 2 * HALO, d), BF16), pltpu.VMEM((tm, d), F32)],
        compiler_params=_cparams(("parallel", "arbitrary")),
        name="convffn",
    )(xf, xf, xf, mod, norm2, w_up, w_up, conv_pack, w_down)


def _rope_tables(t, tm):
    rows = t // GRID_WIDTH

    def axial(dim):
        axis_dim = dim // 2
        freqs = ROPE_BASE ** (-jnp.arange(0, axis_dim, 2, dtype=F32) / axis_dim)
        row = jnp.repeat(jnp.arange(rows, dtype=F32), GRID_WIDTH)
        col = jnp.tile(jnp.arange(GRID_WIDTH, dtype=F32), rows)
        ang = jnp.concatenate([row[:, None] * freqs, col[:, None] * freqs], axis=-1)
        return jnp.cos(ang), jnp.sin(ang)

    freqs = ROPE_BASE ** (-jnp.linspace(0.0, 1.0, HEAD_W // 2, dtype=F32))
    ang = jnp.arange(t, dtype=F32)[:, None] * freqs
    parts = []
    for cos, sin in ((jnp.cos(ang), jnp.sin(ang)), axial(HEAD_W), axial(DIF_QK)):
        reps = LANES // (2 * cos.shape[1])
        parts.append(jnp.tile(cos, (1, 2 * reps)))
        parts.append(jnp.tile(jnp.concatenate([-sin, sin], axis=1), (1, reps)))
    table = jnp.concatenate(parts, axis=1)
    ident = jnp.tile(jnp.concatenate([jnp.ones((1, LANES), F32), jnp.zeros((1, LANES), F32)], axis=1), (tm, 3))
    return jnp.concatenate([table, ident], axis=0)


def _block_diag_ones(width, group):
    idx = jnp.arange(width) // group
    return (idx[:, None] == idx[None, :]).astype(BF16)


def kernel(x, c, ctx, c_ctx, w_mod, b_mod, norm1, norm2, w_in, ret_decay, gla_gate_up, gla_gate_b, gla_norm,
           gqa_qnorm, gqa_knorm, dif_qnorm, dif_knorm, dif_lambda, dif_subln, w_branch, w_out, w_up, conv_w,
           conv_b, w_down):
    nb, t, d = x.shape
    n_ctx = ctx.shape[1]
    depth = w_mod.shape[0]
    assert t & (t - 1) == 0 and n_ctx & (n_ctx - 1) == 0 and t % n_ctx == 0 and n_ctx % RET_CHUNK_LEN == 0
    tm = min(512, nb * n_ctx)
    assert (nb * n_ctx) % tm == 0 and t % tm == 0 and nb + 1 <= 8
    tq = min(256, n_ctx)
    geo = dict(batch=nb, seq=t, ctx=n_ctx, tm=tm, tm_ffn=tm, tf=D_FF_W // 2, tq=tq,
               lat_tiles=nb * t // tm, tiles_per_seq=t // tm)

    xf = jnp.concatenate([x.reshape(nb * t, d), ctx.reshape(nb * n_ctx, d)], axis=0)
    cc = jnp.zeros((8, d), F32).at[0:nb].set(c).at[nb].set(c_ctx)
    mod_all = _modulation(cc, w_mod, b_mod).reshape(depth, 8, 6, d)

    table = _rope_tables(t, tm)
    bd64 = _block_diag_ones(256, HEAD_W)
    bd32 = _block_diag_ones(256, DIF_QK)
    ti = jnp.arange(GLA_CHUNK_LEN)
    tri = jnp.stack([ti[:, None] >= ti[None, :], ti[:, None] <= ti[None, :]]).astype(BF16)
    n_mix = 2080
    n_gate0 = 3360

    for l in range(depth):
        need_ctx = l < depth - 1
        w = w_in[l]
        w_proj = jnp.concatenate([w[:, :n_mix], jnp.zeros((d, LANES - 2 * GLA_RANK_W), F32),
                                  w[:, n_mix:n_gate0]], axis=1).astype(BF16)
        w_gate = w[:, n_gate0:].astype(BF16)
        qk_norms = jnp.zeros((8, 256), F32)
        qk_norms = qk_norms.at[0].set(jnp.tile(gqa_qnorm[l], 4)).at[1].set(jnp.tile(gqa_knorm[l], 4))
        qk_norms = qk_norms.at[2].set(jnp.tile(dif_qnorm[l], 8)).at[3].set(jnp.tile(dif_knorm[l], 8))
        mod = mod_all[l]
        n1 = norm1[l][None, :]
        n2 = norm2[l][None, :]

        zr, zg, cq, ck, cv, dq, dk, dv = _inproj(xf, mod, n1, w_proj, table, bd64, bd32, qk_norms, geo)

        log_gamma = jax.nn.log_sigmoid(ret_decay[l].astype(F32))
        lg_mat = jnp.broadcast_to(log_gamma.reshape(2 * N_HEADS, 1), (2 * N_HEADS, LANES))
        lg_lane = jnp.repeat(log_gamma, HEAD_W, axis=1)
        rof, rob = _retention(zr, lg_mat, lg_lane, geo)

        gu = jnp.zeros((2, LANES, BRANCH_W), F32)
        gu = gu.at[0, 0:GLA_RANK_W].set(gla_gate_up[l, 0]).at[1, GLA_RANK_W:2 * GLA_RANK_W].set(gla_gate_up[l, 1])
        gof, gob = _gla(zg, gu.astype(BF16), gla_gate_b[l], tri, bd64, geo)

        yc = _gqa(cq, ck, cv, geo, need_ctx)

        lam_init = 0.8 - 0.6 * math.exp(-0.3 * l)
        lv = dif_lambda[l].astype(F32)
        lam = jnp.exp(jnp.sum(lv[0] * lv[1])) - jnp.exp(jnp.sum(lv[2] * lv[3])) + lam_init
        lam_lane = jnp.broadcast_to(lam, (1, BRANCH_W)).astype(F32)
        subln = jnp.tile(dif_subln[l], 4)[None, :]
        yd = _dif(dq, dk, dv, lam_lane, subln, bd64, geo, need_ctx, 1.0 - lam_init)

        gnw = jnp.tile(gla_norm[l], 4)[None, :]
        xf = _merge(xf, mod, n1, w_gate, w_branch[l].astype(BF16), w_out[l].astype(BF16), bd64, gnw,
                    rof, rob, zr, gof, gob, zg, yc, yd, geo, need_ctx)

        conv_pack = jnp.zeros((8, D_FF_W), F32).at[0:3].set(conv_w[l]).at[3].set(conv_b[l])
        xf = _ffn(xf, mod, n2, w_up[l].astype(BF16), conv_pack, w_down[l].astype(BF16), geo, need_ctx)

    return xf[:nb * t].reshape(nb, t, d)
```

```python
import functools
import math

import jax
import jax.numpy as jnp
from jax import lax
from jax.experimental import pallas as pl
from jax.experimental.pallas import tpu as pltpu

F32 = jnp.float32
BF16 = jnp.bfloat16

N_HEADS = 4
HEAD_W = 64
BRANCH_W = N_HEADS * HEAD_W
DIF_QK = 32
RET_CHUNK_LEN = 128
GLA_CHUNK_LEN = 64
GLA_SUB = 16
GLA_RANK_W = 16
GLA_TAU_INV = 1.0 / 16.0
GRID_WIDTH = 64
ROPE_BASE = 10000.0
EPS = 1e-6
D_FF_W = 2816
LOG2E = 1.4426950408889634

LANES = 128
VMEM_LIMIT_BYTES = 56 * 1024 * 1024

RET_OFF, GLA_OFF, GQA_OFF, DIF_OFF, PROJ_W = 0, 1024, 2176, 2688, 3456
GLA_GROUP_W = GQA_OFF - GLA_OFF
HALO = 16
SOFTMAX_ROWS = 128


def _cparams(sem):
    return pltpu.CompilerParams(dimension_semantics=sem, vmem_limit_bytes=VMEM_LIMIT_BYTES)


def _resident(shape):
    nd = len(shape)
    return pl.BlockSpec(shape, lambda *_: (0,) * nd, pipeline_mode=pl.Buffered(1))


def _dot(a, b):
    return jnp.dot(a.astype(BF16), b.astype(BF16), preferred_element_type=F32)


def _dot_nt(a, b):
    return lax.dot_general(a.astype(BF16), b.astype(BF16), (((1,), (1,)), ((), ())),
                           preferred_element_type=F32)


def _dot_tn(a, b):
    return lax.dot_general(a.astype(BF16), b.astype(BF16), (((0,), (0,)), ((), ())),
                           preferred_element_type=F32)


def _sigmoid(x):
    return 1.0 / (1.0 + jnp.exp(-x))


def _silu(x):
    return x * _sigmoid(x)


def _adaln(x, norm_w, shift, scale):
    h = x * lax.rsqrt(jnp.mean(x * x, axis=-1, keepdims=True) + EPS) * norm_w
    return h * (1.0 + scale) + shift


def _group_rms(x, bd, weight, group):
    x2 = x * x
    hi = x2.astype(BF16)
    lo = (x2 - hi.astype(F32)).astype(BF16)
    ss = jnp.dot(hi, bd, preferred_element_type=F32) + jnp.dot(lo, bd, preferred_element_type=F32)
    y = x * lax.rsqrt(ss * (1.0 / group) + EPS)
    return y if weight is None else y * weight


def _rope(x, cos, sin_signed, half):
    w = x.shape[1]
    reps = w // LANES
    if reps > 1:
        cos = jnp.concatenate([cos] * reps, axis=1)
        sin_signed = jnp.concatenate([sin_signed] * reps, axis=1)
    lane = lax.broadcasted_iota(jnp.int32, x.shape, 1)
    lower = (lane & (2 * half - 1)) < half
    partner = jnp.where(lower, pltpu.roll(x, w - half, 1), pltpu.roll(x, half, 1))
    return x * cos + partner * sin_signed


def _mod_kernel(c_ref, w_ref, b_ref, o_ref):
    o_ref[...] = _dot(_silu(c_ref[...]), w_ref[...]) + b_ref[...]


def _modulation(cc, w_mod, b_mod):
    n_layers, d, width = w_mod.shape
    tn = 1024
    return pl.pallas_call(
        _mod_kernel,
        grid=(n_layers, width // tn),
        in_specs=[pl.BlockSpec((8, d), lambda l, j: (0, 0)),
                  pl.BlockSpec((None, d, tn), lambda l, j: (l, 0, j)),
                  pl.BlockSpec((None, 1, tn), lambda l, j: (l, 0, j))],
        out_specs=pl.BlockSpec((None, 8, tn), lambda l, j: (l, 0, j)),
        out_shape=jax.ShapeDtypeStruct((n_layers, 8, width), F32),
        compiler_params=_cparams(("parallel", "parallel")),
        name="modulation",
    )(cc, w_mod, b_mod.reshape(n_layers, 1, width))


def _mod_spec(nb, d, n_lat_tiles):
    return pl.BlockSpec((None, nb, 6, d), lambda i, *_: (jnp.where(i < n_lat_tiles, 0, 1), 0, 0, 0))


def _inproj_kernel(x_ref, mod_ref, n1_ref, w_ref, tab_ref, bd64_ref, bd32_ref, nw_ref,
                   zr_ref, zg_ref, cq_ref, ck_ref, cv_ref, dq_ref, dk_ref, dv_ref):
    nb, rt, d = x_ref.shape
    m = nb * rt
    hb = _adaln(x_ref[...], n1_ref[...], mod_ref[:, 0:1, :], mod_ref[:, 1:2, :]).astype(BF16).reshape(m, d)

    def table(lo):
        t = tab_ref[:, lo:lo + LANES]
        return jnp.concatenate([t] * nb, axis=0)

    r_cos, r_sin, g_cos, g_sin, d_cos, d_sin = (table(i * LANES) for i in range(6))

    def put(ref, lo, val):
        ref[:, :, lo:lo + val.shape[1]] = val.reshape(nb, rt, val.shape[1]).astype(ref.dtype)

    zr = jnp.dot(hb, w_ref[:, RET_OFF:GLA_OFF], preferred_element_type=F32)
    put(zr_ref, 0, _rope(zr[:, 0:256], r_cos, r_sin, 32))
    put(zr_ref, 256, _rope(zr[:, 256:512], r_cos, r_sin, 32) * (HEAD_W ** -0.5))
    put(zr_ref, 512, zr[:, 512:1024])

    zg = jnp.dot(hb, w_ref[:, GLA_OFF:GQA_OFF], preferred_element_type=F32)
    put(zg_ref, 0, zg[:, 0:256] * (HEAD_W ** -0.5))
    put(zg_ref, 256, zg[:, 256:GLA_GROUP_W])

    zc = jnp.dot(hb, w_ref[:, GQA_OFF:DIF_OFF], preferred_element_type=F32)
    bd64 = bd64_ref[...]
    q = _group_rms(zc[:, 0:256], bd64, nw_ref[0:1, :], HEAD_W)
    put(cq_ref, 0, _rope(q, g_cos, g_sin, 32) * (HEAD_W ** -0.5 * LOG2E))
    k = _group_rms(zc[:, 256:384], bd64[0:128, 0:128], nw_ref[1:2, 0:128], HEAD_W)
    put(ck_ref, 0, _rope(k, g_cos, g_sin, 32))
    put(cv_ref, 0, zc[:, 384:512])

    zd = jnp.dot(hb, w_ref[:, DIF_OFF:PROJ_W], preferred_element_type=F32)
    bd32 = bd32_ref[...]
    q = _group_rms(zd[:, 0:256], bd32, nw_ref[2:3, :], DIF_QK)
    put(dq_ref, 0, _rope(q, d_cos, d_sin, 16) * (DIF_QK ** -0.5 * LOG2E))
    k = _group_rms(zd[:, 256:512], bd32, nw_ref[3:4, :], DIF_QK)
    put(dk_ref, 0, _rope(k, d_cos, d_sin, 16))
    put(dv_ref, 0, zd[:, 512:768])


def _inproj(xs, mod, norm1, w_proj, table, bd64, bd32, qk_norms, geo):
    nb, s, d = xs.shape
    rt = geo["rt"]
    row = lambda i: (0, i, 0)
    widths = (1024, GLA_GROUP_W, 256, 128, 128, 256, 256, 256)
    dtypes = (F32, F32, BF16, BF16, BF16, BF16, BF16, BF16)
    return pl.pallas_call(
        _inproj_kernel,
        grid=(s // rt,),
        in_specs=[pl.BlockSpec((nb, rt, d), row),
                  _mod_spec(nb, d, geo["seq"] // rt),
                  _resident((1, d)),
                  _resident((d, PROJ_W)),
                  pl.BlockSpec((rt, 768), lambda i: (i, 0)),
                  _resident((256, 256)),
                  _resident((256, 256)),
                  _resident((8, 256))],
        out_specs=[pl.BlockSpec((nb, rt, w), row) for w in widths],
        out_shape=[jax.ShapeDtypeStruct((nb, s, w), t) for w, t in zip(widths, dtypes)],
        compiler_params=_cparams(("parallel",)),
        name="inproj",
    )(xs, mod, norm1, w_proj, table, bd64, bd32, qk_norms)


def _scan_index_maps(geo, chunk):
    nl, nc = geo["seq"] // chunk, geo["ctx"] // chunk
    fwd = lambda s: (0, jnp.where(s < nc, nl + s, s - nc), 0)
    bwd = lambda s: (0, jnp.where(s < nc, nl + nc - 1 - s, nl - 1 - (s - nc)), 0)
    return fwd, bwd, nl + nc


def _ret_kernel(lgm_ref, lgl_ref, zf_ref, zb_ref, of_ref, ob_ref, st_ref):
    nb = zf_ref.shape[0]
    c = RET_CHUNK_LEN

    @pl.when(pl.program_id(0) == 0)
    def _():
        st_ref[...] = jnp.zeros_like(st_ref)

    ii = lax.broadcasted_iota(jnp.int32, (c, c), 0)
    jj = lax.broadcasted_iota(jnp.int32, (c, c), 1)
    ri = lax.broadcasted_iota(jnp.int32, (c, BRANCH_W), 0).astype(F32)
    for rev, z_ref, o_ref in ((False, zf_ref, of_ref), (True, zb_ref, ob_ref)):
        d = int(rev)
        dist = ((jj - ii) if rev else (ii - jj)).astype(F32)
        lgl = lgl_ref[d:d + 1, :]
        q_dec = jnp.exp(lgl * ((c - ri) if rev else (ri + 1.0)))
        k_dec = jnp.exp(lgl * (ri if rev else (c - 1.0 - ri)))
        c_dec = jnp.exp(lgl * float(c))
        decs = []
        for h in range(N_HEADS):
            lg_h = lgm_ref[d * N_HEADS + h:d * N_HEADS + h + 1, :]
            decs.append(jnp.where(dist >= 0, jnp.exp(lg_h * jnp.maximum(dist, 0.0)), 0.0))
        for b in range(nb):
            q = z_ref[b, :, 0:256]
            k = z_ref[b, :, 256:512]
            v = z_ref[b, :, 512:768]
            kd = k * k_dec
            outs = []
            for h in range(N_HEADS):
                sl = slice(h * HEAD_W, (h + 1) * HEAD_W)
                qh = q[:, sl].astype(BF16)
                vh = v[:, sl].astype(BF16)
                sc = _dot_nt(qh, k[:, sl]) * decs[h]
                s_old = st_ref[d, b, h]
                outs.append(_dot(sc, vh) + _dot(qh, s_old) * q_dec[:, sl])
                st_ref[d, b, h] = s_old * c_dec[:, sl] + _dot_tn(kd[:, sl], vh)
            o_ref[b] = jnp.concatenate(outs, axis=1)


def _retention(zr, lg_mat, lg_lane, geo):
    nb, s, _ = zr.shape
    c = RET_CHUNK_LEN
    fwd, bwd, steps = _scan_index_maps(geo, c)
    return pl.pallas_call(
        _ret_kernel,
        grid=(steps,),
        in_specs=[_resident((8, LANES)), _resident((2, BRANCH_W)),
                  pl.BlockSpec((nb, c, 1024), fwd), pl.BlockSpec((nb, c, 1024), bwd)],
        out_specs=[pl.BlockSpec((nb, c, BRANCH_W), fwd), pl.BlockSpec((nb, c, BRANCH_W), bwd)],
        out_shape=[jax.ShapeDtypeStruct((nb, s, BRANCH_W), F32)] * 2,
        scratch_shapes=[pltpu.VMEM((2, nb, N_HEADS, HEAD_W, HEAD_W), F32)],
        compiler_params=_cparams(("arbitrary",)),
        name="retention",
    )(lg_mat, lg_lane, zr, zr)


GLA_FACTOR_MIN_LOG_DECAY = -40.0
GLA_FACTOR_MAX_KEY = 1e15


def _gla_prepare(z_ref, d, gu_ref, gb_ref, tri_ref):
    nb, c, _ = z_ref.shape
    m = nb * c
    q = z_ref[:, :, 0:256].reshape(m, BRANCH_W)
    k = z_ref[:, :, 256:512].reshape(m, BRANCH_W)
    v = z_ref[:, :, 512:768].reshape(m, BRANCH_W)
    logit = _dot(z_ref[:, :, 1024:GLA_GROUP_W].reshape(m, LANES), gu_ref[d]) + gb_ref[d:d + 1, :]
    g = (jnp.minimum(logit, 0.0) - jnp.log(1.0 + jnp.exp(-jnp.abs(logit)))) * GLA_TAU_INV
    g1 = g.astype(BF16)
    r1 = g - g1.astype(F32)
    g2 = r1.astype(BF16)
    g3 = (r1 - g2.astype(F32)).astype(BF16)
    tri = tri_ref[d]
    bcum = (jnp.dot(tri, g1, preferred_element_type=F32) + jnp.dot(tri, g2, preferred_element_type=F32)
            + jnp.dot(tri, g3, preferred_element_type=F32))
    return q, k, v, bcum


def _gla_state_step(st_ref, d, b, q_state_b, kb, vb, bb, rev, bd64_f):
    c = kb.shape[0]
    b_last = bb[0:1, :] if rev else bb[c - 1:c, :]
    st = st_ref[d, b]
    o_b = _dot_nt(q_state_b, st)
    st_ref[d, b] = st * jnp.exp(b_last) + _dot_tn(vb, kb * jnp.exp(b_last - bb)) * bd64_f
    return o_b


def _gla_factored(d, rev, q, k, v, bcum, o_ref, st_ref, hm64_ref, causal_ref, bd64_f):
    nb, c, _ = o_ref.shape
    q_state = q * jnp.exp(bcum)
    k_inv = k * jnp.exp(-bcum)
    hm = hm64_ref[...]
    causal = causal_ref[d]
    outs = []
    for b in range(nb):
        rows = slice(b * c, (b + 1) * c)
        qb, vb = q_state[rows], v[rows]
        q4 = jnp.concatenate([qb] * N_HEADS, axis=0) * hm
        att = jnp.where(causal > 0.0, _dot_nt(q4, k_inv[rows]), 0.0)
        o4 = _dot(att, vb) * hm
        o_b = o4[0:c] + o4[c:2 * c] + o4[2 * c:3 * c] + o4[3 * c:4 * c]
        outs.append(o_b + _gla_state_step(st_ref, d, b, qb, k[rows], vb, bcum[rows], rev, bd64_f))
    o_ref[...] = jnp.concatenate(outs, axis=0).reshape(nb, c, BRANCH_W)


def _gla_pairwise(d, rev, q, k, v, bcum, o_ref, st_ref, w_ref, hm16_ref, bd64, bd64_f):
    nb, c, _ = o_ref.shape
    sub = GLA_SUB
    nsub = c // sub
    m = nb * c
    head_mask = hm16_ref[...]
    rin = lax.broadcasted_iota(jnp.int32, (m, BRANCH_W), 0) & (sub - 1)
    for dl in range(sub):
        if dl == 0:
            w = q * k
        else:
            shift = (m - dl) if rev else dl
            ks = pltpu.roll(k, shift, 0)
            bs = pltpu.roll(bcum, shift, 0)
            valid = (rin + dl <= sub - 1) if rev else (rin >= dl)
            w = jnp.where(valid, q * ks * jnp.exp(jnp.minimum(bcum - bs, 0.0)), 0.0)
        w_ref[dl * m:(dl + 1) * m, :] = w.astype(BF16)
    att = jnp.dot(w_ref[...], bd64, preferred_element_type=F32)
    o_acc = att[0:m] * v
    for dl in range(1, sub):
        o_acc = o_acc + att[dl * m:(dl + 1) * m] * pltpu.roll(v, (m - dl) if rev else dl, 0)

    q_state = q * jnp.exp(bcum)
    outs = []
    for b in range(nb):
        r0 = b * c
        qb, kb, vb, bb = q[r0:r0 + c], k[r0:r0 + c], v[r0:r0 + c], bcum[r0:r0 + c]
        rows = []
        for blk in range(nsub):
            a0, a1 = blk * sub, (blk + 1) * sub
            if rev and blk < nsub - 1:
                anchor, lo, hi = bb[a1:a1 + 1, :], a1, c
            elif (not rev) and blk > 0:
                anchor, lo, hi = bb[a0 - 1:a0, :], 0, a0
            else:
                rows.append(jnp.zeros((sub, BRANCH_W), F32))
                continue
            qa = qb[a0:a1] * jnp.exp(bb[a0:a1] - anchor)
            ka = kb[lo:hi] * jnp.exp(anchor - bb[lo:hi])
            q4 = jnp.concatenate([qa] * N_HEADS, axis=0) * head_mask
            o4 = _dot(_dot_nt(q4, ka), vb[lo:hi]) * head_mask
            rows.append(o4[0:sub] + o4[sub:2 * sub] + o4[2 * sub:3 * sub] + o4[3 * sub:4 * sub])
        o_b = _gla_state_step(st_ref, d, b, q_state[r0:r0 + c], kb, vb, bb, rev, bd64_f)
        outs.append(o_b + jnp.concatenate(rows, axis=0))
    o_ref[...] = (o_acc + jnp.concatenate(outs, axis=0)).reshape(nb, c, BRANCH_W)


def _gla_kernel(gu_ref, gb_ref, tri_ref, bd64_ref, hm16_ref, hm64_ref, causal_ref, zf_ref, zb_ref,
                of_ref, ob_ref, st_ref, w_ref):
    @pl.when(pl.program_id(0) == 0)
    def _():
        st_ref[...] = jnp.zeros_like(st_ref)

    bd64 = bd64_ref[...]
    bd64_f = bd64.astype(F32)
    fwd = _gla_prepare(zf_ref, 0, gu_ref, gb_ref, tri_ref)
    bwd = _gla_prepare(zb_ref, 1, gu_ref, gb_ref, tri_ref)
    min_decay = jnp.minimum(jnp.min(fwd[3]), jnp.min(bwd[3]))
    max_key = jnp.maximum(jnp.max(jnp.abs(fwd[1])), jnp.max(jnp.abs(bwd[1])))
    factor_ok = jnp.logical_and(min_decay > GLA_FACTOR_MIN_LOG_DECAY, max_key < GLA_FACTOR_MAX_KEY)

    @pl.when(factor_ok)
    def _():
        _gla_factored(0, False, *fwd, of_ref, st_ref, hm64_ref, causal_ref, bd64_f)
        _gla_factored(1, True, *bwd, ob_ref, st_ref, hm64_ref, causal_ref, bd64_f)

    @pl.when(jnp.logical_not(factor_ok))
    def _():
        _gla_pairwise(0, False, *fwd, of_ref, st_ref, w_ref, hm16_ref, bd64, bd64_f)
        _gla_pairwise(1, True, *bwd, ob_ref, st_ref, w_ref, hm16_ref, bd64, bd64_f)


def _gla(zg, gate_up_pad, gate_b, tri, bd64, hm16, hm64, causal, geo):
    nb, s, _ = zg.shape
    c = GLA_CHUNK_LEN
    fwd, bwd, steps = _scan_index_maps(geo, c)
    return pl.pallas_call(
        _gla_kernel,
        grid=(steps,),
        in_specs=[_resident((2, LANES, BRANCH_W)), _resident((2, BRANCH_W)),
                  _resident((2, nb * c, nb * c)), _resident((256, 256)), _resident((N_HEADS * GLA_SUB, BRANCH_W)),
                  _resident((N_HEADS * c, BRANCH_W)), _resident((2, N_HEADS * c, c)),
                  pl.BlockSpec((nb, c, GLA_GROUP_W), fwd), pl.BlockSpec((nb, c, GLA_GROUP_W), bwd)],
        out_specs=[pl.BlockSpec((nb, c, BRANCH_W), fwd), pl.BlockSpec((nb, c, BRANCH_W), bwd)],
        out_shape=[jax.ShapeDtypeStruct((nb, s, BRANCH_W), F32)] * 2,
        scratch_shapes=[pltpu.VMEM((2, nb, BRANCH_W, BRANCH_W), F32),
                        pltpu.VMEM((GLA_SUB * nb * c, BRANCH_W), BF16)],
        compiler_params=_cparams(("arbitrary",)),
        name="gla",
    )(gate_up_pad, gate_b, tri, bd64, hm16, hm64, causal, zg, zg)


def _softmax_pv(qk_pairs, v, s_ref, p_ref, slot):
    tq = qk_pairs[0][0].shape[0]
    n_keys = v.shape[0]
    rows = len(qk_pairs) * tq
    for i, (q, k) in enumerate(qk_pairs):
        s_ref[slot, i * tq:(i + 1) * tq, 0:n_keys] = _dot_nt(q, k)
    sums = []
    for r0 in range(0, rows, SOFTMAX_ROWS):
        s = s_ref[slot, r0:r0 + SOFTMAX_ROWS, 0:n_keys]
        p = jnp.exp2(s - jnp.max(s, axis=-1, keepdims=True))
        sums.append(jnp.sum(p, axis=-1, keepdims=True))
        p_ref[slot, r0:r0 + SOFTMAX_ROWS, 0:n_keys] = p.astype(BF16)
    l = jnp.concatenate(sums, axis=0)
    o = jnp.dot(p_ref[slot, 0:rows, 0:n_keys], v, preferred_element_type=F32) / l
    return [o[i * tq:(i + 1) * tq] for i in range(len(qk_pairs))]


def _gqa_kernel(q_ref, k_ref, v_ref, *rest):
    o_ref, s_ref, p_ref = rest[-3:]
    outs = [None] * N_HEADS
    for kv in range(2):
        cs = slice(kv * HEAD_W, (kv + 1) * HEAD_W)
        pairs = [(q_ref[:, h * HEAD_W:(h + 1) * HEAD_W], k_ref[:, cs]) for h in (2 * kv, 2 * kv + 1)]
        outs[2 * kv], outs[2 * kv + 1] = _softmax_pv(pairs, v_ref[:, cs], s_ref, p_ref, kv)
    o_ref[...] = jnp.concatenate(outs, axis=1).astype(o_ref.dtype)


def _dif_kernel(lam_ref, sub_ref, bd64_ref, q_ref, k_ref, v_ref, *rest, out_scale):
    o_ref, s_ref, p_ref = rest[-3:]
    outs = []
    for h in range(N_HEADS):
        vs = slice(h * HEAD_W, (h + 1) * HEAD_W)
        pairs = []
        for mp in range(2):
            qs = slice((2 * h + mp) * DIF_QK, (2 * h + mp + 1) * DIF_QK)
            pairs.append((q_ref[:, qs], k_ref[:, qs]))
        o1, o2 = _softmax_pv(pairs, v_ref[:, vs], s_ref, p_ref, h % 2)
        outs.append(o1 - lam_ref[:, vs] * o2)
    o = _group_rms(jnp.concatenate(outs, axis=1), bd64_ref[...], sub_ref[...], HEAD_W) * out_scale
    o_ref[...] = o.astype(o_ref.dtype)


def _attention(body, name, pre_specs, pre_args, q, k, v, geo, need_ctx):
    nb, t, n_ctx, tq = geo["batch"], geo["seq"], geo["ctx"], geo["tq"]
    s = t + n_ctx
    kw, vw = k.shape[2], v.shape[2]
    keys = lambda n: pl.BlockSpec((None, n, kw), lambda b, i: (b, 0 if n == s else t // n_ctx, 0),
                                  pipeline_mode=pl.Buffered(1))
    vals = lambda n: pl.BlockSpec((None, n, vw), lambda b, i: (b, 0 if n == s else t // n_ctx, 0),
                                  pipeline_mode=pl.Buffered(1))

    def call(n_keys, q_tiles, q_off, prev):
        qmap = lambda b, i: (b, q_off + i, 0)
        alias = {} if prev is None else {len(pre_args) + 3: 0}
        extra = [] if prev is None else [pl.BlockSpec(memory_space=pl.ANY)]
        return pl.pallas_call(
            body, grid=(nb, q_tiles),
            in_specs=pre_specs + [pl.BlockSpec((None, tq, BRANCH_W), qmap), keys(n_keys), vals(n_keys)] + extra,
            out_specs=pl.BlockSpec((None, tq, BRANCH_W), qmap),
            out_shape=jax.ShapeDtypeStruct(q.shape, BF16),
            scratch_shapes=[pltpu.VMEM((2, 2 * tq, n_keys), F32), pltpu.VMEM((2, 2 * tq, n_keys), BF16)],
            input_output_aliases=alias,
            compiler_params=_cparams(("parallel", "parallel")),
            name=name,
        )(*pre_args, q, k, v, *([] if prev is None else [prev]))

    out = call(s, t // tq, 0, None)
    if need_ctx:
        out = call(n_ctx, n_ctx // tq, t // tq, out)
    return out


def _gqa(cq, ck, cv, geo, need_ctx):
    return _attention(_gqa_kernel, "gqa", [], [], cq, ck, cv, geo, need_ctx)


def _dif(dq, dk, dv, lam_lane, subln, bd64, geo, need_ctx, out_scale):
    pre = [_resident((1, BRANCH_W)), _resident((1, BRANCH_W)), _resident((256, 256))]
    return _attention(functools.partial(_dif_kernel, out_scale=out_scale), "diffattn", pre,
                      [lam_lane, subln, bd64], dq, dk, dv, geo, need_ctx)


def _merge_kernel(x_ref, mod_ref, n1_ref, wg_ref, wb_ref, wo_ref, bd64_ref, gnw_ref,
                  rof_ref, rob_ref, rg_ref, gof_ref, gob_ref, gr_ref, yc_ref, yd_ref, o_ref):
    nb, rt, d = x_ref.shape
    m = nb * rt
    x = x_ref[...]
    hb = _adaln(x, n1_ref[...], mod_ref[:, 0:1, :], mod_ref[:, 1:2, :]).astype(BF16).reshape(m, d)
    flat = lambda ref: ref[...].reshape(m, BRANCH_W)
    bd64 = bd64_ref[...]
    ya = _group_rms(flat(rof_ref) + flat(rob_ref), bd64, None, HEAD_W) * _silu(flat(rg_ref))
    yb = _group_rms(flat(gof_ref) + flat(gob_ref), bd64, gnw_ref[...], HEAD_W) * _silu(flat(gr_ref))
    ys = (ya.astype(BF16), yb.astype(BF16), flat(yc_ref), flat(yd_ref))
    mix = None
    for i in range(4):
        gate = _sigmoid(jnp.dot(hb, wg_ref[:, i * d:(i + 1) * d], preferred_element_type=F32))
        term = gate * jnp.dot(ys[i], wb_ref[i], preferred_element_type=F32)
        mix = term if mix is None else mix + term
    o_ref[...] = x + mod_ref[:, 2:3, :] * _dot(mix, wo_ref[...]).reshape(nb, rt, d)


def _merge(xs, mod, norm1, w_gate, w_branch, w_out, bd64, gla_norm, rof, rob, zr, gof, gob, zg, yc, yd, geo,
           need_ctx):
    nb, s, d = xs.shape
    rt = geo["rt"]
    tiles = (s if need_ctx else geo["seq"]) // rt
    row = lambda i: (0, i, 0)
    col3 = lambda i: (0, i, 3)
    bw = BRANCH_W
    act = lambda imap: pl.BlockSpec((nb, rt, bw), imap)
    return pl.pallas_call(
        _merge_kernel,
        grid=(tiles,),
        in_specs=[pl.BlockSpec((nb, rt, d), row),
                  _mod_spec(nb, d, geo["seq"] // rt),
                  _resident((1, d)), _resident((d, 4 * d)), _resident((4, bw, d)), _resident((d, d)),
                  _resident((256, 256)), _resident((1, bw)),
                  act(row), act(row), act(col3), act(row), act(row), act(col3), act(row), act(row)],
        out_specs=pl.BlockSpec((nb, rt, d), row),
        out_shape=jax.ShapeDtypeStruct((nb, s, d), F32),
        input_output_aliases={0: 0},
        compiler_params=_cparams(("parallel",)),
        name="merge",
    )(xs, mod, norm1, w_gate, w_branch, w_out, bd64, gla_norm, rof, rob, zr, gof, gob, zg, yc, yd)


def _ffn_kernel(x_ref, xp_ref, xn_ref, mod_ref, n2_ref, wa_ref, wg_ref, cw_ref, wd_ref, o_ref, h_ref, acc_ref,
                *, seq, total, n_f):
    nb, rt, d = x_ref.shape
    ext = rt + 2 * HALO
    i = pl.program_id(0)
    j = pl.program_id(1)

    @pl.when(j == 0)
    def _():
        def hn(xx):
            return _adaln(xx, n2_ref[...], mod_ref[:, 3:4, :], mod_ref[:, 4:5, :]).astype(BF16)
        h_ref[:, 0:HALO, :] = hn(xp_ref[...])
        h_ref[:, HALO:HALO + rt, :] = hn(x_ref[...])
        h_ref[:, HALO + rt:ext, :] = hn(xn_ref[...])
        acc_ref[...] = jnp.zeros_like(acc_ref)

    tf = wa_ref.shape[1]
    u = jnp.dot(h_ref[...].reshape(nb * ext, d), wa_ref[...], preferred_element_type=F32)
    centre = lambda a: a.reshape(nb, ext, tf)[:, HALO:HALO + rt, :]
    up = centre(pltpu.roll(u, 1, 0))
    mid = centre(u)
    dn = centre(pltpu.roll(u, nb * ext - 1, 0))
    pos = i * rt + lax.broadcasted_iota(jnp.int32, (rt, 1), 0)
    keep_up = jnp.where(pos == 0, 0.0, 1.0) * jnp.where(pos == seq, 0.0, 1.0)
    keep_dn = jnp.where(pos == seq - 1, 0.0, 1.0) * jnp.where(pos == total - 1, 0.0, 1.0)
    a = ((up * keep_up) * cw_ref[0:1, :] + mid * cw_ref[1:2, :] + (dn * keep_dn) * cw_ref[2:3, :]
         + cw_ref[3:4, :])
    gate = jnp.dot(h_ref[:, HALO:HALO + rt, :].reshape(nb * rt, d), wg_ref[...], preferred_element_type=F32)
    acc_ref[...] += _dot((_silu(a) * gate.reshape(nb, rt, tf)).reshape(nb * rt, tf), wd_ref[...])

    @pl.when(j == n_f - 1)
    def _():
        o_ref[...] = x_ref[...] + mod_ref[:, 5:6, :] * acc_ref[...].reshape(nb, rt, d)


def _ffn(xs, mod, norm2, w_up, conv_pack, w_down, geo, need_ctx):
    nb, s, d = xs.shape
    rt, tf, t = geo["rt_ffn"], geo["tf"], geo["seq"]
    s_out = s if need_ctx else t
    n_f = D_FF_W // tf
    hb = rt // HALO
    last_blk = s // HALO - 1
    row = lambda i, j: (0, i, 0)
    return pl.pallas_call(
        functools.partial(_ffn_kernel, seq=t, total=s, n_f=n_f),
        grid=(s_out // rt, n_f),
        in_specs=[pl.BlockSpec((nb, rt, d), row),
                  pl.BlockSpec((nb, HALO, d), lambda i, j: (0, jnp.maximum(i * hb - 1, 0), 0)),
                  pl.BlockSpec((nb, HALO, d), lambda i, j: (0, jnp.minimum((i + 1) * hb, last_blk), 0)),
                  _mod_spec(nb, d, t // rt),
                  _resident((1, d)),
                  pl.BlockSpec((d, tf), lambda i, j: (0, j)),
                  pl.BlockSpec((d, tf), lambda i, j: (0, n_f + j)),
                  pl.BlockSpec((8, tf), lambda i, j: (0, j)),
                  pl.BlockSpec((tf, d), lambda i, j: (j, 0))],
        out_specs=pl.BlockSpec((nb, rt, d), row),
        out_shape=jax.ShapeDtypeStruct((nb, s_out, d), F32),
        scratch_shapes=[pltpu.VMEM((nb, rt + 2 * HALO, d), BF16), pltpu.VMEM((nb * rt, d), F32)],
        compiler_params=_cparams(("parallel", "arbitrary")),
        name="convffn",
    )(xs, xs, xs, mod, norm2, w_up, w_up, conv_pack, w_down)


def _rope_tables(t, n_ctx):
    rows = t // GRID_WIDTH

    def axial(dim):
        axis_dim = dim // 2
        freqs = ROPE_BASE ** (-jnp.arange(0, axis_dim, 2, dtype=F32) / axis_dim)
        row = jnp.repeat(jnp.arange(rows, dtype=F32), GRID_WIDTH)
        col = jnp.tile(jnp.arange(GRID_WIDTH, dtype=F32), rows)
        ang = jnp.concatenate([row[:, None] * freqs, col[:, None] * freqs], axis=-1)
        return jnp.cos(ang), jnp.sin(ang)

    freqs = ROPE_BASE ** (-jnp.linspace(0.0, 1.0, HEAD_W // 2, dtype=F32))
    ang = jnp.arange(t, dtype=F32)[:, None] * freqs
    parts = []
    for cos, sin in ((jnp.cos(ang), jnp.sin(ang)), axial(HEAD_W), axial(DIF_QK)):
        reps = LANES // (2 * cos.shape[1])
        parts.append(jnp.tile(cos, (1, 2 * reps)))
        parts.append(jnp.tile(jnp.concatenate([-sin, sin], axis=1), (1, reps)))
    table = jnp.concatenate(parts, axis=1)
    ident = jnp.tile(jnp.concatenate([jnp.ones((1, LANES), F32), jnp.zeros((1, LANES), F32)], axis=1), (n_ctx, 3))
    return jnp.concatenate([table, ident], axis=0)


def _block_diag_ones(width, group):
    idx = jnp.arange(width) // group
    return (idx[:, None] == idx[None, :]).astype(BF16)


def kernel(x, c, ctx, c_ctx, w_mod, b_mod, norm1, norm2, w_in, ret_decay, gla_gate_up, gla_gate_b, gla_norm,
           gqa_qnorm, gqa_knorm, dif_qnorm, dif_knorm, dif_lambda, dif_subln, w_branch, w_out, w_up, conv_w,
           conv_b, w_down):
    nb, t, d = x.shape
    n_ctx = ctx.shape[1]
    depth = w_mod.shape[0]
    rt = 128
    assert t % 256 == 0 and n_ctx % 256 == 0 and nb + 1 <= 8
    geo = dict(batch=nb, seq=t, ctx=n_ctx, rt=rt, rt_ffn=128, tf=D_FF_W // 2, tq=256)

    xs = jnp.concatenate([x, ctx], axis=1)
    cc = jnp.zeros((8, d), F32).at[0:nb].set(c).at[nb].set(c_ctx)
    mod_all = _modulation(cc, w_mod, b_mod).reshape(depth, 8, 6, d)

    table = _rope_tables(t, n_ctx)
    bd64 = _block_diag_ones(256, HEAD_W)
    bd32 = _block_diag_ones(256, DIF_QK)
    mc = nb * GLA_CHUNK_LEN
    ti = jnp.arange(mc)
    same_chunk = (ti[:, None] // GLA_CHUNK_LEN) == (ti[None, :] // GLA_CHUNK_LEN)
    tri = jnp.stack([same_chunk & (ti[:, None] >= ti[None, :]), same_chunk & (ti[:, None] <= ti[None, :])]).astype(BF16)
    lane_head = jnp.arange(BRANCH_W)[None, :] // HEAD_W
    hm16 = (jnp.arange(N_HEADS * GLA_SUB)[:, None] // GLA_SUB == lane_head).astype(F32)
    hm64 = (jnp.arange(N_HEADS * GLA_CHUNK_LEN)[:, None] // GLA_CHUNK_LEN == lane_head).astype(F32)
    qi = jnp.arange(N_HEADS * GLA_CHUNK_LEN)[:, None] % GLA_CHUNK_LEN
    kj = jnp.arange(GLA_CHUNK_LEN)[None, :]
    causal = jnp.stack([kj <= qi, kj >= qi]).astype(F32)
    n_mix = 2080
    n_gate0 = 3360

    for l in range(depth):
        need_ctx = l < depth - 1
        w = w_in[l]
        w_proj = jnp.concatenate([w[:, :n_mix], jnp.zeros((d, LANES - 2 * GLA_RANK_W), F32),
                                  w[:, n_mix:n_gate0]], axis=1).astype(BF16)
        w_gate = w[:, n_gate0:].astype(BF16)
        qk_norms = jnp.zeros((8, 256), F32)
        qk_norms = qk_norms.at[0].set(jnp.tile(gqa_qnorm[l], 4)).at[1].set(jnp.tile(gqa_knorm[l], 4))
        qk_norms = qk_norms.at[2].set(jnp.tile(dif_qnorm[l], 8)).at[3].set(jnp.tile(dif_knorm[l], 8))
        mod = jnp.stack([mod_all[l, 0:nb], jnp.broadcast_to(mod_all[l, nb], (nb, 6, d))])
        n1 = norm1[l][None, :]
        n2 = norm2[l][None, :]

        zr, zg, cq, ck, cv, dq, dk, dv = _inproj(xs, mod, n1, w_proj, table, bd64, bd32, qk_norms, geo)

        log_gamma = jax.nn.log_sigmoid(ret_decay[l].astype(F32))
        lg_mat = jnp.broadcast_to(log_gamma.reshape(2 * N_HEADS, 1), (2 * N_HEADS, LANES))
        lg_lane = jnp.repeat(log_gamma, HEAD_W, axis=1)
        rof, rob = _retention(zr, lg_mat, lg_lane, geo)

        gu = jnp.zeros((2, LANES, BRANCH_W), F32)
        gu = gu.at[0, 0:GLA_RANK_W].set(gla_gate_up[l, 0]).at[1, GLA_RANK_W:2 * GLA_RANK_W].set(gla_gate_up[l, 1])
        gof, gob = _gla(zg, gu.astype(BF16), gla_gate_b[l], tri, bd64, hm16, hm64, causal, geo)

        yc = _gqa(cq, ck, cv, geo, need_ctx)

        lam_init = 0.8 - 0.6 * math.exp(-0.3 * l)
        lv = dif_lambda[l].astype(F32)
        lam = jnp.exp(jnp.sum(lv[0] * lv[1])) - jnp.exp(jnp.sum(lv[2] * lv[3])) + lam_init
        lam_lane = jnp.broadcast_to(lam, (1, BRANCH_W)).astype(F32)
        subln = jnp.tile(dif_subln[l], 4)[None, :]
        yd = _dif(dq, dk, dv, lam_lane, subln, bd64, geo, need_ctx, 1.0 - lam_init)

        gnw = jnp.tile(gla_norm[l], 4)[None, :]
        xs = _merge(xs, mod, n1, w_gate, w_branch[l].astype(BF16), w_out[l].astype(BF16), bd64, gnw,
                    rof, rob, zr, gof, gob, zg, yc, yd, geo, need_ctx)

        conv_pack = jnp.zeros((8, D_FF_W), F32).at[0:3].set(conv_w[l]).at[3].set(conv_b[l])
        xs = _ffn(xs, mod, n2, w_up[l].astype(BF16), conv_pack, w_down[l].astype(BF16), geo, need_ctx)

    return xs
```

```python
import functools
import math

import jax
import jax.numpy as jnp
from jax import lax
from jax.experimental import pallas as pl
from jax.experimental.pallas import tpu as pltpu

F32 = jnp.float32
BF16 = jnp.bfloat16

N_HEADS = 4
HEAD_W = 64
BRANCH_W = N_HEADS * HEAD_W
DIF_QK = 32
RET_CHUNK_LEN = 128
GLA_CHUNK_LEN = 64
GLA_SUB = 16
GLA_RANK_W = 16
GLA_TAU_INV = 1.0 / 16.0
GRID_WIDTH = 64
ROPE_BASE = 10000.0
EPS = 1e-6
D_FF_W = 2816
LOG2E = 1.4426950408889634

LANES = 128
VMEM_LIMIT_BYTES = 56 * 1024 * 1024

RET_OFF, GLA_OFF, GQA_OFF, DIF_OFF, PROJ_W = 0, 1024, 2176, 2688, 3456
GLA_GROUP_W = GQA_OFF - GLA_OFF
HALO = 16
SOFTMAX_ROWS = 128


def _cparams(sem):
    return pltpu.CompilerParams(dimension_semantics=sem, vmem_limit_bytes=VMEM_LIMIT_BYTES)


def _resident(shape):
    nd = len(shape)
    return pl.BlockSpec(shape, lambda *_: (0,) * nd, pipeline_mode=pl.Buffered(1))


def _dot(a, b):
    return jnp.dot(a.astype(BF16), b.astype(BF16), preferred_element_type=F32)


def _dot_nt(a, b):
    return lax.dot_general(a.astype(BF16), b.astype(BF16), (((1,), (1,)), ((), ())),
                           preferred_element_type=F32)


def _dot_tn(a, b):
    return lax.dot_general(a.astype(BF16), b.astype(BF16), (((0,), (0,)), ((), ())),
                           preferred_element_type=F32)


def _sigmoid(x):
    return 1.0 / (1.0 + jnp.exp(-x))


def _silu(x):
    return x * _sigmoid(x)


def _adaln(x, norm_w, shift, scale):
    h = x * lax.rsqrt(jnp.mean(x * x, axis=-1, keepdims=True) + EPS) * norm_w
    return h * (1.0 + scale) + shift


def _group_rms(x, bd, weight, group):
    x2 = x * x
    hi = x2.astype(BF16)
    lo = (x2 - hi.astype(F32)).astype(BF16)
    ss = jnp.dot(hi, bd, preferred_element_type=F32) + jnp.dot(lo, bd, preferred_element_type=F32)
    y = x * lax.rsqrt(ss * (1.0 / group) + EPS)
    return y if weight is None else y * weight


def _rope(x, cos, sin_signed, half):
    w = x.shape[1]
    reps = w // LANES
    if reps > 1:
        cos = jnp.concatenate([cos] * reps, axis=1)
        sin_signed = jnp.concatenate([sin_signed] * reps, axis=1)
    lane = lax.broadcasted_iota(jnp.int32, x.shape, 1)
    lower = (lane & (2 * half - 1)) < half
    partner = jnp.where(lower, pltpu.roll(x, w - half, 1), pltpu.roll(x, half, 1))
    return x * cos + partner * sin_signed


def _mod_kernel(c_ref, w_ref, b_ref, o_ref):
    o_ref[...] = _dot(_silu(c_ref[...]), w_ref[...]) + b_ref[...]


def _modulation(cc, w_mod, b_mod):
    n_layers, d, width = w_mod.shape
    tn = 1024
    return pl.pallas_call(
        _mod_kernel,
        grid=(n_layers, width // tn),
        in_specs=[pl.BlockSpec((8, d), lambda l, j: (0, 0)),
                  pl.BlockSpec((None, d, tn), lambda l, j: (l, 0, j)),
                  pl.BlockSpec((None, 1, tn), lambda l, j: (l, 0, j))],
        out_specs=pl.BlockSpec((None, 8, tn), lambda l, j: (l, 0, j)),
        out_shape=jax.ShapeDtypeStruct((n_layers, 8, width), F32),
        compiler_params=_cparams(("parallel", "parallel")),
        name="modulation",
    )(cc, w_mod, b_mod.reshape(n_layers, 1, width))


def _mod_spec(nb, d, n_lat_tiles):
    return pl.BlockSpec((None, nb, 6, d), lambda i, *_: (jnp.where(i < n_lat_tiles, 0, 1), 0, 0, 0))


def _inproj_kernel(x_ref, mod_ref, n1_ref, w_ref, tab_ref, bd64_ref, bd32_ref, nw_ref,
                   zr_ref, zg_ref, cq_ref, ck_ref, cv_ref, dq_ref, dk_ref, dv_ref):
    nb, rt, d = x_ref.shape
    m = nb * rt
    hb = _adaln(x_ref[...], n1_ref[...], mod_ref[:, 0:1, :], mod_ref[:, 1:2, :]).astype(BF16).reshape(m, d)

    def table(lo):
        t = tab_ref[:, lo:lo + LANES]
        return jnp.concatenate([t] * nb, axis=0)

    r_cos, r_sin, g_cos, g_sin, d_cos, d_sin = (table(i * LANES) for i in range(6))

    def put(ref, lo, val):
        ref[:, :, lo:lo + val.shape[1]] = val.reshape(nb, rt, val.shape[1]).astype(ref.dtype)

    zr = jnp.dot(hb, w_ref[:, RET_OFF:GLA_OFF], preferred_element_type=F32)
    put(zr_ref, 0, _rope(zr[:, 0:256], r_cos, r_sin, 32))
    put(zr_ref, 256, _rope(zr[:, 256:512], r_cos, r_sin, 32) * (HEAD_W ** -0.5))
    put(zr_ref, 512, zr[:, 512:1024])

    zg = jnp.dot(hb, w_ref[:, GLA_OFF:GQA_OFF], preferred_element_type=F32)
    put(zg_ref, 0, zg[:, 0:256] * (HEAD_W ** -0.5))
    put(zg_ref, 256, zg[:, 256:GLA_GROUP_W])

    zc = jnp.dot(hb, w_ref[:, GQA_OFF:DIF_OFF], preferred_element_type=F32)
    bd64 = bd64_ref[...]
    q = _group_rms(zc[:, 0:256], bd64, nw_ref[0:1, :], HEAD_W)
    put(cq_ref, 0, _rope(q, g_cos, g_sin, 32) * (HEAD_W ** -0.5 * LOG2E))
    k = _group_rms(zc[:, 256:384], bd64[0:128, 0:128], nw_ref[1:2, 0:128], HEAD_W)
    put(ck_ref, 0, _rope(k, g_cos, g_sin, 32))
    put(cv_ref, 0, zc[:, 384:512])

    zd = jnp.dot(hb, w_ref[:, DIF_OFF:PROJ_W], preferred_element_type=F32)
    bd32 = bd32_ref[...]
    q = _group_rms(zd[:, 0:256], bd32, nw_ref[2:3, :], DIF_QK)
    put(dq_ref, 0, _rope(q, d_cos, d_sin, 16) * (DIF_QK ** -0.5 * LOG2E))
    k = _group_rms(zd[:, 256:512], bd32, nw_ref[3:4, :], DIF_QK)
    put(dk_ref, 0, _rope(k, d_cos, d_sin, 16))
    put(dv_ref, 0, zd[:, 512:768])


def _inproj(xs, mod, norm1, w_proj, table, bd64, bd32, qk_norms, geo):
    nb, s, d = xs.shape
    rt = geo["rt"]
    row = lambda i: (0, i, 0)
    widths = (1024, GLA_GROUP_W, 256, 128, 128, 256, 256, 256)
    dtypes = (F32, F32, BF16, BF16, BF16, BF16, BF16, BF16)
    return pl.pallas_call(
        _inproj_kernel,
        grid=(s // rt,),
        in_specs=[pl.BlockSpec((nb, rt, d), row),
                  _mod_spec(nb, d, geo["seq"] // rt),
                  _resident((1, d)),
                  _resident((d, PROJ_W)),
                  pl.BlockSpec((rt, 768), lambda i: (i, 0)),
                  _resident((256, 256)),
                  _resident((256, 256)),
                  _resident((8, 256))],
        out_specs=[pl.BlockSpec((nb, rt, w), row) for w in widths],
        out_shape=[jax.ShapeDtypeStruct((nb, s, w), t) for w, t in zip(widths, dtypes)],
        compiler_params=_cparams(("parallel",)),
        name="inproj",
    )(xs, mod, norm1, w_proj, table, bd64, bd32, qk_norms)


def _scan_index_maps(geo, chunk):
    nl, nc = geo["seq"] // chunk, geo["ctx"] // chunk
    fwd = lambda s: (0, jnp.where(s < nc, nl + s, s - nc), 0)
    bwd = lambda s: (0, jnp.where(s < nc, nl + nc - 1 - s, nl - 1 - (s - nc)), 0)
    return fwd, bwd, nl + nc


def _ret_kernel(lgr_ref, lgl_ref, hm_ref, bd64_ref, zf_ref, zb_ref, of_ref, ob_ref, st_ref):
    nb = zf_ref.shape[0]
    c = RET_CHUNK_LEN

    @pl.when(pl.program_id(0) == 0)
    def _():
        st_ref[...] = jnp.zeros_like(st_ref)

    hm = hm_ref[...]
    bd64_f = bd64_ref[...].astype(F32)
    ii = lax.broadcasted_iota(jnp.int32, (N_HEADS * c, c), 0) & (c - 1)
    jj = lax.broadcasted_iota(jnp.int32, (N_HEADS * c, c), 1)
    ri = lax.broadcasted_iota(jnp.int32, (c, BRANCH_W), 0).astype(F32)
    for rev, z_ref, o_ref in ((False, zf_ref, of_ref), (True, zb_ref, ob_ref)):
        d = int(rev)
        dist = ((jj - ii) if rev else (ii - jj)).astype(F32)
        dec = jnp.where(dist >= 0, jnp.exp(lgr_ref[d] * jnp.maximum(dist, 0.0)), 0.0)
        lgl = lgl_ref[d:d + 1, :]
        q_dec = jnp.exp(lgl * ((c - ri) if rev else (ri + 1.0)))
        k_dec = jnp.exp(lgl * (ri if rev else (c - 1.0 - ri)))
        c_dec = jnp.exp(lgl * float(c))
        for b in range(nb):
            q = z_ref[b, :, 0:256]
            k = z_ref[b, :, 256:512]
            v = z_ref[b, :, 512:768]
            q4 = jnp.concatenate([q] * N_HEADS, axis=0) * hm
            o4 = _dot(_dot_nt(q4, k) * dec, v) * hm
            st = st_ref[d, b]
            o_ref[b] = o4[0:c] + o4[c:2 * c] + o4[2 * c:3 * c] + o4[3 * c:4 * c] + _dot(q, st) * q_dec
            st_ref[d, b] = st * c_dec + _dot_tn(k * k_dec, v) * bd64_f


def _retention(zr, lg_rows, lg_lane, head_mask, bd64, geo):
    nb, s, _ = zr.shape
    c = RET_CHUNK_LEN
    fwd, bwd, steps = _scan_index_maps(geo, c)
    return pl.pallas_call(
        _ret_kernel,
        grid=(steps,),
        in_specs=[_resident((2, N_HEADS * c, c)), _resident((2, BRANCH_W)), _resident((N_HEADS * c, BRANCH_W)),
                  _resident((256, 256)),
                  pl.BlockSpec((nb, c, 1024), fwd), pl.BlockSpec((nb, c, 1024), bwd)],
        out_specs=[pl.BlockSpec((nb, c, BRANCH_W), fwd), pl.BlockSpec((nb, c, BRANCH_W), bwd)],
        out_shape=[jax.ShapeDtypeStruct((nb, s, BRANCH_W), F32)] * 2,
        scratch_shapes=[pltpu.VMEM((2, nb, BRANCH_W, BRANCH_W), F32)],
        compiler_params=_cparams(("arbitrary",)),
        name="retention",
    )(lg_rows, lg_lane, head_mask, bd64, zr, zr)


GLA_FACTOR_MIN_LOG_DECAY = -40.0
GLA_FACTOR_MAX_KEY = 1e15


def _gla_prepare(z_ref, d, gu_ref, gb_ref, tri_ref):
    nb, c, _ = z_ref.shape
    m = nb * c
    q = z_ref[:, :, 0:256].reshape(m, BRANCH_W)
    k = z_ref[:, :, 256:512].reshape(m, BRANCH_W)
    v = z_ref[:, :, 512:768].reshape(m, BRANCH_W)
    logit = _dot(z_ref[:, :, 1024:GLA_GROUP_W].reshape(m, LANES), gu_ref[d]) + gb_ref[d:d + 1, :]
    g = (jnp.minimum(logit, 0.0) - jnp.log(1.0 + jnp.exp(-jnp.abs(logit)))) * GLA_TAU_INV
    g1 = g.astype(BF16)
    r1 = g - g1.astype(F32)
    g2 = r1.astype(BF16)
    g3 = (r1 - g2.astype(F32)).astype(BF16)
    tri = tri_ref[d]
    bcum = (jnp.dot(tri, g1, preferred_element_type=F32) + jnp.dot(tri, g2, preferred_element_type=F32)
            + jnp.dot(tri, g3, preferred_element_type=F32))
    return q, k, v, bcum


def _gla_state_step(st_ref, d, b, q_state_b, kb, vb, bb, rev, bd64_f):
    c = kb.shape[0]
    b_last = bb[0:1, :] if rev else bb[c - 1:c, :]
    st = st_ref[d, b]
    o_b = _dot_nt(q_state_b, st)
    st_ref[d, b] = st * jnp.exp(b_last) + _dot_tn(vb, kb * jnp.exp(b_last - bb)) * bd64_f
    return o_b


def _gla_factored(d, rev, q, k, v, bcum, o_ref, st_ref, hm64_ref, causal_ref, bd64_f):
    nb, c, _ = o_ref.shape
    q_state = q * jnp.exp(bcum)
    k_inv = k * jnp.exp(-bcum)
    hm = hm64_ref[...]
    causal = causal_ref[d]
    outs = []
    for b in range(nb):
        rows = slice(b * c, (b + 1) * c)
        qb, vb = q_state[rows], v[rows]
        q4 = jnp.concatenate([qb] * N_HEADS, axis=0) * hm
        att = jnp.where(causal > 0.0, _dot_nt(q4, k_inv[rows]), 0.0)
        o4 = _dot(att, vb) * hm
        o_b = o4[0:c] + o4[c:2 * c] + o4[2 * c:3 * c] + o4[3 * c:4 * c]
        outs.append(o_b + _gla_state_step(st_ref, d, b, qb, k[rows], vb, bcum[rows], rev, bd64_f))
    o_ref[...] = jnp.concatenate(outs, axis=0).reshape(nb, c, BRANCH_W)


def _gla_pairwise(d, rev, q, k, v, bcum, o_ref, st_ref, w_ref, hm16_ref, bd64, bd64_f):
    nb, c, _ = o_ref.shape
    sub = GLA_SUB
    nsub = c // sub
    m = nb * c
    head_mask = hm16_ref[...]
    rin = lax.broadcasted_iota(jnp.int32, (m, BRANCH_W), 0) & (sub - 1)
    for dl in range(sub):
        if dl == 0:
            w = q * k
        else:
            shift = (m - dl) if rev else dl
            ks = pltpu.roll(k, shift, 0)
            bs = pltpu.roll(bcum, shift, 0)
            valid = (rin + dl <= sub - 1) if rev else (rin >= dl)
            w = jnp.where(valid, q * ks * jnp.exp(jnp.minimum(bcum - bs, 0.0)), 0.0)
        w_ref[dl * m:(dl + 1) * m, :] = w.astype(BF16)
    att = jnp.dot(w_ref[...], bd64, preferred_element_type=F32)
    o_acc = att[0:m] * v
    for dl in range(1, sub):
        o_acc = o_acc + att[dl * m:(dl + 1) * m] * pltpu.roll(v, (m - dl) if rev else dl, 0)

    q_state = q * jnp.exp(bcum)
    outs = []
    for b in range(nb):
        r0 = b * c
        qb, kb, vb, bb = q[r0:r0 + c], k[r0:r0 + c], v[r0:r0 + c], bcum[r0:r0 + c]
        rows = []
        for blk in range(nsub):
            a0, a1 = blk * sub, (blk + 1) * sub
            if rev and blk < nsub - 1:
                anchor, lo, hi = bb[a1:a1 + 1, :], a1, c
            elif (not rev) and blk > 0:
                anchor, lo, hi = bb[a0 - 1:a0, :], 0, a0
            else:
                rows.append(jnp.zeros((sub, BRANCH_W), F32))
                continue
            qa = qb[a0:a1] * jnp.exp(bb[a0:a1] - anchor)
            ka = kb[lo:hi] * jnp.exp(anchor - bb[lo:hi])
            q4 = jnp.concatenate([qa] * N_HEADS, axis=0) * head_mask
            o4 = _dot(_dot_nt(q4, ka), vb[lo:hi]) * head_mask
            rows.append(o4[0:sub] + o4[sub:2 * sub] + o4[2 * sub:3 * sub] + o4[3 * sub:4 * sub])
        o_b = _gla_state_step(st_ref, d, b, q_state[r0:r0 + c], kb, vb, bb, rev, bd64_f)
        outs.append(o_b + jnp.concatenate(rows, axis=0))
    o_ref[...] = (o_acc + jnp.concatenate(outs, axis=0)).reshape(nb, c, BRANCH_W)


def _gla_kernel(gu_ref, gb_ref, tri_ref, bd64_ref, hm16_ref, hm64_ref, causal_ref, zf_ref, zb_ref,
                of_ref, ob_ref, st_ref, w_ref):
    @pl.when(pl.program_id(0) == 0)
    def _():
        st_ref[...] = jnp.zeros_like(st_ref)

    bd64 = bd64_ref[...]
    bd64_f = bd64.astype(F32)
    fwd = _gla_prepare(zf_ref, 0, gu_ref, gb_ref, tri_ref)
    bwd = _gla_prepare(zb_ref, 1, gu_ref, gb_ref, tri_ref)
    min_decay = jnp.minimum(jnp.min(fwd[3]), jnp.min(bwd[3]))
    max_key = jnp.maximum(jnp.max(jnp.abs(fwd[1])), jnp.max(jnp.abs(bwd[1])))
    factor_ok = jnp.logical_and(min_decay > GLA_FACTOR_MIN_LOG_DECAY, max_key < GLA_FACTOR_MAX_KEY)

    @pl.when(factor_ok)
    def _():
        _gla_factored(0, False, *fwd, of_ref, st_ref, hm64_ref, causal_ref, bd64_f)
        _gla_factored(1, True, *bwd, ob_ref, st_ref, hm64_ref, causal_ref, bd64_f)

    @pl.when(jnp.logical_not(factor_ok))
    def _():
        _gla_pairwise(0, False, *fwd, of_ref, st_ref, w_ref, hm16_ref, bd64, bd64_f)
        _gla_pairwise(1, True, *bwd, ob_ref, st_ref, w_ref, hm16_ref, bd64, bd64_f)


def _gla(zg, gate_up_pad, gate_b, tri, bd64, hm16, hm64, causal, geo):
    nb, s, _ = zg.shape
    c = GLA_CHUNK_LEN
    fwd, bwd, steps = _scan_index_maps(geo, c)
    return pl.pallas_call(
        _gla_kernel,
        grid=(steps,),
        in_specs=[_resident((2, LANES, BRANCH_W)), _resident((2, BRANCH_W)),
                  _resident((2, nb * c, nb * c)), _resident((256, 256)), _resident((N_HEADS * GLA_SUB, BRANCH_W)),
                  _resident((N_HEADS * c, BRANCH_W)), _resident((2, N_HEADS * c, c)),
                  pl.BlockSpec((nb, c, GLA_GROUP_W), fwd), pl.BlockSpec((nb, c, GLA_GROUP_W), bwd)],
        out_specs=[pl.BlockSpec((nb, c, BRANCH_W), fwd), pl.BlockSpec((nb, c, BRANCH_W), bwd)],
        out_shape=[jax.ShapeDtypeStruct((nb, s, BRANCH_W), F32)] * 2,
        scratch_shapes=[pltpu.VMEM((2, nb, BRANCH_W, BRANCH_W), F32),
                        pltpu.VMEM((GLA_SUB * nb * c, BRANCH_W), BF16)],
        compiler_params=_cparams(("arbitrary",)),
        name="gla",
    )(gate_up_pad, gate_b, tri, bd64, hm16, hm64, causal, zg, zg)


SOFTMAX_MIN_SUM = 2.0 ** -80


def _softmax_pv(qk_pairs, v, s_ref, p_ref, slot):
    tq = qk_pairs[0][0].shape[0]
    n_keys = v.shape[0]
    rows = len(qk_pairs) * tq
    for i, (q, k) in enumerate(qk_pairs):
        s_ref[i * tq:(i + 1) * tq, 0:n_keys] = _dot_nt(q, k)
    sums = []
    for r0 in range(0, rows, SOFTMAX_ROWS):
        s = s_ref[r0:r0 + SOFTMAX_ROWS, 0:n_keys]
        p = jnp.exp2(s - jnp.max(s, axis=-1, keepdims=True))
        sums.append(jnp.sum(p, axis=-1, keepdims=True))
        p_ref[slot, r0:r0 + SOFTMAX_ROWS, 0:n_keys] = p.astype(BF16)
    l = jnp.concatenate(sums, axis=0)
    o = jnp.dot(p_ref[slot, 0:rows, 0:n_keys], v, preferred_element_type=F32) / l
    return [o[i * tq:(i + 1) * tq] for i in range(len(qk_pairs))]


def _softmax_pv_bounded(qk_pairs, v, bounds, p_ref, slot, shared_keys):
    tq = qk_pairs[0][0].shape[0]
    n_keys = v.shape[0]
    rows = len(qk_pairs) * tq
    if shared_keys:
        q_all = jnp.concatenate([q for q, _ in qk_pairs], axis=0)
        p = jnp.exp2(_dot_nt(q_all, qk_pairs[0][1]) - jnp.concatenate(bounds, axis=0))
        l = jnp.sum(p, axis=-1, keepdims=True)
        p_ref[slot, 0:rows, 0:n_keys] = p.astype(BF16)
    else:
        sums = []
        for i, (q, k) in enumerate(qk_pairs):
            p = jnp.exp2(_dot_nt(q, k) - bounds[i])
            sums.append(jnp.sum(p, axis=-1, keepdims=True))
            p_ref[slot, i * tq:(i + 1) * tq, 0:n_keys] = p.astype(BF16)
        l = jnp.concatenate(sums, axis=0)
    o = jnp.dot(p_ref[slot, 0:rows, 0:n_keys], v, preferred_element_type=F32) / l
    return [o[i * tq:(i + 1) * tq] for i in range(len(qk_pairs))], jnp.min(l)


def _score_bounds(q_ref, key_max, bd):
    a = jnp.abs(q_ref[...].astype(F32)) * key_max
    hi = a.astype(BF16)
    lo = (a - hi.astype(F32)).astype(BF16)
    return jnp.dot(hi, bd, preferred_element_type=F32) + jnp.dot(lo, bd, preferred_element_type=F32)


def _attention_body(units, finish, q_ref, k_ref, bound, o_ref, s_ref, p_ref):
    def run(bounded):
        outs, l_min = [], None
        for u, (v, pairs) in enumerate(units):
            qk = [(q_ref[:, qs], k_ref[:, ks]) for qs, ks, _ in pairs]
            if bounded:
                shared = all(ks == pairs[0][1] for _, ks, _ in pairs)
                o, lm = _softmax_pv_bounded(qk, v, [bound[:, c:c + 1] for _, _, c in pairs], p_ref, u % 2, shared)
                l_min = lm if l_min is None else jnp.minimum(l_min, lm)
            else:
                o = _softmax_pv(qk, v, s_ref, p_ref, u % 2)
            outs.extend(o)
        o_ref[...] = finish(outs).astype(o_ref.dtype)
        return l_min

    l_min = run(True)
    pl.when(jnp.logical_not(l_min >= SOFTMAX_MIN_SUM))(lambda: run(False))


def _key_max(k_ref, kmax_ref):
    @pl.when(pl.program_id(1) == 0)
    def _():
        kmax_ref[...] = jnp.max(jnp.abs(k_ref[...].astype(F32)), axis=0, keepdims=True)
    return kmax_ref[...]


def _gqa_kernel(bd64_ref, q_ref, k_ref, v_ref, *rest):
    o_ref, s_ref, p_ref, kmax_ref = rest[-4:]
    km = _key_max(k_ref, kmax_ref)
    km = jnp.concatenate([km[:, 0:HEAD_W]] * 2 + [km[:, HEAD_W:2 * HEAD_W]] * 2, axis=1)
    bound = _score_bounds(q_ref, km, bd64_ref[...])
    units = []
    for kv in range(2):
        cs = slice(kv * HEAD_W, (kv + 1) * HEAD_W)
        pairs = [(slice(h * HEAD_W, (h + 1) * HEAD_W), cs, h * HEAD_W) for h in (2 * kv, 2 * kv + 1)]
        units.append((v_ref[:, cs], pairs))
    _attention_body(units, lambda outs: jnp.concatenate(outs, axis=1), q_ref, k_ref, bound, o_ref, s_ref, p_ref)


def _dif_kernel(lam_ref, sub_ref, bd64_ref, bd32_ref, q_ref, k_ref, v_ref, *rest, out_scale):
    o_ref, s_ref, p_ref, kmax_ref = rest[-4:]
    bound = _score_bounds(q_ref, _key_max(k_ref, kmax_ref), bd32_ref[...])
    units = []
    for h in range(N_HEADS):
        pairs = []
        for mp in range(2):
            qs = slice((2 * h + mp) * DIF_QK, (2 * h + mp + 1) * DIF_QK)
            pairs.append((qs, qs, (2 * h + mp) * DIF_QK))
        units.append((v_ref[:, h * HEAD_W:(h + 1) * HEAD_W], pairs))

    def finish(outs):
        heads = [outs[2 * h] - lam_ref[:, h * HEAD_W:(h + 1) * HEAD_W] * outs[2 * h + 1] for h in range(N_HEADS)]
        return _group_rms(jnp.concatenate(heads, axis=1), bd64_ref[...], sub_ref[...], HEAD_W) * out_scale

    _attention_body(units, finish, q_ref, k_ref, bound, o_ref, s_ref, p_ref)


def _attention(body, name, pre_specs, pre_args, q, k, v, geo, need_ctx):
    nb, t, n_ctx, tq = geo["batch"], geo["seq"], geo["ctx"], geo["tq"]
    s = t + n_ctx
    kw, vw = k.shape[2], v.shape[2]
    keys = lambda n: pl.BlockSpec((None, n, kw), lambda b, i: (b, 0 if n == s else t // n_ctx, 0),
                                  pipeline_mode=pl.Buffered(1))
    vals = lambda n: pl.BlockSpec((None, n, vw), lambda b, i: (b, 0 if n == s else t // n_ctx, 0),
                                  pipeline_mode=pl.Buffered(1))

    def call(n_keys, q_tiles, q_off, prev):
        qmap = lambda b, i: (b, q_off + i, 0)
        alias = {} if prev is None else {len(pre_args) + 3: 0}
        extra = [] if prev is None else [pl.BlockSpec(memory_space=pl.ANY)]
        return pl.pallas_call(
            body, grid=(nb, q_tiles),
            in_specs=pre_specs + [pl.BlockSpec((None, tq, BRANCH_W), qmap), keys(n_keys), vals(n_keys)] + extra,
            out_specs=pl.BlockSpec((None, tq, BRANCH_W), qmap),
            out_shape=jax.ShapeDtypeStruct(q.shape, BF16),
            scratch_shapes=[pltpu.VMEM((2 * tq, n_keys), F32), pltpu.VMEM((2, 2 * tq, n_keys), BF16),
                            pltpu.VMEM((1, kw), F32)],
            input_output_aliases=alias,
            compiler_params=_cparams(("parallel", "arbitrary")),
            name=name,
        )(*pre_args, q, k, v, *([] if prev is None else [prev]))

    out = call(s, t // tq, 0, None)
    if need_ctx:
        out = call(n_ctx, n_ctx // tq, t // tq, out)
    return out


def _gqa(cq, ck, cv, bd64, geo, need_ctx):
    return _attention(_gqa_kernel, "gqa", [_resident((256, 256))], [bd64], cq, ck, cv, geo, need_ctx)


def _dif(dq, dk, dv, lam_lane, subln, bd64, bd32, geo, need_ctx, out_scale):
    pre = [_resident((1, BRANCH_W)), _resident((1, BRANCH_W)), _resident((256, 256)), _resident((256, 256))]
    return _attention(functools.partial(_dif_kernel, out_scale=out_scale), "diffattn", pre,
                      [lam_lane, subln, bd64, bd32], dq, dk, dv, geo, need_ctx)


def _merge_kernel(x_ref, mod_ref, n1_ref, wg_ref, wb_ref, wo_ref, bd64_ref, gnw_ref,
                  rof_ref, rob_ref, rg_ref, gof_ref, gob_ref, gr_ref, yc_ref, yd_ref, o_ref):
    nb, rt, d = x_ref.shape
    m = nb * rt
    x = x_ref[...]
    hb = _adaln(x, n1_ref[...], mod_ref[:, 0:1, :], mod_ref[:, 1:2, :]).astype(BF16).reshape(m, d)
    flat = lambda ref: ref[...].reshape(m, BRANCH_W)
    bd64 = bd64_ref[...]
    ya = _group_rms(flat(rof_ref) + flat(rob_ref), bd64, None, HEAD_W) * _silu(flat(rg_ref))
    yb = _group_rms(flat(gof_ref) + flat(gob_ref), bd64, gnw_ref[...], HEAD_W) * _silu(flat(gr_ref))
    ys = (ya.astype(BF16), yb.astype(BF16), flat(yc_ref), flat(yd_ref))
    mix = None
    for i in range(4):
        gate = _sigmoid(jnp.dot(hb, wg_ref[:, i * d:(i + 1) * d], preferred_element_type=F32))
        term = gate * jnp.dot(ys[i], wb_ref[i], preferred_element_type=F32)
        mix = term if mix is None else mix + term
    o_ref[...] = x + mod_ref[:, 2:3, :] * _dot(mix, wo_ref[...]).reshape(nb, rt, d)


def _merge(xs, mod, norm1, w_gate, w_branch, w_out, bd64, gla_norm, rof, rob, zr, gof, gob, zg, yc, yd, geo,
           need_ctx):
    nb, s, d = xs.shape
    rt = geo["rt"]
    tiles = (s if need_ctx else geo["seq"]) // rt
    row = lambda i: (0, i, 0)
    col3 = lambda i: (0, i, 3)
    bw = BRANCH_W
    act = lambda imap: pl.BlockSpec((nb, rt, bw), imap)
    return pl.pallas_call(
        _merge_kernel,
        grid=(tiles,),
        in_specs=[pl.BlockSpec((nb, rt, d), row),
                  _mod_spec(nb, d, geo["seq"] // rt),
                  _resident((1, d)), _resident((d, 4 * d)), _resident((4, bw, d)), _resident((d, d)),
                  _resident((256, 256)), _resident((1, bw)),
                  act(row), act(row), act(col3), act(row), act(row), act(col3), act(row), act(row)],
        out_specs=pl.BlockSpec((nb, rt, d), row),
        out_shape=jax.ShapeDtypeStruct((nb, s, d), F32),
        input_output_aliases={0: 0},
        compiler_params=_cparams(("parallel",)),
        name="merge",
    )(xs, mod, norm1, w_gate, w_branch, w_out, bd64, gla_norm, rof, rob, zr, gof, gob, zg, yc, yd)


def _ffn_kernel(x_ref, xp_ref, xn_ref, mod_ref, n2_ref, wa_ref, wg_ref, cw_ref, wd_ref, o_ref, h_ref, acc_ref,
                *, seq, total, n_f):
    nb, rt, d = x_ref.shape
    ext = rt + 2 * HALO
    i = pl.program_id(0)
    j = pl.program_id(1)

    @pl.when(j == 0)
    def _():
        def hn(xx):
            return _adaln(xx, n2_ref[...], mod_ref[:, 3:4, :], mod_ref[:, 4:5, :]).astype(BF16)
        h_ref[:, 0:HALO, :] = hn(xp_ref[...])
        h_ref[:, HALO:HALO + rt, :] = hn(x_ref[...])
        h_ref[:, HALO + rt:ext, :] = hn(xn_ref[...])
        acc_ref[...] = jnp.zeros_like(acc_ref)

    tf = wa_ref.shape[1]
    u = jnp.dot(h_ref[...].reshape(nb * ext, d), wa_ref[...], preferred_element_type=F32)
    centre = lambda a: a.reshape(nb, ext, tf)[:, HALO:HALO + rt, :]
    up = centre(pltpu.roll(u, 1, 0))
    mid = centre(u)
    dn = centre(pltpu.roll(u, nb * ext - 1, 0))
    pos = i * rt + lax.broadcasted_iota(jnp.int32, (rt, 1), 0)
    keep_up = jnp.where(pos == 0, 0.0, 1.0) * jnp.where(pos == seq, 0.0, 1.0)
    keep_dn = jnp.where(pos == seq - 1, 0.0, 1.0) * jnp.where(pos == total - 1, 0.0, 1.0)
    a = ((up * keep_up) * cw_ref[0:1, :] + mid * cw_ref[1:2, :] + (dn * keep_dn) * cw_ref[2:3, :]
         + cw_ref[3:4, :])
    gate = jnp.dot(h_ref[:, HALO:HALO + rt, :].reshape(nb * rt, d), wg_ref[...], preferred_element_type=F32)
    acc_ref[...] += _dot((_silu(a) * gate.reshape(nb, rt, tf)).reshape(nb * rt, tf), wd_ref[...])

    @pl.when(j == n_f - 1)
    def _():
        o_ref[...] = x_ref[...] + mod_ref[:, 5:6, :] * acc_ref[...].reshape(nb, rt, d)


def _ffn(xs, mod, norm2, w_up, conv_pack, w_down, geo, need_ctx):
    nb, s, d = xs.shape
    rt, tf, t = geo["rt_ffn"], geo["tf"], geo["seq"]
    s_out = s if need_ctx else t
    n_f = D_FF_W // tf
    hb = rt // HALO
    last_blk = s // HALO - 1
    row = lambda i, j: (0, i, 0)
    return pl.pallas_call(
        functools.partial(_ffn_kernel, seq=t, total=s, n_f=n_f),
        grid=(s_out // rt, n_f),
        in_specs=[pl.BlockSpec((nb, rt, d), row),
                  pl.BlockSpec((nb, HALO, d), lambda i, j: (0, jnp.maximum(i * hb - 1, 0), 0)),
                  pl.BlockSpec((nb, HALO, d), lambda i, j: (0, jnp.minimum((i + 1) * hb, last_blk), 0)),
                  _mod_spec(nb, d, t // rt),
                  _resident((1, d)),
                  pl.BlockSpec((d, tf), lambda i, j: (0, j)),
                  pl.BlockSpec((d, tf), lambda i, j: (0, n_f + j)),
                  pl.BlockSpec((8, tf), lambda i, j: (0, j)),
                  pl.BlockSpec((tf, d), lambda i, j: (j, 0))],
        out_specs=pl.BlockSpec((nb, rt, d), row),
        out_shape=jax.ShapeDtypeStruct((nb, s_out, d), F32),
        scratch_shapes=[pltpu.VMEM((nb, rt + 2 * HALO, d), BF16), pltpu.VMEM((nb * rt, d), F32)],
        compiler_params=_cparams(("parallel", "arbitrary")),
        name="convffn",
    )(xs, xs, xs, mod, norm2, w_up, w_up, conv_pack, w_down)


def _rope_tables(t, n_ctx):
    rows = t // GRID_WIDTH

    def axial(dim):
        axis_dim = dim // 2
        freqs = ROPE_BASE ** (-jnp.arange(0, axis_dim, 2, dtype=F32) / axis_dim)
        row = jnp.repeat(jnp.arange(rows, dtype=F32), GRID_WIDTH)
        col = jnp.tile(jnp.arange(GRID_WIDTH, dtype=F32), rows)
        ang = jnp.concatenate([row[:, None] * freqs, col[:, None] * freqs], axis=-1)
        return jnp.cos(ang), jnp.sin(ang)

    freqs = ROPE_BASE ** (-jnp.linspace(0.0, 1.0, HEAD_W // 2, dtype=F32))
    ang = jnp.arange(t, dtype=F32)[:, None] * freqs
    parts = []
    for cos, sin in ((jnp.cos(ang), jnp.sin(ang)), axial(HEAD_W), axial(DIF_QK)):
        reps = LANES // (2 * cos.shape[1])
        parts.append(jnp.tile(cos, (1, 2 * reps)))
        parts.append(jnp.tile(jnp.concatenate([-sin, sin], axis=1), (1, reps)))
    table = jnp.concatenate(parts, axis=1)
    ident = jnp.tile(jnp.concatenate([jnp.ones((1, LANES), F32), jnp.zeros((1, LANES), F32)], axis=1), (n_ctx, 3))
    return jnp.concatenate([table, ident], axis=0)


def _block_diag_ones(width, group):
    idx = jnp.arange(width) // group
    return (idx[:, None] == idx[None, :]).astype(BF16)


def kernel(x, c, ctx, c_ctx, w_mod, b_mod, norm1, norm2, w_in, ret_decay, gla_gate_up, gla_gate_b, gla_norm,
           gqa_qnorm, gqa_knorm, dif_qnorm, dif_knorm, dif_lambda, dif_subln, w_branch, w_out, w_up, conv_w,
           conv_b, w_down):
    nb, t, d = x.shape
    n_ctx = ctx.shape[1]
    depth = w_mod.shape[0]
    rt = 128
    assert t % 256 == 0 and n_ctx % 256 == 0 and nb + 1 <= 8
    geo = dict(batch=nb, seq=t, ctx=n_ctx, rt=rt, rt_ffn=128, tf=D_FF_W // 2, tq=256)

    xs = jnp.concatenate([x, ctx], axis=1)
    cc = jnp.zeros((8, d), F32).at[0:nb].set(c).at[nb].set(c_ctx)
    mod_all = _modulation(cc, w_mod, b_mod).reshape(depth, 8, 6, d)

    table = _rope_tables(t, n_ctx)
    bd64 = _block_diag_ones(256, HEAD_W)
    bd32 = _block_diag_ones(256, DIF_QK)
    mc = nb * GLA_CHUNK_LEN
    ti = jnp.arange(mc)
    same_chunk = (ti[:, None] // GLA_CHUNK_LEN) == (ti[None, :] // GLA_CHUNK_LEN)
    tri = jnp.stack([same_chunk & (ti[:, None] >= ti[None, :]), same_chunk & (ti[:, None] <= ti[None, :])]).astype(BF16)
    lane_head = jnp.arange(BRANCH_W)[None, :] // HEAD_W
    hm16 = (jnp.arange(N_HEADS * GLA_SUB)[:, None] // GLA_SUB == lane_head).astype(F32)
    hm64 = (jnp.arange(N_HEADS * GLA_CHUNK_LEN)[:, None] // GLA_CHUNK_LEN == lane_head).astype(F32)
    hm128 = (jnp.arange(N_HEADS * RET_CHUNK_LEN)[:, None] // RET_CHUNK_LEN == lane_head).astype(F32)
    qi = jnp.arange(N_HEADS * GLA_CHUNK_LEN)[:, None] % GLA_CHUNK_LEN
    kj = jnp.arange(GLA_CHUNK_LEN)[None, :]
    causal = jnp.stack([kj <= qi, kj >= qi]).astype(F32)
    n_mix = 2080
    n_gate0 = 3360

    for l in range(depth):
        need_ctx = l < depth - 1
        w = w_in[l]
        w_proj = jnp.concatenate([w[:, :n_mix], jnp.zeros((d, LANES - 2 * GLA_RANK_W), F32),
                                  w[:, n_mix:n_gate0]], axis=1).astype(BF16)
        w_gate = w[:, n_gate0:].astype(BF16)
        qk_norms = jnp.zeros((8, 256), F32)
        qk_norms = qk_norms.at[0].set(jnp.tile(gqa_qnorm[l], 4)).at[1].set(jnp.tile(gqa_knorm[l], 4))
        qk_norms = qk_norms.at[2].set(jnp.tile(dif_qnorm[l], 8)).at[3].set(jnp.tile(dif_knorm[l], 8))
        mod = jnp.stack([mod_all[l, 0:nb], jnp.broadcast_to(mod_all[l, nb], (nb, 6, d))])
        n1 = norm1[l][None, :]
        n2 = norm2[l][None, :]

        zr, zg, cq, ck, cv, dq, dk, dv = _inproj(xs, mod, n1, w_proj, table, bd64, bd32, qk_norms, geo)

        log_gamma = jax.nn.log_sigmoid(ret_decay[l].astype(F32))
        lg_rows = jnp.broadcast_to(jnp.repeat(log_gamma, RET_CHUNK_LEN, axis=1)[:, :, None],
                                   (2, N_HEADS * RET_CHUNK_LEN, RET_CHUNK_LEN))
        lg_lane = jnp.repeat(log_gamma, HEAD_W, axis=1)
        rof, rob = _retention(zr, lg_rows, lg_lane, hm128, bd64, geo)

        gu = jnp.zeros((2, LANES, BRANCH_W), F32)
        gu = gu.at[0, 0:GLA_RANK_W].set(gla_gate_up[l, 0]).at[1, GLA_RANK_W:2 * GLA_RANK_W].set(gla_gate_up[l, 1])
        gof, gob = _gla(zg, gu.astype(BF16), gla_gate_b[l], tri, bd64, hm16, hm64, causal, geo)

        yc = _gqa(cq, ck, cv, bd64, geo, need_ctx)

        lam_init = 0.8 - 0.6 * math.exp(-0.3 * l)
        lv = dif_lambda[l].astype(F32)
        lam = jnp.exp(jnp.sum(lv[0] * lv[1])) - jnp.exp(jnp.sum(lv[2] * lv[3])) + lam_init
        lam_lane = jnp.broadcast_to(lam, (1, BRANCH_W)).astype(F32)
        subln = jnp.tile(dif_subln[l], 4)[None, :]
        yd = _dif(dq, dk, dv, lam_lane, subln, bd64, bd32, geo, need_ctx, 1.0 - lam_init)

        gnw = jnp.tile(gla_norm[l], 4)[None, :]
        xs = _merge(xs, mod, n1, w_gate, w_branch[l].astype(BF16), w_out[l].astype(BF16), bd64, gnw,
                    rof, rob, zr, gof, gob, zg, yc, yd, geo, need_ctx)

        conv_pack = jnp.zeros((8, D_FF_W), F32).at[0:3].set(conv_w[l]).at[3].set(conv_b[l])
        xs = _ffn(xs, mod, n2, w_up[l].astype(BF16), conv_pack, w_down[l].astype(BF16), geo, need_ctx)

    return xs
```

```python
import functools
import math

import jax
import jax.numpy as jnp
import numpy as np
from jax import lax
from jax.experimental import pallas as pl
from jax.experimental.pallas import tpu as pltpu

F32 = jnp.float32
BF16 = jnp.bfloat16

N_HEADS = 4
HEAD_W = 64
BRANCH_W = N_HEADS * HEAD_W
DIF_QK = 32
RET_CHUNK_LEN = 128
GLA_CHUNK_LEN = 64
GLA_SUB = 16
GLA_RANK_W = 16
GLA_TAU_INV = 1.0 / 16.0
GRID_WIDTH = 64
ROPE_BASE = 10000.0
EPS = 1e-6
D_FF_W = 2816
LOG2E = 1.4426950408889634

LANES = 128
VMEM_LIMIT_BYTES = 56 * 1024 * 1024

RET_OFF, GLA_OFF, GQA_OFF, DIF_OFF, PROJ_W = 0, 1024, 2176, 2688, 3456
GLA_GROUP_W = GQA_OFF - GLA_OFF
HALO = 16
SOFTMAX_ROWS = 128


def _cparams(sem):
    return pltpu.CompilerParams(dimension_semantics=sem, vmem_limit_bytes=VMEM_LIMIT_BYTES)


def _resident(shape, layer=None):
    nd = len(shape)
    if layer is None:
        return pl.BlockSpec(shape, lambda *_: (0,) * nd, pipeline_mode=pl.Buffered(1))
    return pl.BlockSpec((None,) + tuple(shape), lambda *_: (layer,) + (0,) * nd, pipeline_mode=pl.Buffered(1))


def _dot(a, b):
    return jnp.dot(a.astype(BF16), b.astype(BF16), preferred_element_type=F32)


def _dot_nt(a, b):
    return lax.dot_general(a.astype(BF16), b.astype(BF16), (((1,), (1,)), ((), ())),
                           preferred_element_type=F32)


def _dot_tn(a, b):
    return lax.dot_general(a.astype(BF16), b.astype(BF16), (((0,), (0,)), ((), ())),
                           preferred_element_type=F32)


def _sigmoid(x):
    return 1.0 / (1.0 + jnp.exp(-x))


def _silu(x):
    return x * _sigmoid(x)


def _adaln(x, norm_w, shift, scale):
    h = x * lax.rsqrt(jnp.mean(x * x, axis=-1, keepdims=True) + EPS) * norm_w
    return h * (1.0 + scale) + shift


def _group_rms(x, bd, weight, group):
    x2 = x * x
    hi = x2.astype(BF16)
    lo = (x2 - hi.astype(F32)).astype(BF16)
    ss = jnp.dot(hi, bd, preferred_element_type=F32) + jnp.dot(lo, bd, preferred_element_type=F32)
    y = x * lax.rsqrt(ss * (1.0 / group) + EPS)
    return y if weight is None else y * weight


def _rope(x, cos, sin_signed, half):
    w = x.shape[1]
    reps = w // LANES
    if reps > 1:
        cos = jnp.concatenate([cos] * reps, axis=1)
        sin_signed = jnp.concatenate([sin_signed] * reps, axis=1)
    lane = lax.broadcasted_iota(jnp.int32, x.shape, 1)
    lower = (lane & (2 * half - 1)) < half
    partner = jnp.where(lower, pltpu.roll(x, w - half, 1), pltpu.roll(x, half, 1))
    return x * cos + partner * sin_signed


def _mod_kernel(c_ref, w_ref, b_ref, o_ref):
    o_ref[...] = _dot(_silu(c_ref[...]), w_ref[...]) + b_ref[...]


def _modulation(cc, w_mod, b_mod):
    n_layers, d, width = w_mod.shape
    tn = 1024
    return pl.pallas_call(
        _mod_kernel,
        grid=(n_layers, width // tn),
        in_specs=[pl.BlockSpec((8, d), lambda l, j: (0, 0)),
                  pl.BlockSpec((None, d, tn), lambda l, j: (l, 0, j)),
                  pl.BlockSpec((None, 1, tn), lambda l, j: (l, 0, j))],
        out_specs=pl.BlockSpec((None, 8, tn), lambda l, j: (l, 0, j)),
        out_shape=jax.ShapeDtypeStruct((n_layers, 8, width), F32),
        compiler_params=_cparams(("parallel", "parallel")),
        name="modulation",
    )(cc, w_mod, b_mod.reshape(n_layers, 1, width))


def _mod_spec(nb, d, n_lat_tiles, layer):
    return pl.BlockSpec((None, None, nb, 6, d),
                        lambda i, *_: (layer, jnp.where(i < n_lat_tiles, 0, 1), 0, 0, 0))


def _inproj_kernel(x_ref, mod_ref, n1_ref, w_ref, tab_ref, bd64_ref, bd32_ref, nw_ref,
                   zr_ref, zg_ref, cq_ref, ck_ref, cv_ref, dq_ref, dk_ref, dv_ref):
    nb, rt, d = x_ref.shape
    m = nb * rt
    hb = _adaln(x_ref[...], n1_ref[...], mod_ref[:, 0:1, :], mod_ref[:, 1:2, :]).astype(BF16).reshape(m, d)

    def table(lo):
        t = tab_ref[:, lo:lo + LANES]
        return jnp.concatenate([t] * nb, axis=0)

    r_cos, r_sin, g_cos, g_sin, d_cos, d_sin = (table(i * LANES) for i in range(6))

    def put(ref, lo, val):
        ref[:, :, lo:lo + val.shape[1]] = val.reshape(nb, rt, val.shape[1]).astype(ref.dtype)

    zr = jnp.dot(hb, w_ref[:, RET_OFF:GLA_OFF], preferred_element_type=F32)
    put(zr_ref, 0, _rope(zr[:, 0:256], r_cos, r_sin, 32))
    put(zr_ref, 256, _rope(zr[:, 256:512], r_cos, r_sin, 32) * (HEAD_W ** -0.5))
    put(zr_ref, 512, zr[:, 512:1024])

    zg = jnp.dot(hb, w_ref[:, GLA_OFF:GQA_OFF], preferred_element_type=F32)
    put(zg_ref, 0, zg[:, 0:256] * (HEAD_W ** -0.5))
    put(zg_ref, 256, zg[:, 256:GLA_GROUP_W])

    zc = jnp.dot(hb, w_ref[:, GQA_OFF:DIF_OFF], preferred_element_type=F32)
    bd64 = bd64_ref[...]
    q = _group_rms(zc[:, 0:256], bd64, nw_ref[0:1, :], HEAD_W)
    put(cq_ref, 0, _rope(q, g_cos, g_sin, 32) * (HEAD_W ** -0.5 * LOG2E))
    k = _group_rms(zc[:, 256:384], bd64[0:128, 0:128], nw_ref[1:2, 0:128], HEAD_W)
    put(ck_ref, 0, _rope(k, g_cos, g_sin, 32))
    put(cv_ref, 0, zc[:, 384:512])

    zd = jnp.dot(hb, w_ref[:, DIF_OFF:PROJ_W], preferred_element_type=F32)
    bd32 = bd32_ref[...]
    q = _group_rms(zd[:, 0:256], bd32, nw_ref[2:3, :], DIF_QK)
    put(dq_ref, 0, _rope(q, d_cos, d_sin, 16) * (DIF_QK ** -0.5 * LOG2E))
    k = _group_rms(zd[:, 256:512], bd32, nw_ref[3:4, :], DIF_QK)
    put(dk_ref, 0, _rope(k, d_cos, d_sin, 16))
    put(dv_ref, 0, zd[:, 512:768])


def _inproj(xs, mod, norm1, w_proj, table, bd64, bd32, qk_norms, geo, layer):
    nb, s, d = xs.shape
    rt = geo["rt"]
    row = lambda i: (0, i, 0)
    widths = (1024, GLA_GROUP_W, 256, 128, 128, 256, 256, 256)
    dtypes = (F32, F32, BF16, BF16, BF16, BF16, BF16, BF16)
    return pl.pallas_call(
        _inproj_kernel,
        grid=(s // rt,),
        in_specs=[pl.BlockSpec((nb, rt, d), row),
                  _mod_spec(nb, d, geo["seq"] // rt, layer),
                  _resident((1, d), layer),
                  _resident((d, PROJ_W), layer),
                  pl.BlockSpec((rt, 768), lambda i: (i, 0)),
                  _resident((256, 256)),
                  _resident((256, 256)),
                  _resident((8, 256), layer)],
        out_specs=[pl.BlockSpec((nb, rt, w), row) for w in widths],
        out_shape=[jax.ShapeDtypeStruct((nb, s, w), t) for w, t in zip(widths, dtypes)],
        compiler_params=_cparams(("parallel",)),
        name="inproj",
    )(xs, mod, norm1, w_proj, table, bd64, bd32, qk_norms)


def _scan_index_maps(geo, chunk):
    nl, nc = geo["seq"] // chunk, geo["ctx"] // chunk
    fwd = lambda s: (0, jnp.where(s < nc, nl + s, s - nc), 0)
    bwd = lambda s: (0, jnp.where(s < nc, nl + nc - 1 - s, nl - 1 - (s - nc)), 0)
    return fwd, bwd, nl + nc


def _ret_kernel(lgr_ref, lgl_ref, hm_ref, bd64_ref, zf_ref, zb_ref, of_ref, ob_ref, st_ref):
    nb = zf_ref.shape[0]
    c = RET_CHUNK_LEN

    @pl.when(pl.program_id(0) == 0)
    def _():
        st_ref[...] = jnp.zeros_like(st_ref)

    hm = hm_ref[...]
    bd64_f = bd64_ref[...].astype(F32)
    ii = lax.broadcasted_iota(jnp.int32, (N_HEADS * c, c), 0) & (c - 1)
    jj = lax.broadcasted_iota(jnp.int32, (N_HEADS * c, c), 1)
    ri = lax.broadcasted_iota(jnp.int32, (c, BRANCH_W), 0).astype(F32)
    for rev, z_ref, o_ref in ((False, zf_ref, of_ref), (True, zb_ref, ob_ref)):
        d = int(rev)
        dist = ((jj - ii) if rev else (ii - jj)).astype(F32)
        dec = jnp.where(dist >= 0, jnp.exp(lgr_ref[d] * jnp.maximum(dist, 0.0)), 0.0)
        lgl = lgl_ref[d:d + 1, :]
        q_dec = jnp.exp(lgl * ((c - ri) if rev else (ri + 1.0)))
        k_dec = jnp.exp(lgl * (ri if rev else (c - 1.0 - ri)))
        c_dec = jnp.exp(lgl * float(c))
        for b in range(nb):
            q = z_ref[b, :, 0:256]
            k = z_ref[b, :, 256:512]
            v = z_ref[b, :, 512:768]
            q4 = jnp.concatenate([q] * N_HEADS, axis=0) * hm
            o4 = _dot(_dot_nt(q4, k) * dec, v) * hm
            st = st_ref[d, b]
            o_ref[b] = o4[0:c] + o4[c:2 * c] + o4[2 * c:3 * c] + o4[3 * c:4 * c] + _dot(q, st) * q_dec
            st_ref[d, b] = st * c_dec + _dot_tn(k * k_dec, v) * bd64_f


def _retention(zr, lg_rows, lg_lane, head_mask, bd64, geo, layer):
    nb, s, _ = zr.shape
    c = RET_CHUNK_LEN
    fwd, bwd, steps = _scan_index_maps(geo, c)
    return pl.pallas_call(
        _ret_kernel,
        grid=(steps,),
        in_specs=[_resident((2, N_HEADS * c, c), layer), _resident((2, BRANCH_W), layer),
                  _resident((N_HEADS * c, BRANCH_W)),
                  _resident((256, 256)),
                  pl.BlockSpec((nb, c, 1024), fwd), pl.BlockSpec((nb, c, 1024), bwd)],
        out_specs=[pl.BlockSpec((nb, c, BRANCH_W), fwd), pl.BlockSpec((nb, c, BRANCH_W), bwd)],
        out_shape=[jax.ShapeDtypeStruct((nb, s, BRANCH_W), F32)] * 2,
        scratch_shapes=[pltpu.VMEM((2, nb, BRANCH_W, BRANCH_W), F32)],
        compiler_params=_cparams(("arbitrary",)),
        name="retention",
    )(lg_rows, lg_lane, head_mask, bd64, zr, zr)


GLA_FACTOR_MIN_LOG_DECAY = -40.0
GLA_FACTOR_MAX_KEY = 1e15


def _gla_prepare(z_ref, d, gu_ref, gb_ref, tri_ref):
    nb, c, _ = z_ref.shape
    m = nb * c
    q = z_ref[:, :, 0:256].reshape(m, BRANCH_W)
    k = z_ref[:, :, 256:512].reshape(m, BRANCH_W)
    v = z_ref[:, :, 512:768].reshape(m, BRANCH_W)
    logit = _dot(z_ref[:, :, 1024:GLA_GROUP_W].reshape(m, LANES), gu_ref[d]) + gb_ref[d:d + 1, :]
    g = (jnp.minimum(logit, 0.0) - jnp.log(1.0 + jnp.exp(-jnp.abs(logit)))) * GLA_TAU_INV
    g1 = g.astype(BF16)
    r1 = g - g1.astype(F32)
    g2 = r1.astype(BF16)
    g3 = (r1 - g2.astype(F32)).astype(BF16)
    tri = tri_ref[d]
    bcum = (jnp.dot(tri, g1, preferred_element_type=F32) + jnp.dot(tri, g2, preferred_element_type=F32)
            + jnp.dot(tri, g3, preferred_element_type=F32))
    return q, k, v, bcum


def _gla_state_step(st_ref, d, b, q_state_b, kb, vb, bb, rev, bd64_f):
    c = kb.shape[0]
    b_last = bb[0:1, :] if rev else bb[c - 1:c, :]
    st = st_ref[d, b]
    o_b = _dot_nt(q_state_b, st)
    st_ref[d, b] = st * jnp.exp(b_last) + _dot_tn(vb, kb * jnp.exp(b_last - bb)) * bd64_f
    return o_b


def _gla_factored(d, rev, q, k, v, bcum, o_ref, st_ref, hm64_ref, causal_ref, bd64_f):
    nb, c, _ = o_ref.shape
    q_state = q * jnp.exp(bcum)
    k_inv = k * jnp.exp(-bcum)
    hm = hm64_ref[...]
    causal = causal_ref[d]
    outs = []
    for b in range(nb):
        rows = slice(b * c, (b + 1) * c)
        qb, vb = q_state[rows], v[rows]
        q4 = jnp.concatenate([qb] * N_HEADS, axis=0) * hm
        att = jnp.where(causal > 0.0, _dot_nt(q4, k_inv[rows]), 0.0)
        o4 = _dot(att, vb) * hm
        o_b = o4[0:c] + o4[c:2 * c] + o4[2 * c:3 * c] + o4[3 * c:4 * c]
        outs.append(o_b + _gla_state_step(st_ref, d, b, qb, k[rows], vb, bcum[rows], rev, bd64_f))
    o_ref[...] = jnp.concatenate(outs, axis=0).reshape(nb, c, BRANCH_W)


def _gla_pairwise(d, rev, q, k, v, bcum, o_ref, st_ref, w_ref, hm16_ref, bd64, bd64_f):
    nb, c, _ = o_ref.shape
    sub = GLA_SUB
    nsub = c // sub
    m = nb * c
    head_mask = hm16_ref[...]
    rin = lax.broadcasted_iota(jnp.int32, (m, BRANCH_W), 0) & (sub - 1)
    for dl in range(sub):
        if dl == 0:
            w = q * k
        else:
            shift = (m - dl) if rev else dl
            ks = pltpu.roll(k, shift, 0)
            bs = pltpu.roll(bcum, shift, 0)
            valid = (rin + dl <= sub - 1) if rev else (rin >= dl)
            w = jnp.where(valid, q * ks * jnp.exp(jnp.minimum(bcum - bs, 0.0)), 0.0)
        w_ref[dl * m:(dl + 1) * m, :] = w.astype(BF16)
    att = jnp.dot(w_ref[...], bd64, preferred_element_type=F32)
    o_acc = att[0:m] * v
    for dl in range(1, sub):
        o_acc = o_acc + att[dl * m:(dl + 1) * m] * pltpu.roll(v, (m - dl) if rev else dl, 0)

    q_state = q * jnp.exp(bcum)
    outs = []
    for b in range(nb):
        r0 = b * c
        qb, kb, vb, bb = q[r0:r0 + c], k[r0:r0 + c], v[r0:r0 + c], bcum[r0:r0 + c]
        rows = []
        for blk in range(nsub):
            a0, a1 = blk * sub, (blk + 1) * sub
            if rev and blk < nsub - 1:
                anchor, lo, hi = bb[a1:a1 + 1, :], a1, c
            elif (not rev) and blk > 0:
                anchor, lo, hi = bb[a0 - 1:a0, :], 0, a0
            else:
                rows.append(jnp.zeros((sub, BRANCH_W), F32))
                continue
            qa = qb[a0:a1] * jnp.exp(bb[a0:a1] - anchor)
            ka = kb[lo:hi] * jnp.exp(anchor - bb[lo:hi])
            q4 = jnp.concatenate([qa] * N_HEADS, axis=0) * head_mask
            o4 = _dot(_dot_nt(q4, ka), vb[lo:hi]) * head_mask
            rows.append(o4[0:sub] + o4[sub:2 * sub] + o4[2 * sub:3 * sub] + o4[3 * sub:4 * sub])
        o_b = _gla_state_step(st_ref, d, b, q_state[r0:r0 + c], kb, vb, bb, rev, bd64_f)
        outs.append(o_b + jnp.concatenate(rows, axis=0))
    o_ref[...] = (o_acc + jnp.concatenate(outs, axis=0)).reshape(nb, c, BRANCH_W)


def _gla_kernel(gu_ref, gb_ref, tri_ref, bd64_ref, hm16_ref, hm64_ref, causal_ref, zf_ref, zb_ref,
                of_ref, ob_ref, st_ref, w_ref):
    @pl.when(pl.program_id(0) == 0)
    def _():
        st_ref[...] = jnp.zeros_like(st_ref)

    bd64 = bd64_ref[...]
    bd64_f = bd64.astype(F32)
    fwd = _gla_prepare(zf_ref, 0, gu_ref, gb_ref, tri_ref)
    bwd = _gla_prepare(zb_ref, 1, gu_ref, gb_ref, tri_ref)
    min_decay = jnp.minimum(jnp.min(fwd[3]), jnp.min(bwd[3]))
    max_key = jnp.maximum(jnp.max(jnp.abs(fwd[1])), jnp.max(jnp.abs(bwd[1])))
    factor_ok = jnp.logical_and(min_decay > GLA_FACTOR_MIN_LOG_DECAY, max_key < GLA_FACTOR_MAX_KEY)

    @pl.when(factor_ok)
    def _():
        _gla_factored(0, False, *fwd, of_ref, st_ref, hm64_ref, causal_ref, bd64_f)
        _gla_factored(1, True, *bwd, ob_ref, st_ref, hm64_ref, causal_ref, bd64_f)

    @pl.when(jnp.logical_not(factor_ok))
    def _():
        _gla_pairwise(0, False, *fwd, of_ref, st_ref, w_ref, hm16_ref, bd64, bd64_f)
        _gla_pairwise(1, True, *bwd, ob_ref, st_ref, w_ref, hm16_ref, bd64, bd64_f)


def _gla(zg, gate_up_pad, gate_b, tri, bd64, hm16, hm64, causal, geo, layer):
    nb, s, _ = zg.shape
    c = GLA_CHUNK_LEN
    fwd, bwd, steps = _scan_index_maps(geo, c)
    return pl.pallas_call(
        _gla_kernel,
        grid=(steps,),
        in_specs=[_resident((2, LANES, BRANCH_W), layer), _resident((2, BRANCH_W), layer),
                  _resident((2, nb * c, nb * c)), _resident((256, 256)), _resident((N_HEADS * GLA_SUB, BRANCH_W)),
                  _resident((N_HEADS * c, BRANCH_W)), _resident((2, N_HEADS * c, c)),
                  pl.BlockSpec((nb, c, GLA_GROUP_W), fwd), pl.BlockSpec((nb, c, GLA_GROUP_W), bwd)],
        out_specs=[pl.BlockSpec((nb, c, BRANCH_W), fwd), pl.BlockSpec((nb, c, BRANCH_W), bwd)],
        out_shape=[jax.ShapeDtypeStruct((nb, s, BRANCH_W), F32)] * 2,
        scratch_shapes=[pltpu.VMEM((2, nb, BRANCH_W, BRANCH_W), F32),
                        pltpu.VMEM((GLA_SUB * nb * c, BRANCH_W), BF16)],
        compiler_params=_cparams(("arbitrary",)),
        name="gla",
    )(gate_up_pad, gate_b, tri, bd64, hm16, hm64, causal, zg, zg)


SOFTMAX_MIN_SUM = 2.0 ** -80


def _softmax_pv(qk_pairs, v, s_ref, p_ref, slot):
    tq = qk_pairs[0][0].shape[0]
    n_keys = v.shape[0]
    rows = len(qk_pairs) * tq
    for i, (q, k) in enumerate(qk_pairs):
        s_ref[i * tq:(i + 1) * tq, 0:n_keys] = _dot_nt(q, k)
    sums = []
    for r0 in range(0, rows, SOFTMAX_ROWS):
        s = s_ref[r0:r0 + SOFTMAX_ROWS, 0:n_keys]
        p = jnp.exp2(s - jnp.max(s, axis=-1, keepdims=True))
        sums.append(jnp.sum(p, axis=-1, keepdims=True))
        p_ref[slot, r0:r0 + SOFTMAX_ROWS, 0:n_keys] = p.astype(BF16)
    l = jnp.concatenate(sums, axis=0)
    o = jnp.dot(p_ref[slot, 0:rows, 0:n_keys], v, preferred_element_type=F32) / l
    return [o[i * tq:(i + 1) * tq] for i in range(len(qk_pairs))]


def _softmax_pv_bounded(qk_pairs, v, bounds, p_ref, slot, shared_keys, mix=None):
    tq = qk_pairs[0][0].shape[0]
    n_keys = v.shape[0]
    rows = len(qk_pairs) * tq
    if shared_keys:
        q_all = jnp.concatenate([q for q, _ in qk_pairs], axis=0)
        p = jnp.exp2(_dot_nt(q_all, qk_pairs[0][1]) - jnp.concatenate(bounds, axis=0))
        l = jnp.sum(p, axis=-1, keepdims=True)
        p_ref[slot, 0:rows, 0:n_keys] = p.astype(BF16)
    else:
        sums = []
        for i, (q, k) in enumerate(qk_pairs):
            p = jnp.exp2(_dot_nt(q, k) - bounds[i])
            sums.append(jnp.sum(p, axis=-1, keepdims=True))
            p_ref[slot, i * tq:(i + 1) * tq, 0:n_keys] = p.astype(BF16)
        l = jnp.concatenate(sums, axis=0)
        if mix is not None:
            l0, l1 = sums
            p_mix = (p_ref[slot, 0:tq, 0:n_keys].astype(F32)
                     - (mix * l0 / l1) * p_ref[slot, tq:2 * tq, 0:n_keys].astype(F32))
            p_ref[slot, 0:tq, 0:n_keys] = p_mix.astype(BF16)
            o = jnp.dot(p_ref[slot, 0:tq, 0:n_keys], v, preferred_element_type=F32) / l0
            return [o], jnp.min(l)
    o = jnp.dot(p_ref[slot, 0:rows, 0:n_keys], v, preferred_element_type=F32) / l
    return [o[i * tq:(i + 1) * tq] for i in range(len(qk_pairs))], jnp.min(l)


def _score_bounds(q_ref, key_max, bd):
    a = jnp.abs(q_ref[...].astype(F32)) * key_max
    hi = a.astype(BF16)
    lo = (a - hi.astype(F32)).astype(BF16)
    return jnp.dot(hi, bd, preferred_element_type=F32) + jnp.dot(lo, bd, preferred_element_type=F32)


def _attention_body(units, finish, q_ref, k_ref, bound, o_ref, s_ref, p_ref, mixes=None):
    def run(bounded):
        outs, l_min = [], None
        for u, (v, pairs) in enumerate(units):
            qk = [(q_ref[:, qs], k_ref[:, ks]) for qs, ks, _ in pairs]
            if bounded:
                shared = all(ks == pairs[0][1] for _, ks, _ in pairs)
                o, lm = _softmax_pv_bounded(qk, v, [bound[:, c:c + 1] for _, _, c in pairs], p_ref, u % 2, shared,
                                            None if mixes is None else mixes[u])
                l_min = lm if l_min is None else jnp.minimum(l_min, lm)
            else:
                o = _softmax_pv(qk, v, s_ref, p_ref, u % 2)
            outs.extend(o)
        o_ref[...] = finish(outs, bounded and mixes is not None).astype(o_ref.dtype)
        return l_min

    l_min = run(True)
    pl.when(jnp.logical_not(l_min >= SOFTMAX_MIN_SUM))(lambda: run(False))


def _key_max(k_ref, kmax_ref):
    @pl.when(pl.program_id(1) == 0)
    def _():
        kmax_ref[...] = jnp.max(jnp.abs(k_ref[...].astype(F32)), axis=0, keepdims=True)
    return kmax_ref[...]


def _gqa_kernel(bd64_ref, q_ref, k_ref, v_ref, *rest):
    o_ref, s_ref, p_ref, kmax_ref = rest[-4:]
    km = _key_max(k_ref, kmax_ref)
    km = jnp.concatenate([km[:, 0:HEAD_W]] * 2 + [km[:, HEAD_W:2 * HEAD_W]] * 2, axis=1)
    bound = _score_bounds(q_ref, km, bd64_ref[...])
    units = []
    for kv in range(2):
        cs = slice(kv * HEAD_W, (kv + 1) * HEAD_W)
        pairs = [(slice(h * HEAD_W, (h + 1) * HEAD_W), cs, h * HEAD_W) for h in (2 * kv, 2 * kv + 1)]
        units.append((v_ref[:, cs], pairs))
    _attention_body(units, lambda outs, _: jnp.concatenate(outs, axis=1), q_ref, k_ref, bound, o_ref, s_ref, p_ref)


def _dif_kernel(lam_ref, sub_ref, bd64_ref, bd32_ref, q_ref, k_ref, v_ref, *rest, out_scale):
    o_ref, s_ref, p_ref, kmax_ref = rest[-4:]
    bound = _score_bounds(q_ref, _key_max(k_ref, kmax_ref), bd32_ref[...])
    units = []
    for h in range(N_HEADS):
        pairs = []
        for mp in range(2):
            qs = slice((2 * h + mp) * DIF_QK, (2 * h + mp + 1) * DIF_QK)
            pairs.append((qs, qs, (2 * h + mp) * DIF_QK))
        units.append((v_ref[:, h * HEAD_W:(h + 1) * HEAD_W], pairs))

    def finish(outs, merged):
        heads = outs if merged else [outs[2 * h] - lam_ref[:, h * HEAD_W:(h + 1) * HEAD_W] * outs[2 * h + 1]
                                     for h in range(N_HEADS)]
        return _group_rms(jnp.concatenate(heads, axis=1), bd64_ref[...], sub_ref[...], HEAD_W) * out_scale

    _attention_body(units, finish, q_ref, k_ref, bound, o_ref, s_ref, p_ref, mixes=[lam_ref[:, 0:1]] * N_HEADS)


def _attention(body, name, pre_specs, pre_args, q, k, v, geo, need_ctx):
    nb, t, n_ctx, tq = geo["batch"], geo["seq"], geo["ctx"], geo["tq"]
    s = t + n_ctx
    kw, vw = k.shape[2], v.shape[2]
    keys = lambda n: pl.BlockSpec((None, n, kw), lambda b, i: (b, 0 if n == s else t // n_ctx, 0),
                                  pipeline_mode=pl.Buffered(1))
    vals = lambda n: pl.BlockSpec((None, n, vw), lambda b, i: (b, 0 if n == s else t // n_ctx, 0),
                                  pipeline_mode=pl.Buffered(1))

    def call(n_keys, q_tiles, q_off, prev):
        qmap = lambda b, i: (b, q_off + i, 0)
        alias = {} if prev is None else {len(pre_args) + 3: 0}
        extra = [] if prev is None else [pl.BlockSpec(memory_space=pl.ANY)]
        return pl.pallas_call(
            body, grid=(nb, q_tiles),
            in_specs=pre_specs + [pl.BlockSpec((None, tq, BRANCH_W), qmap), keys(n_keys), vals(n_keys)] + extra,
            out_specs=pl.BlockSpec((None, tq, BRANCH_W), qmap),
            out_shape=jax.ShapeDtypeStruct(q.shape, BF16),
            scratch_shapes=[pltpu.VMEM((2 * tq, n_keys), F32), pltpu.VMEM((2, 2 * tq, n_keys), BF16),
                            pltpu.VMEM((1, kw), F32)],
            input_output_aliases=alias,
            compiler_params=_cparams(("parallel", "arbitrary")),
            name=name,
        )(*pre_args, q, k, v, *([] if prev is None else [prev]))

    out = call(s, t // tq, 0, None)
    if need_ctx:
        out = call(n_ctx, n_ctx // tq, t // tq, out)
    return out


def _gqa(cq, ck, cv, bd64, geo, need_ctx):
    return _attention(_gqa_kernel, "gqa", [_resident((256, 256))], [bd64], cq, ck, cv, geo, need_ctx)


def _dif(dq, dk, dv, lam_lane, subln, bd64, bd32, geo, need_ctx, out_scale, layer):
    pre = [_resident((1, BRANCH_W), layer), _resident((1, BRANCH_W), layer), _resident((256, 256)),
           _resident((256, 256))]
    return _attention(functools.partial(_dif_kernel, out_scale=out_scale), "diffattn", pre,
                      [lam_lane, subln, bd64, bd32], dq, dk, dv, geo, need_ctx)


def _merge_kernel(x_ref, mod_ref, n1_ref, wg_ref, wb_ref, wo_ref, bd64_ref, gnw_ref,
                  rof_ref, rob_ref, rg_ref, gof_ref, gob_ref, gr_ref, yc_ref, yd_ref, o_ref):
    nb, rt, d = x_ref.shape
    m = nb * rt
    x = x_ref[...]
    hb = _adaln(x, n1_ref[...], mod_ref[:, 0:1, :], mod_ref[:, 1:2, :]).astype(BF16).reshape(m, d)
    flat = lambda ref: ref[...].reshape(m, BRANCH_W)
    bd64 = bd64_ref[...]
    ya = _group_rms(flat(rof_ref) + flat(rob_ref), bd64, None, HEAD_W) * _silu(flat(rg_ref))
    yb = _group_rms(flat(gof_ref) + flat(gob_ref), bd64, gnw_ref[...], HEAD_W) * _silu(flat(gr_ref))
    ys = (ya.astype(BF16), yb.astype(BF16), flat(yc_ref), flat(yd_ref))
    mix = None
    for i in range(4):
        gate = _sigmoid(jnp.dot(hb, wg_ref[:, i * d:(i + 1) * d], preferred_element_type=F32))
        term = gate * jnp.dot(ys[i], wb_ref[i], preferred_element_type=F32)
        mix = term if mix is None else mix + term
    o_ref[...] = x + mod_ref[:, 2:3, :] * _dot(mix, wo_ref[...]).reshape(nb, rt, d)


def _merge(xs, mod, norm1, w_gate, w_branch, w_out, bd64, gla_norm, rof, rob, zr, gof, gob, zg, yc, yd, geo,
           need_ctx, layer):
    nb, s, d = xs.shape
    rt = geo["rt_merge"]
    tiles = (s if need_ctx else geo["seq"]) // rt
    row = lambda i: (0, i, 0)
    col3 = lambda i: (0, i, 3)
    bw = BRANCH_W
    act = lambda imap: pl.BlockSpec((nb, rt, bw), imap)
    return pl.pallas_call(
        _merge_kernel,
        grid=(tiles,),
        in_specs=[pl.BlockSpec((nb, rt, d), row),
                  _mod_spec(nb, d, geo["seq"] // rt, layer),
                  _resident((1, d), layer), _resident((d, 4 * d), layer), _resident((4, bw, d), layer),
                  _resident((d, d), layer), _resident((256, 256)), _resident((1, bw), layer),
                  act(row), act(row), act(col3), act(row), act(row), act(col3), act(row), act(row)],
        out_specs=pl.BlockSpec((nb, rt, d), row),
        out_shape=jax.ShapeDtypeStruct((nb, s, d), F32),
        input_output_aliases={0: 0},
        compiler_params=_cparams(("parallel",)),
        name="merge",
    )(xs, mod, norm1, w_gate, w_branch, w_out, bd64, gla_norm, rof, rob, zr, gof, gob, zg, yc, yd)


def _ffn_kernel(x_ref, xp_ref, xn_ref, mod_ref, n2_ref, wa_ref, wg_ref, cw_ref, wd_ref, o_ref, h_ref, acc_ref,
                *, seq, total, n_f):
    nb, rt, d = x_ref.shape
    ext = rt + 2 * HALO
    i = pl.program_id(0)
    j = pl.program_id(1)

    @pl.when(j == 0)
    def _():
        def hn(xx):
            return _adaln(xx, n2_ref[...], mod_ref[:, 3:4, :], mod_ref[:, 4:5, :]).astype(BF16)
        h_ref[:, 0:HALO, :] = hn(xp_ref[...])
        h_ref[:, HALO:HALO + rt, :] = hn(x_ref[...])
        h_ref[:, HALO + rt:ext, :] = hn(xn_ref[...])
        acc_ref[...] = jnp.zeros_like(acc_ref)

    tf = wa_ref.shape[1]
    u = jnp.dot(h_ref[...].reshape(nb * ext, d), wa_ref[...], preferred_element_type=F32)
    centre = lambda a: a.reshape(nb, ext, tf)[:, HALO:HALO + rt, :]
    up = centre(pltpu.roll(u, 1, 0))
    mid = centre(u)
    dn = centre(pltpu.roll(u, nb * ext - 1, 0))
    pos = i * rt + lax.broadcasted_iota(jnp.int32, (rt, 1), 0)
    keep_up = jnp.where(pos == 0, 0.0, 1.0) * jnp.where(pos == seq, 0.0, 1.0)
    keep_dn = jnp.where(pos == seq - 1, 0.0, 1.0) * jnp.where(pos == total - 1, 0.0, 1.0)
    a = ((up * keep_up) * cw_ref[0:1, :] + mid * cw_ref[1:2, :] + (dn * keep_dn) * cw_ref[2:3, :]
         + cw_ref[3:4, :])
    gate = jnp.dot(h_ref[:, HALO:HALO + rt, :].reshape(nb * rt, d), wg_ref[...], preferred_element_type=F32)
    acc_ref[...] += _dot((_silu(a) * gate.reshape(nb, rt, tf)).reshape(nb * rt, tf), wd_ref[...])

    @pl.when(j == n_f - 1)
    def _():
        o_ref[...] = x_ref[...] + mod_ref[:, 5:6, :] * acc_ref[...].reshape(nb, rt, d)


def _ffn(xs, mod, norm2, w_up, conv_pack, w_down, geo, need_ctx, layer):
    nb, s, d = xs.shape
    rt, tf, t = geo["rt_ffn"], geo["tf"], geo["seq"]
    s_out = s if need_ctx else t
    n_f = D_FF_W // tf
    hb = rt // HALO
    last_blk = s // HALO - 1
    row = lambda i, j: (0, i, 0)
    return pl.pallas_call(
        functools.partial(_ffn_kernel, seq=t, total=s, n_f=n_f),
        grid=(s_out // rt, n_f),
        in_specs=[pl.BlockSpec((nb, rt, d), row),
                  pl.BlockSpec((nb, HALO, d), lambda i, j: (0, jnp.maximum(i * hb - 1, 0), 0)),
                  pl.BlockSpec((nb, HALO, d), lambda i, j: (0, jnp.minimum((i + 1) * hb, last_blk), 0)),
                  _mod_spec(nb, d, t // rt, layer),
                  _resident((1, d), layer),
                  pl.BlockSpec((None, d, tf), lambda i, j: (layer, 0, j)),
                  pl.BlockSpec((None, d, tf), lambda i, j: (layer, 0, n_f + j)),
                  pl.BlockSpec((None, 8, tf), lambda i, j: (layer, 0, j)),
                  pl.BlockSpec((None, tf, d), lambda i, j: (layer, j, 0))],
        out_specs=pl.BlockSpec((nb, rt, d), row),
        out_shape=jax.ShapeDtypeStruct((nb, s_out, d), F32),
        scratch_shapes=[pltpu.VMEM((nb, rt + 2 * HALO, d), BF16), pltpu.VMEM((nb * rt, d), F32)],
        compiler_params=_cparams(("parallel", "arbitrary")),
        name="convffn",
    )(xs, xs, xs, mod, norm2, w_up, w_up, conv_pack, w_down)


def _rope_tables(t, n_ctx):
    rows = t // GRID_WIDTH

    def axial(dim):
        axis_dim = dim // 2
        freqs = ROPE_BASE ** (-np.arange(0, axis_dim, 2, dtype=np.float64) / axis_dim)
        row = np.repeat(np.arange(rows, dtype=np.float64), GRID_WIDTH)
        col = np.tile(np.arange(GRID_WIDTH, dtype=np.float64), rows)
        ang = np.concatenate([row[:, None] * freqs, col[:, None] * freqs], axis=-1)
        return np.cos(ang), np.sin(ang)

    freqs = ROPE_BASE ** (-np.linspace(0.0, 1.0, HEAD_W // 2))
    ang = np.arange(t, dtype=np.float64)[:, None] * freqs
    parts = []
    for cos, sin in ((np.cos(ang), np.sin(ang)), axial(HEAD_W), axial(DIF_QK)):
        reps = LANES // (2 * cos.shape[1])
        parts.append(np.tile(cos, (1, 2 * reps)))
        parts.append(np.tile(np.concatenate([-sin, sin], axis=1), (1, reps)))
    table = np.concatenate(parts, axis=1)
    ident = np.tile(np.concatenate([np.ones((1, LANES)), np.zeros((1, LANES))], axis=1), (n_ctx, 3))
    return jnp.asarray(np.concatenate([table, ident], axis=0), dtype=F32)


def _block_diag_ones(width, group):
    idx = np.arange(width) // group
    return jnp.asarray(idx[:, None] == idx[None, :], dtype=BF16)


def _row_block_lane_mask(block, n_lanes, lane_group):
    n_groups = n_lanes // lane_group
    rows = np.arange(n_groups * block)[:, None] // block
    return jnp.asarray(rows == np.arange(n_lanes)[None, :] // lane_group, dtype=F32)


def kernel(x, c, ctx, c_ctx, w_mod, b_mod, norm1, norm2, w_in, ret_decay, gla_gate_up, gla_gate_b, gla_norm,
           gqa_qnorm, gqa_knorm, dif_qnorm, dif_knorm, dif_lambda, dif_subln, w_branch, w_out, w_up, conv_w,
           conv_b, w_down):
    nb, t, d = x.shape
    n_ctx = ctx.shape[1]
    depth = w_mod.shape[0]
    assert t % 256 == 0 and n_ctx % 256 == 0 and nb + 1 <= 8
    geo = dict(batch=nb, seq=t, ctx=n_ctx, rt=256, rt_merge=128, rt_ffn=128, tf=D_FF_W // 2, tq=256)

    xs = jnp.concatenate([x, ctx], axis=1)
    cc = jnp.zeros((8, d), F32).at[0:nb].set(c).at[nb].set(c_ctx)
    mod_all = _modulation(cc, w_mod, b_mod).reshape(depth, 8, 6, d)

    table = _rope_tables(t, n_ctx)
    bd64 = _block_diag_ones(256, HEAD_W)
    bd32 = _block_diag_ones(256, DIF_QK)
    ti = np.arange(nb * GLA_CHUNK_LEN)
    same_chunk = (ti[:, None] // GLA_CHUNK_LEN) == (ti[None, :] // GLA_CHUNK_LEN)
    tri = jnp.asarray(np.stack([same_chunk & (ti[:, None] >= ti[None, :]),
                                same_chunk & (ti[:, None] <= ti[None, :])]), dtype=BF16)
    hm16 = _row_block_lane_mask(GLA_SUB, BRANCH_W, HEAD_W)
    hm64 = _row_block_lane_mask(GLA_CHUNK_LEN, BRANCH_W, HEAD_W)
    hm128 = _row_block_lane_mask(RET_CHUNK_LEN, BRANCH_W, HEAD_W)
    qi = np.arange(N_HEADS * GLA_CHUNK_LEN)[:, None] % GLA_CHUNK_LEN
    kj = np.arange(GLA_CHUNK_LEN)[None, :]
    causal = jnp.asarray(np.stack([kj <= qi, kj >= qi]), dtype=F32)

    n_mix = 2080
    n_gate0 = 3360
    w_proj = jnp.concatenate([w_in[:, :, :n_mix], jnp.zeros((depth, d, LANES - 2 * GLA_RANK_W), F32),
                              w_in[:, :, n_mix:n_gate0]], axis=2).astype(BF16)
    w_gate = w_in[:, :, n_gate0:].astype(BF16)
    w_branch_b, w_out_b, w_up_b, w_down_b = (a.astype(BF16) for a in (w_branch, w_out, w_up, w_down))
    qk_norms = jnp.stack([jnp.tile(gqa_qnorm, (1, 4)), jnp.tile(gqa_knorm, (1, 4)), jnp.tile(dif_qnorm, (1, 8)),
                          jnp.tile(dif_knorm, (1, 8))] + [jnp.zeros((depth, 256), F32)] * 4, axis=1)
    mod = jnp.stack([mod_all[:, 0:nb], jnp.broadcast_to(mod_all[:, nb:nb + 1], (depth, nb, 6, d))], axis=1)
    n1 = norm1[:, None, :]
    n2 = norm2[:, None, :]
    log_gamma = jax.nn.log_sigmoid(ret_decay.astype(F32))
    lg_rows = jnp.broadcast_to(jnp.repeat(log_gamma, RET_CHUNK_LEN, axis=2)[..., None],
                               (depth, 2, N_HEADS * RET_CHUNK_LEN, RET_CHUNK_LEN))
    lg_lane = jnp.repeat(log_gamma, HEAD_W, axis=2)
    pad_rows = lambda a, lo: jnp.pad(a, ((0, 0), (lo, LANES - GLA_RANK_W - lo), (0, 0)))
    gate_up = jnp.stack([pad_rows(gla_gate_up[:, 0], 0), pad_rows(gla_gate_up[:, 1], GLA_RANK_W)],
                        axis=1).astype(BF16)
    lam_init = [0.8 - 0.6 * math.exp(-0.3 * l) for l in range(depth)]
    lv = dif_lambda.astype(F32)
    lam = (jnp.exp(jnp.sum(lv[:, 0] * lv[:, 1], axis=-1)) - jnp.exp(jnp.sum(lv[:, 2] * lv[:, 3], axis=-1))
           + jnp.asarray(lam_init, F32))
    lam_lane = jnp.broadcast_to(lam[:, None, None], (depth, 1, BRANCH_W))
    subln = jnp.tile(dif_subln, (1, 4))[:, None, :]
    gnw = jnp.tile(gla_norm, (1, 4))[:, None, :]
    conv_pack = jnp.concatenate([conv_w, conv_b[:, None, :], jnp.zeros((depth, 4, D_FF_W), F32)], axis=1)

    for l in range(depth):
        need_ctx = l < depth - 1
        zr, zg, cq, ck, cv, dq, dk, dv = _inproj(xs, mod, n1, w_proj, table, bd64, bd32, qk_norms, geo, l)
        rof, rob = _retention(zr, lg_rows, lg_lane, hm128, bd64, geo, l)
        gof, gob = _gla(zg, gate_up, gla_gate_b, tri, bd64, hm16, hm64, causal, geo, l)
        yc = _gqa(cq, ck, cv, bd64, geo, need_ctx)
        yd = _dif(dq, dk, dv, lam_lane, subln, bd64, bd32, geo, need_ctx, 1.0 - lam_init[l], l)
        xs = _merge(xs, mod, n1, w_gate, w_branch_b, w_out_b, bd64, gnw, rof, rob, zr, gof, gob, zg, yc, yd,
                    geo, need_ctx, l)
        xs = _ffn(xs, mod, n2, w_up_b, conv_pack, w_down_b, geo, need_ctx, l)

    return xs
```

```python
import functools
import math

import jax
import jax.numpy as jnp
import numpy as np
from jax import lax
from jax.experimental import pallas as pl
from jax.experimental.pallas import tpu as pltpu

F32 = jnp.float32
BF16 = jnp.bfloat16

N_HEADS = 4
HEAD_W = 64
BRANCH_W = N_HEADS * HEAD_W
DIF_QK = 32
RET_CHUNK_LEN = 128
GLA_CHUNK_LEN = 64
GLA_SUB = 16
GLA_RANK_W = 16
GLA_TAU_INV = 1.0 / 16.0
GRID_WIDTH = 64
ROPE_BASE = 10000.0
EPS = 1e-6
D_FF_W = 2816
LOG2E = 1.4426950408889634

LANES = 128
VMEM_LIMIT_BYTES = 56 * 1024 * 1024

RET_OFF, GLA_OFF, GQA_OFF, DIF_OFF, PROJ_W = 0, 1024, 2176, 2688, 3456
GLA_GROUP_W = GQA_OFF - GLA_OFF
HALO = 16
SOFTMAX_ROWS = 128


def _cparams(sem):
    return pltpu.CompilerParams(dimension_semantics=sem, vmem_limit_bytes=VMEM_LIMIT_BYTES)


def _resident(shape, layer=None):
    nd = len(shape)
    if layer is None:
        return pl.BlockSpec(shape, lambda *_: (0,) * nd, pipeline_mode=pl.Buffered(1))
    return pl.BlockSpec((None,) + tuple(shape), lambda *_: (layer,) + (0,) * nd, pipeline_mode=pl.Buffered(1))


def _sample_groups(nb):
    half = nb // 2
    return [(0, nb)] if nb % 2 else [(0, half), (half, nb)]


def _dot(a, b):
    return jnp.dot(a.astype(BF16), b.astype(BF16), preferred_element_type=F32)


def _dot_nt(a, b):
    return lax.dot_general(a.astype(BF16), b.astype(BF16), (((1,), (1,)), ((), ())),
                           preferred_element_type=F32)


def _dot_tn(a, b):
    return lax.dot_general(a.astype(BF16), b.astype(BF16), (((0,), (0,)), ((), ())),
                           preferred_element_type=F32)


def _sigmoid(x):
    return 0.5 * jnp.tanh(0.5 * x) + 0.5


def _silu(x):
    return x * _sigmoid(x)


def _adaln(x, norm_w, shift, scale):
    h = x * lax.rsqrt(jnp.mean(x * x, axis=-1, keepdims=True) + EPS) * norm_w
    return h * (1.0 + scale) + shift


def _group_rms(x, bd, weight, group):
    ss = jnp.dot((x * x).astype(BF16), bd, preferred_element_type=F32)
    y = x * lax.rsqrt(ss * (1.0 / group) + EPS)
    return y if weight is None else y * weight


def _rope(x, cos, sin_signed, half):
    w = x.shape[1]
    reps = w // LANES
    if reps > 1:
        cos = jnp.concatenate([cos] * reps, axis=1)
        sin_signed = jnp.concatenate([sin_signed] * reps, axis=1)
    lane = lax.broadcasted_iota(jnp.int32, x.shape, 1)
    lower = (lane & (2 * half - 1)) < half
    partner = jnp.where(lower, pltpu.roll(x, w - half, 1), pltpu.roll(x, half, 1))
    return x * cos + partner * sin_signed


def _mod_kernel(c_ref, w_ref, b_ref, o_ref):
    o_ref[...] = _dot(_silu(c_ref[...]), w_ref[...]) + b_ref[...]


def _modulation(cc, w_mod, b_mod):
    n_layers, d, width = w_mod.shape
    tn = 1024
    return pl.pallas_call(
        _mod_kernel,
        grid=(n_layers, width // tn),
        in_specs=[pl.BlockSpec((8, d), lambda l, j: (0, 0)),
                  pl.BlockSpec((None, d, tn), lambda l, j: (l, 0, j)),
                  pl.BlockSpec((None, 1, tn), lambda l, j: (l, 0, j))],
        out_specs=pl.BlockSpec((None, 8, tn), lambda l, j: (l, 0, j)),
        out_shape=jax.ShapeDtypeStruct((n_layers, 8, width), F32),
        compiler_params=_cparams(("parallel", "parallel")),
        name="modulation",
    )(cc, w_mod, b_mod.reshape(n_layers, 1, width))


def _mod_spec(nb, d, n_lat_tiles, layer):
    return pl.BlockSpec((None, None, nb, 6, d),
                        lambda i, *_: (layer, jnp.where(i < n_lat_tiles, 0, 1), 0, 0, 0))


def _inproj_kernel(x_ref, mod_ref, n1_ref, w_ref, tab_ref, bd64_ref, bd32_ref, nw_ref,
                   zr_ref, zg_ref, cq_ref, ck_ref, cv_ref, dq_ref, dk_ref, dv_ref):
    nb, rt, d = x_ref.shape
    m = nb * rt
    hb = _adaln(x_ref[...], n1_ref[...], mod_ref[:, 0:1, :], mod_ref[:, 1:2, :]).astype(BF16).reshape(m, d)

    def table(lo):
        t = tab_ref[:, lo:lo + LANES]
        return jnp.concatenate([t] * nb, axis=0)

    r_cos, r_sin, g_cos, g_sin, d_cos, d_sin = (table(i * LANES) for i in range(6))

    def put(ref, lo, val):
        ref[:, :, lo:lo + val.shape[1]] = val.reshape(nb, rt, val.shape[1]).astype(ref.dtype)

    zr = jnp.dot(hb, w_ref[:, RET_OFF:GLA_OFF], preferred_element_type=F32)
    put(zr_ref, 0, _rope(zr[:, 0:256], r_cos, r_sin, 32))
    put(zr_ref, 256, _rope(zr[:, 256:512], r_cos, r_sin, 32) * (HEAD_W ** -0.5))
    put(zr_ref, 512, zr[:, 512:1024])

    zg = jnp.dot(hb, w_ref[:, GLA_OFF:GQA_OFF], preferred_element_type=F32)
    put(zg_ref, 0, zg[:, 0:256] * (HEAD_W ** -0.5))
    put(zg_ref, 256, zg[:, 256:GLA_GROUP_W])

    zc = jnp.dot(hb, w_ref[:, GQA_OFF:DIF_OFF], preferred_element_type=F32)
    bd64 = bd64_ref[...]
    q = _group_rms(zc[:, 0:256], bd64, nw_ref[0:1, :], HEAD_W)
    put(cq_ref, 0, _rope(q, g_cos, g_sin, 32) * (HEAD_W ** -0.5 * LOG2E))
    k = _group_rms(zc[:, 256:384], bd64[0:128, 0:128], nw_ref[1:2, 0:128], HEAD_W)
    put(ck_ref, 0, _rope(k, g_cos, g_sin, 32))
    put(cv_ref, 0, zc[:, 384:512])

    zd = jnp.dot(hb, w_ref[:, DIF_OFF:PROJ_W], preferred_element_type=F32)
    bd32 = bd32_ref[...]
    q = _group_rms(zd[:, 0:256], bd32, nw_ref[2:3, :], DIF_QK)
    put(dq_ref, 0, _rope(q, d_cos, d_sin, 16) * (DIF_QK ** -0.5 * LOG2E))
    k = _group_rms(zd[:, 256:512], bd32, nw_ref[3:4, :], DIF_QK)
    put(dk_ref, 0, _rope(k, d_cos, d_sin, 16))
    put(dv_ref, 0, zd[:, 512:768])


def _inproj(xs, mod, norm1, w_proj, table, bd64, bd32, qk_norms, geo, layer):
    nb, s, d = xs.shape
    rt = geo["rt"]
    row = lambda i: (0, i, 0)
    widths = (1024, GLA_GROUP_W, 256, 128, 128, 256, 256, 256)
    dtypes = (F32, F32, BF16, BF16, BF16, BF16, BF16, BF16)
    return pl.pallas_call(
        _inproj_kernel,
        grid=(s // rt,),
        in_specs=[pl.BlockSpec((nb, rt, d), row),
                  _mod_spec(nb, d, geo["seq"] // rt, layer),
                  _resident((1, d), layer),
                  _resident((d, PROJ_W), layer),
                  pl.BlockSpec((rt, 768), lambda i: (i, 0)),
                  _resident((256, 256)),
                  _resident((256, 256)),
                  _resident((8, 256), layer)],
        out_specs=[pl.BlockSpec((nb, rt, w), row) for w in widths],
        out_shape=[jax.ShapeDtypeStruct((nb, s, w), t) for w, t in zip(widths, dtypes)],
        compiler_params=_cparams(("parallel",)),
        name="inproj",
    )(xs, mod, norm1, w_proj, table, bd64, bd32, qk_norms)


def _scan_index_maps(geo, chunk):
    nl, nc = geo["seq"] // chunk, geo["ctx"] // chunk
    fwd = lambda s: (0, jnp.where(s < nc, nl + s, s - nc), 0)
    bwd = lambda s: (0, jnp.where(s < nc, nl + nc - 1 - s, nl - 1 - (s - nc)), 0)
    return fwd, bwd, nl + nc


def _ret_kernel(lgr_ref, lgl_ref, hm_ref, bd64_ref, zf_ref, zb_ref, of_ref, ob_ref, st_ref):
    nb = zf_ref.shape[0]
    c = RET_CHUNK_LEN

    @pl.when(pl.program_id(0) == 0)
    def _():
        st_ref[...] = jnp.zeros_like(st_ref)

    hm = hm_ref[...]
    bd64_f = bd64_ref[...].astype(F32)
    ii = lax.broadcasted_iota(jnp.int32, (N_HEADS * c, c), 0) & (c - 1)
    jj = lax.broadcasted_iota(jnp.int32, (N_HEADS * c, c), 1)
    ri = lax.broadcasted_iota(jnp.int32, (c, BRANCH_W), 0).astype(F32)
    for rev, z_ref, o_ref in ((False, zf_ref, of_ref), (True, zb_ref, ob_ref)):
        d = int(rev)
        dist = ((jj - ii) if rev else (ii - jj)).astype(F32)
        dec = jnp.where(dist >= 0, jnp.exp(lgr_ref[d] * jnp.maximum(dist, 0.0)), 0.0)
        lgl = lgl_ref[d:d + 1, :]
        q_dec = jnp.exp(lgl * ((c - ri) if rev else (ri + 1.0)))
        k_dec = jnp.exp(lgl * (ri if rev else (c - 1.0 - ri)))
        c_dec = jnp.exp(lgl * float(c))
        for b in range(nb):
            q = z_ref[b, :, 0:256]
            k = z_ref[b, :, 256:512]
            v = z_ref[b, :, 512:768]
            q4 = jnp.concatenate([q] * N_HEADS, axis=0) * hm
            o4 = _dot(_dot_nt(q4, k) * dec, v) * hm
            st = st_ref[d, b]
            o_ref[b] = o4[0:c] + o4[c:2 * c] + o4[2 * c:3 * c] + o4[3 * c:4 * c] + _dot(q, st) * q_dec
            st_ref[d, b] = st * c_dec + _dot_tn(k * k_dec, v) * bd64_f


def _retention(zr, lg_rows, lg_lane, head_mask, bd64, geo, layer):
    nb, s, _ = zr.shape
    c = RET_CHUNK_LEN
    fwd, bwd, steps = _scan_index_maps(geo, c)
    return pl.pallas_call(
        _ret_kernel,
        grid=(steps,),
        in_specs=[_resident((2, N_HEADS * c, c), layer), _resident((2, BRANCH_W), layer),
                  _resident((N_HEADS * c, BRANCH_W)),
                  _resident((256, 256)),
                  pl.BlockSpec((nb, c, 1024), fwd), pl.BlockSpec((nb, c, 1024), bwd)],
        out_specs=[pl.BlockSpec((nb, c, BRANCH_W), fwd), pl.BlockSpec((nb, c, BRANCH_W), bwd)],
        out_shape=[jax.ShapeDtypeStruct((nb, s, BRANCH_W), F32)] * 2,
        scratch_shapes=[pltpu.VMEM((2, nb, BRANCH_W, BRANCH_W), F32)],
        compiler_params=_cparams(("arbitrary",)),
        name="retention",
    )(lg_rows, lg_lane, head_mask, bd64, zr, zr)


GLA_FACTOR_MIN_LOG_DECAY = -40.0
GLA_FACTOR_MAX_KEY = 1e15


def _gla_prepare(z_ref, d, gu_ref, gb_ref, tri_ref):
    nb, c, _ = z_ref.shape
    m = nb * c
    q = z_ref[:, :, 0:256].reshape(m, BRANCH_W)
    k = z_ref[:, :, 256:512].reshape(m, BRANCH_W)
    v = z_ref[:, :, 512:768].reshape(m, BRANCH_W)
    logit = _dot(z_ref[:, :, 1024:GLA_GROUP_W].reshape(m, LANES), gu_ref[d]) + gb_ref[d:d + 1, :]
    g = (jnp.minimum(logit, 0.0) - jnp.log(1.0 + jnp.exp(-jnp.abs(logit)))) * GLA_TAU_INV
    g1 = g.astype(BF16)
    r1 = g - g1.astype(F32)
    g2 = r1.astype(BF16)
    g3 = (r1 - g2.astype(F32)).astype(BF16)
    tri = tri_ref[d]
    bcum = (jnp.dot(tri, g1, preferred_element_type=F32) + jnp.dot(tri, g2, preferred_element_type=F32)
            + jnp.dot(tri, g3, preferred_element_type=F32))
    return q, k, v, bcum


def _gla_state_step(st_ref, d, b, q_state_b, kb, vb, bb, rev, bd64_f):
    c = kb.shape[0]
    b_last = bb[0:1, :] if rev else bb[c - 1:c, :]
    st = st_ref[d, b]
    o_b = _dot_nt(q_state_b, st)
    st_ref[d, b] = st * jnp.exp(b_last) + _dot_tn(vb, kb * jnp.exp(b_last - bb)) * bd64_f
    return o_b


def _gla_factored(d, rev, q, k, v, bcum, o_ref, st_ref, hm64_ref, causal_ref, bd64_f):
    nb, c, _ = o_ref.shape
    q_state = q * jnp.exp(bcum)
    k_inv = k * jnp.exp(-bcum)
    hm = hm64_ref[...]
    causal = causal_ref[d]
    outs = []
    for b in range(nb):
        rows = slice(b * c, (b + 1) * c)
        qb, vb = q_state[rows], v[rows]
        q4 = jnp.concatenate([qb] * N_HEADS, axis=0) * hm
        att = jnp.where(causal > 0.0, _dot_nt(q4, k_inv[rows]), 0.0)
        o4 = _dot(att, vb) * hm
        o_b = o4[0:c] + o4[c:2 * c] + o4[2 * c:3 * c] + o4[3 * c:4 * c]
        outs.append(o_b + _gla_state_step(st_ref, d, b, qb, k[rows], vb, bcum[rows], rev, bd64_f))
    o_ref[...] = jnp.concatenate(outs, axis=0).reshape(nb, c, BRANCH_W)


def _gla_pairwise(d, rev, q, k, v, bcum, o_ref, st_ref, w_ref, hm16_ref, bd64, bd64_f):
    nb, c, _ = o_ref.shape
    sub = GLA_SUB
    nsub = c // sub
    m = nb * c
    head_mask = hm16_ref[...]
    rin = lax.broadcasted_iota(jnp.int32, (m, BRANCH_W), 0) & (sub - 1)
    for dl in range(sub):
        if dl == 0:
            w = q * k
        else:
            shift = (m - dl) if rev else dl
            ks = pltpu.roll(k, shift, 0)
            bs = pltpu.roll(bcum, shift, 0)
            valid = (rin + dl <= sub - 1) if rev else (rin >= dl)
            w = jnp.where(valid, q * ks * jnp.exp(jnp.minimum(bcum - bs, 0.0)), 0.0)
        w_ref[dl * m:(dl + 1) * m, :] = w.astype(BF16)
    att = jnp.dot(w_ref[...], bd64, preferred_element_type=F32)
    o_acc = att[0:m] * v
    for dl in range(1, sub):
        o_acc = o_acc + att[dl * m:(dl + 1) * m] * pltpu.roll(v, (m - dl) if rev else dl, 0)

    q_state = q * jnp.exp(bcum)
    outs = []
    for b in range(nb):
        r0 = b * c
        qb, kb, vb, bb = q[r0:r0 + c], k[r0:r0 + c], v[r0:r0 + c], bcum[r0:r0 + c]
        rows = []
        for blk in range(nsub):
            a0, a1 = blk * sub, (blk + 1) * sub
            if rev and blk < nsub - 1:
                anchor, lo, hi = bb[a1:a1 + 1, :], a1, c
            elif (not rev) and blk > 0:
                anchor, lo, hi = bb[a0 - 1:a0, :], 0, a0
            else:
                rows.append(jnp.zeros((sub, BRANCH_W), F32))
                continue
            qa = qb[a0:a1] * jnp.exp(bb[a0:a1] - anchor)
            ka = kb[lo:hi] * jnp.exp(anchor - bb[lo:hi])
            q4 = jnp.concatenate([qa] * N_HEADS, axis=0) * head_mask
            o4 = _dot(_dot_nt(q4, ka), vb[lo:hi]) * head_mask
            rows.append(o4[0:sub] + o4[sub:2 * sub] + o4[2 * sub:3 * sub] + o4[3 * sub:4 * sub])
        o_b = _gla_state_step(st_ref, d, b, q_state[r0:r0 + c], kb, vb, bb, rev, bd64_f)
        outs.append(o_b + jnp.concatenate(rows, axis=0))
    o_ref[...] = (o_acc + jnp.concatenate(outs, axis=0)).reshape(nb, c, BRANCH_W)


def _gla_kernel(gu_ref, gb_ref, tri_ref, bd64_ref, hm16_ref, hm64_ref, causal_ref, zf_ref, zb_ref,
                of_ref, ob_ref, st_ref, w_ref):
    @pl.when(pl.program_id(0) == 0)
    def _():
        st_ref[...] = jnp.zeros_like(st_ref)

    bd64 = bd64_ref[...]
    bd64_f = bd64.astype(F32)
    fwd = _gla_prepare(zf_ref, 0, gu_ref, gb_ref, tri_ref)
    bwd = _gla_prepare(zb_ref, 1, gu_ref, gb_ref, tri_ref)
    min_decay = jnp.minimum(jnp.min(fwd[3]), jnp.min(bwd[3]))
    max_key = jnp.maximum(jnp.max(jnp.abs(fwd[1])), jnp.max(jnp.abs(bwd[1])))
    factor_ok = jnp.logical_and(min_decay > GLA_FACTOR_MIN_LOG_DECAY, max_key < GLA_FACTOR_MAX_KEY)

    @pl.when(factor_ok)
    def _():
        _gla_factored(0, False, *fwd, of_ref, st_ref, hm64_ref, causal_ref, bd64_f)
        _gla_factored(1, True, *bwd, ob_ref, st_ref, hm64_ref, causal_ref, bd64_f)

    @pl.when(jnp.logical_not(factor_ok))
    def _():
        _gla_pairwise(0, False, *fwd, of_ref, st_ref, w_ref, hm16_ref, bd64, bd64_f)
        _gla_pairwise(1, True, *bwd, ob_ref, st_ref, w_ref, hm16_ref, bd64, bd64_f)


def _gla(zg, gate_up_pad, gate_b, tri, bd64, hm16, hm64, causal, geo, layer):
    nb, s, _ = zg.shape
    c = GLA_CHUNK_LEN
    fwd, bwd, steps = _scan_index_maps(geo, c)
    return pl.pallas_call(
        _gla_kernel,
        grid=(steps,),
        in_specs=[_resident((2, LANES, BRANCH_W), layer), _resident((2, BRANCH_W), layer),
                  _resident((2, nb * c, nb * c)), _resident((256, 256)), _resident((N_HEADS * GLA_SUB, BRANCH_W)),
                  _resident((N_HEADS * c, BRANCH_W)), _resident((2, N_HEADS * c, c)),
                  pl.BlockSpec((nb, c, GLA_GROUP_W), fwd), pl.BlockSpec((nb, c, GLA_GROUP_W), bwd)],
        out_specs=[pl.BlockSpec((nb, c, BRANCH_W), fwd), pl.BlockSpec((nb, c, BRANCH_W), bwd)],
        out_shape=[jax.ShapeDtypeStruct((nb, s, BRANCH_W), F32)] * 2,
        scratch_shapes=[pltpu.VMEM((2, nb, BRANCH_W, BRANCH_W), F32),
                        pltpu.VMEM((GLA_SUB * nb * c, BRANCH_W), BF16)],
        compiler_params=_cparams(("arbitrary",)),
        name="gla",
    )(gate_up_pad, gate_b, tri, bd64, hm16, hm64, causal, zg, zg)


SOFTMAX_MIN_SUM = 2.0 ** -80


def _softmax_pv(qk_pairs, v, s_ref, p_ref, slot):
    tq = qk_pairs[0][0].shape[0]
    n_keys = v.shape[0]
    rows = len(qk_pairs) * tq
    for i, (q, k) in enumerate(qk_pairs):
        s_ref[i * tq:(i + 1) * tq, 0:n_keys] = _dot_nt(q, k)
    sums = []
    for r0 in range(0, rows, SOFTMAX_ROWS):
        s = s_ref[r0:r0 + SOFTMAX_ROWS, 0:n_keys]
        p = jnp.exp2(s - jnp.max(s, axis=-1, keepdims=True))
        sums.append(jnp.sum(p, axis=-1, keepdims=True))
        p_ref[slot, r0:r0 + SOFTMAX_ROWS, 0:n_keys] = p.astype(BF16)
    l = jnp.concatenate(sums, axis=0)
    o = jnp.dot(p_ref[slot, 0:rows, 0:n_keys], v, preferred_element_type=F32) / l
    return [o[i * tq:(i + 1) * tq] for i in range(len(qk_pairs))]


def _softmax_pv_bounded(qk_pairs, v, bounds, p_ref, slot, shared_keys, mix=None):
    tq = qk_pairs[0][0].shape[0]
    n_keys = v.shape[0]
    rows = len(qk_pairs) * tq
    if shared_keys:
        q_all = jnp.concatenate([q for q, _ in qk_pairs], axis=0)
        p = jnp.exp2(_dot_nt(q_all, qk_pairs[0][1]) - jnp.concatenate(bounds, axis=0))
        l = jnp.sum(p, axis=-1, keepdims=True)
        p_ref[slot, 0:rows, 0:n_keys] = p.astype(BF16)
    else:
        sums = []
        for i, (q, k) in enumerate(qk_pairs):
            p = jnp.exp2(_dot_nt(q, k) - bounds[i])
            sums.append(jnp.sum(p, axis=-1, keepdims=True))
            p_ref[slot, i * tq:(i + 1) * tq, 0:n_keys] = p.astype(BF16)
        l = jnp.concatenate(sums, axis=0)
        if mix is not None:
            l0, l1 = sums
            weight = (mix * l0 / l1).astype(BF16)
            p_ref[slot, 0:tq, 0:n_keys] = (p_ref[slot, 0:tq, 0:n_keys]
                                           - weight * p_ref[slot, tq:2 * tq, 0:n_keys])
            o = jnp.dot(p_ref[slot, 0:tq, 0:n_keys], v, preferred_element_type=F32) / l0
            return [o], jnp.min(l)
    o = jnp.dot(p_ref[slot, 0:rows, 0:n_keys], v, preferred_element_type=F32) / l
    return [o[i * tq:(i + 1) * tq] for i in range(len(qk_pairs))], jnp.min(l)


def _score_bounds(q_ref, key_max, bd):
    a = jnp.abs(q_ref[...].astype(F32)) * key_max
    return jnp.dot(a.astype(BF16), bd, preferred_element_type=F32)


def _attention_body(units, finish, q_ref, k_ref, bound, o_ref, s_ref, p_ref, mixes=None):
    def run(bounded):
        outs, l_min = [], None
        for u, (v, pairs) in enumerate(units):
            qk = [(q_ref[:, qs], k_ref[:, ks]) for qs, ks, _ in pairs]
            if bounded:
                shared = all(ks == pairs[0][1] for _, ks, _ in pairs)
                o, lm = _softmax_pv_bounded(qk, v, [bound[:, c:c + 1] for _, _, c in pairs], p_ref, u % 2, shared,
                                            None if mixes is None else mixes[u])
                l_min = lm if l_min is None else jnp.minimum(l_min, lm)
            else:
                o = _softmax_pv(qk, v, s_ref, p_ref, u % 2)
            outs.extend(o)
        o_ref[...] = finish(outs, bounded and mixes is not None).astype(o_ref.dtype)
        return l_min

    l_min = run(True)
    pl.when(jnp.logical_not(l_min >= SOFTMAX_MIN_SUM))(lambda: run(False))


def _key_max(k_ref, kmax_ref):
    @pl.when(pl.program_id(1) == 0)
    def _():
        kmax_ref[...] = jnp.max(jnp.abs(k_ref[...].astype(F32)), axis=0, keepdims=True)
    return kmax_ref[...]


def _gqa_kernel(bd64_ref, q_ref, k_ref, v_ref, *rest):
    o_ref, s_ref, p_ref, kmax_ref = rest[-4:]
    km = _key_max(k_ref, kmax_ref)
    km = jnp.concatenate([km[:, 0:HEAD_W]] * 2 + [km[:, HEAD_W:2 * HEAD_W]] * 2, axis=1)
    bound = _score_bounds(q_ref, km, bd64_ref[...])
    units = []
    for kv in range(2):
        cs = slice(kv * HEAD_W, (kv + 1) * HEAD_W)
        pairs = [(slice(h * HEAD_W, (h + 1) * HEAD_W), cs, h * HEAD_W) for h in (2 * kv, 2 * kv + 1)]
        units.append((v_ref[:, cs], pairs))
    _attention_body(units, lambda outs, _: jnp.concatenate(outs, axis=1), q_ref, k_ref, bound, o_ref, s_ref, p_ref)


def _dif_kernel(lam_ref, sub_ref, bd64_ref, bd32_ref, q_ref, k_ref, v_ref, *rest, out_scale):
    o_ref, s_ref, p_ref, kmax_ref = rest[-4:]
    bound = _score_bounds(q_ref, _key_max(k_ref, kmax_ref), bd32_ref[...])
    units = []
    for h in range(N_HEADS):
        pairs = []
        for mp in range(2):
            qs = slice((2 * h + mp) * DIF_QK, (2 * h + mp + 1) * DIF_QK)
            pairs.append((qs, qs, (2 * h + mp) * DIF_QK))
        units.append((v_ref[:, h * HEAD_W:(h + 1) * HEAD_W], pairs))

    def finish(outs, merged):
        heads = outs if merged else [outs[2 * h] - lam_ref[:, h * HEAD_W:(h + 1) * HEAD_W] * outs[2 * h + 1]
                                     for h in range(N_HEADS)]
        return _group_rms(jnp.concatenate(heads, axis=1), bd64_ref[...], sub_ref[...], HEAD_W) * out_scale

    _attention_body(units, finish, q_ref, k_ref, bound, o_ref, s_ref, p_ref, mixes=[lam_ref[:, 0:1]] * N_HEADS)


def _attention(body, name, pre_specs, pre_args, q, k, v, geo, need_ctx):
    nb, t, n_ctx, tq = geo["batch"], geo["seq"], geo["ctx"], geo["tq"]
    s = t + n_ctx
    kw, vw = k.shape[2], v.shape[2]
    keys = lambda n: pl.BlockSpec((None, n, kw), lambda b, i: (b, 0 if n == s else t // n_ctx, 0),
                                  pipeline_mode=pl.Buffered(1))
    vals = lambda n: pl.BlockSpec((None, n, vw), lambda b, i: (b, 0 if n == s else t // n_ctx, 0),
                                  pipeline_mode=pl.Buffered(1))

    def call(n_keys, q_tiles, q_off, prev):
        qmap = lambda b, i: (b, q_off + i, 0)
        alias = {} if prev is None else {len(pre_args) + 3: 0}
        extra = [] if prev is None else [pl.BlockSpec(memory_space=pl.ANY)]
        return pl.pallas_call(
            body, grid=(nb, q_tiles),
            in_specs=pre_specs + [pl.BlockSpec((None, tq, BRANCH_W), qmap), keys(n_keys), vals(n_keys)] + extra,
            out_specs=pl.BlockSpec((None, tq, BRANCH_W), qmap),
            out_shape=jax.ShapeDtypeStruct(q.shape, BF16),
            scratch_shapes=[pltpu.VMEM((2 * tq, n_keys), F32), pltpu.VMEM((2, 2 * tq, n_keys), BF16),
                            pltpu.VMEM((1, kw), F32)],
            input_output_aliases=alias,
            compiler_params=_cparams(("parallel", "arbitrary")),
            name=name,
        )(*pre_args, q, k, v, *([] if prev is None else [prev]))

    out = call(s, t // tq, 0, None)
    if need_ctx:
        out = call(n_ctx, n_ctx // tq, t // tq, out)
    return out


def _gqa(cq, ck, cv, bd64, geo, need_ctx):
    return _attention(_gqa_kernel, "gqa", [_resident((256, 256))], [bd64], cq, ck, cv, geo, need_ctx)


def _dif(dq, dk, dv, lam_lane, subln, bd64, bd32, geo, need_ctx, out_scale, layer):
    pre = [_resident((1, BRANCH_W), layer), _resident((1, BRANCH_W), layer), _resident((256, 256)),
           _resident((256, 256))]
    return _attention(functools.partial(_dif_kernel, out_scale=out_scale), "diffattn", pre,
                      [lam_lane, subln, bd64, bd32], dq, dk, dv, geo, need_ctx)


def _merge_kernel(x_ref, mod_ref, n1_ref, wg_ref, wb_ref, wo_ref, bd64_ref, gnw_ref,
                  rof_ref, rob_ref, rg_ref, gof_ref, gob_ref, gr_ref, yc_ref, yd_ref, o_ref):
    nb, rt, d = x_ref.shape
    m = nb * rt
    bd64 = bd64_ref[...]
    for g0, g1 in _sample_groups(nb):
        ng = g1 - g0
        m = ng * rt
        x = x_ref[g0:g1]
        hb = _adaln(x, n1_ref[...], mod_ref[g0:g1, 0:1, :], mod_ref[g0:g1, 1:2, :]).astype(BF16).reshape(m, d)
        flat = lambda ref: ref[g0:g1].reshape(m, BRANCH_W)
        ya = _group_rms(flat(rof_ref) + flat(rob_ref), bd64, None, HEAD_W) * _silu(flat(rg_ref))
        yb = _group_rms(flat(gof_ref) + flat(gob_ref), bd64, gnw_ref[...], HEAD_W) * _silu(flat(gr_ref))
        ys = (ya.astype(BF16), yb.astype(BF16), flat(yc_ref), flat(yd_ref))
        mix = None
        for i in range(4):
            gate = _sigmoid(jnp.dot(hb, wg_ref[:, i * d:(i + 1) * d], preferred_element_type=F32))
            term = gate * jnp.dot(ys[i], wb_ref[i], preferred_element_type=F32)
            mix = term if mix is None else mix + term
        o_ref[g0:g1] = x + mod_ref[g0:g1, 2:3, :] * _dot(mix, wo_ref[...]).reshape(ng, rt, d)


def _merge(xs, mod, norm1, w_gate, w_branch, w_out, bd64, gla_norm, rof, rob, zr, gof, gob, zg, yc, yd, geo,
           need_ctx, layer):
    nb, s, d = xs.shape
    rt = geo["rt_merge"]
    tiles = (s if need_ctx else geo["seq"]) // rt
    row = lambda i: (0, i, 0)
    col3 = lambda i: (0, i, 3)
    bw = BRANCH_W
    act = lambda imap: pl.BlockSpec((nb, rt, bw), imap)
    return pl.pallas_call(
        _merge_kernel,
        grid=(tiles,),
        in_specs=[pl.BlockSpec((nb, rt, d), row),
                  _mod_spec(nb, d, geo["seq"] // rt, layer),
                  _resident((1, d), layer), _resident((d, 4 * d), layer), _resident((4, bw, d), layer),
                  _resident((d, d), layer), _resident((256, 256)), _resident((1, bw), layer),
                  act(row), act(row), act(col3), act(row), act(row), act(col3), act(row), act(row)],
        out_specs=pl.BlockSpec((nb, rt, d), row),
        out_shape=jax.ShapeDtypeStruct((nb, s, d), F32),
        input_output_aliases={0: 0},
        compiler_params=_cparams(("parallel",)),
        name="merge",
    )(xs, mod, norm1, w_gate, w_branch, w_out, bd64, gla_norm, rof, rob, zr, gof, gob, zg, yc, yd)


FFN_CHUNKS = ((0, 1024), (1024, 2048), (2048, D_FF_W))


def _ffn_kernel(x_ref, xp_ref, xn_ref, mod_ref, n2_ref, wu_ref, cw_ref, wd_ref, o_ref, *, seq, total):
    nb, rt, d = x_ref.shape
    ext = rt + 2 * HALO
    first = pl.program_id(0) * rt
    keep_prev = jnp.where(jnp.logical_or(first == 0, first == seq), 0.0, 1.0)
    keep_next = jnp.where(jnp.logical_or(first + rt == seq, first + rt == total), 0.0, 1.0)
    for g0, g1 in _sample_groups(nb):
        ng = g1 - g0

        def hn(ref):
            return _adaln(ref[g0:g1], n2_ref[...], mod_ref[g0:g1, 3:4, :], mod_ref[g0:g1, 4:5, :])
        h_mid = hn(x_ref).astype(BF16)
        h_ext = jnp.concatenate([(hn(xp_ref) * keep_prev).astype(BF16), h_mid,
                                 (hn(xn_ref) * keep_next).astype(BF16)], axis=1).reshape(ng * ext, d)
        h_mid = h_mid.reshape(ng * rt, d)
        acc = None
        for f0, f1 in FFN_CHUNKS:
            tf = f1 - f0
            u = jnp.dot(h_ext, wu_ref[:, f0:f1], preferred_element_type=F32)
            centre = lambda a: a.reshape(ng, ext, tf)[:, HALO:HALO + rt, :]
            up = centre(pltpu.roll(u, 1, 0))
            mid = centre(u)
            dn = centre(pltpu.roll(u, ng * ext - 1, 0))
            a = (up * cw_ref[0:1, f0:f1] + mid * cw_ref[1:2, f0:f1] + dn * cw_ref[2:3, f0:f1]
                 + cw_ref[3:4, f0:f1])
            gate = jnp.dot(h_mid, wu_ref[:, D_FF_W + f0:D_FF_W + f1], preferred_element_type=F32)
            part = _dot((_silu(a) * gate.reshape(ng, rt, tf)).reshape(ng * rt, tf), wd_ref[f0:f1, :])
            acc = part if acc is None else acc + part
        o_ref[g0:g1] = x_ref[g0:g1] + mod_ref[g0:g1, 5:6, :] * acc.reshape(ng, rt, d)


def _ffn(xs, mod, norm2, w_up, conv_pack, w_down, geo, need_ctx, layer):
    nb, s, d = xs.shape
    rt, t = geo["rt_ffn"], geo["seq"]
    s_out = s if need_ctx else t
    hb = rt // HALO
    last_blk = s // HALO - 1
    row = lambda i: (0, i, 0)
    return pl.pallas_call(
        functools.partial(_ffn_kernel, seq=t, total=s),
        grid=(s_out // rt,),
        in_specs=[pl.BlockSpec((nb, rt, d), row),
                  pl.BlockSpec((nb, HALO, d), lambda i: (0, jnp.maximum(i * hb - 1, 0), 0)),
                  pl.BlockSpec((nb, HALO, d), lambda i: (0, jnp.minimum((i + 1) * hb, last_blk), 0)),
                  _mod_spec(nb, d, t // rt, layer),
                  _resident((1, d), layer),
                  _resident((d, 2 * D_FF_W), layer),
                  _resident((8, D_FF_W), layer),
                  _resident((D_FF_W, d), layer)],
        out_specs=pl.BlockSpec((nb, rt, d), row),
        out_shape=jax.ShapeDtypeStruct((nb, s_out, d), F32),
        compiler_params=_cparams(("parallel",)),
        name="convffn",
    )(xs, xs, xs, mod, norm2, w_up, conv_pack, w_down)


def _rope_tables(t, n_ctx):
    rows = t // GRID_WIDTH

    def axial(dim):
        axis_dim = dim // 2
        freqs = ROPE_BASE ** (-np.arange(0, axis_dim, 2, dtype=np.float64) / axis_dim)
        row = np.repeat(np.arange(rows, dtype=np.float64), GRID_WIDTH)
        col = np.tile(np.arange(GRID_WIDTH, dtype=np.float64), rows)
        ang = np.concatenate([row[:, None] * freqs, col[:, None] * freqs], axis=-1)
        return np.cos(ang), np.sin(ang)

    freqs = ROPE_BASE ** (-np.linspace(0.0, 1.0, HEAD_W // 2))
    ang = np.arange(t, dtype=np.float64)[:, None] * freqs
    parts = []
    for cos, sin in ((np.cos(ang), np.sin(ang)), axial(HEAD_W), axial(DIF_QK)):
        reps = LANES // (2 * cos.shape[1])
        parts.append(np.tile(cos, (1, 2 * reps)))
        parts.append(np.tile(np.concatenate([-sin, sin], axis=1), (1, reps)))
    table = np.concatenate(parts, axis=1)
    ident = np.tile(np.concatenate([np.ones((1, LANES)), np.zeros((1, LANES))], axis=1), (n_ctx, 3))
    return jnp.asarray(np.concatenate([table, ident], axis=0), dtype=F32)


def _block_diag_ones(width, group):
    idx = np.arange(width) // group
    return jnp.asarray(idx[:, None] == idx[None, :], dtype=BF16)


def _row_block_lane_mask(block, n_lanes, lane_group):
    n_groups = n_lanes // lane_group
    rows = np.arange(n_groups * block)[:, None] // block
    return jnp.asarray(rows == np.arange(n_lanes)[None, :] // lane_group, dtype=F32)


def kernel(x, c, ctx, c_ctx, w_mod, b_mod, norm1, norm2, w_in, ret_decay, gla_gate_up, gla_gate_b, gla_norm,
           gqa_qnorm, gqa_knorm, dif_qnorm, dif_knorm, dif_lambda, dif_subln, w_branch, w_out, w_up, conv_w,
           conv_b, w_down):
    nb, t, d = x.shape
    n_ctx = ctx.shape[1]
    depth = w_mod.shape[0]
    assert t % 256 == 0 and n_ctx % 256 == 0 and nb + 1 <= 8
    geo = dict(batch=nb, seq=t, ctx=n_ctx, rt=256, rt_merge=128, rt_ffn=256, tq=256)

    xs = jnp.concatenate([x, ctx], axis=1)
    cc = jnp.zeros((8, d), F32).at[0:nb].set(c).at[nb].set(c_ctx)
    mod_all = _modulation(cc, w_mod, b_mod).reshape(depth, 8, 6, d)

    table = _rope_tables(t, n_ctx)
    bd64 = _block_diag_ones(256, HEAD_W)
    bd32 = _block_diag_ones(256, DIF_QK)
    ti = np.arange(nb * GLA_CHUNK_LEN)
    same_chunk = (ti[:, None] // GLA_CHUNK_LEN) == (ti[None, :] // GLA_CHUNK_LEN)
    tri = jnp.asarray(np.stack([same_chunk & (ti[:, None] >= ti[None, :]),
                                same_chunk & (ti[:, None] <= ti[None, :])]), dtype=BF16)
    hm16 = _row_block_lane_mask(GLA_SUB, BRANCH_W, HEAD_W)
    hm64 = _row_block_lane_mask(GLA_CHUNK_LEN, BRANCH_W, HEAD_W)
    hm128 = _row_block_lane_mask(RET_CHUNK_LEN, BRANCH_W, HEAD_W)
    qi = np.arange(N_HEADS * GLA_CHUNK_LEN)[:, None] % GLA_CHUNK_LEN
    kj = np.arange(GLA_CHUNK_LEN)[None, :]
    causal = jnp.asarray(np.stack([kj <= qi, kj >= qi]), dtype=F32)

    n_mix = 2080
    n_gate0 = 3360
    w_proj = jnp.concatenate([w_in[:, :, :n_mix], jnp.zeros((depth, d, LANES - 2 * GLA_RANK_W), F32),
                              w_in[:, :, n_mix:n_gate0]], axis=2).astype(BF16)
    w_gate = w_in[:, :, n_gate0:].astype(BF16)
    w_branch_b, w_out_b, w_up_b, w_down_b = (a.astype(BF16) for a in (w_branch, w_out, w_up, w_down))
    qk_norms = jnp.stack([jnp.tile(gqa_qnorm, (1, 4)), jnp.tile(gqa_knorm, (1, 4)), jnp.tile(dif_qnorm, (1, 8)),
                          jnp.tile(dif_knorm, (1, 8))] + [jnp.zeros((depth, 256), F32)] * 4, axis=1)
    mod = jnp.stack([mod_all[:, 0:nb], jnp.broadcast_to(mod_all[:, nb:nb + 1], (depth, nb, 6, d))], axis=1)
    n1 = norm1[:, None, :]
    n2 = norm2[:, None, :]
    log_gamma = jax.nn.log_sigmoid(ret_decay.astype(F32))
    lg_rows = jnp.broadcast_to(jnp.repeat(log_gamma, RET_CHUNK_LEN, axis=2)[..., None],
                               (depth, 2, N_HEADS * RET_CHUNK_LEN, RET_CHUNK_LEN))
    lg_lane = jnp.repeat(log_gamma, HEAD_W, axis=2)
    pad_rows = lambda a, lo: jnp.pad(a, ((0, 0), (lo, LANES - GLA_RANK_W - lo), (0, 0)))
    gate_up = jnp.stack([pad_rows(gla_gate_up[:, 0], 0), pad_rows(gla_gate_up[:, 1], GLA_RANK_W)],
                        axis=1).astype(BF16)
    lam_init = [0.8 - 0.6 * math.exp(-0.3 * l) for l in range(depth)]
    lv = dif_lambda.astype(F32)
    lam = (jnp.exp(jnp.sum(lv[:, 0] * lv[:, 1], axis=-1)) - jnp.exp(jnp.sum(lv[:, 2] * lv[:, 3], axis=-1))
           + jnp.asarray(lam_init, F32))
    lam_lane = jnp.broadcast_to(lam[:, None, None], (depth, 1, BRANCH_W))
    subln = jnp.tile(dif_subln, (1, 4))[:, None, :]
    gnw = jnp.tile(gla_norm, (1, 4))[:, None, :]
    conv_pack = jnp.concatenate([conv_w, conv_b[:, None, :], jnp.zeros((depth, 4, D_FF_W), F32)], axis=1)

    for l in range(depth):
        need_ctx = l < depth - 1
        zr, zg, cq, ck, cv, dq, dk, dv = _inproj(xs, mod, n1, w_proj, table, bd64, bd32, qk_norms, geo, l)
        rof, rob = _retention(zr, lg_rows, lg_lane, hm128, bd64, geo, l)
        gof, gob = _gla(zg, gate_up, gla_gate_b, tri, bd64, hm16, hm64, causal, geo, l)
        yc = _gqa(cq, ck, cv, bd64, geo, need_ctx)
        yd = _dif(dq, dk, dv, lam_lane, subln, bd64, bd32, geo, need_ctx, 1.0 - lam_init[l], l)
        xs = _merge(xs, mod, n1, w_gate, w_branch_b, w_out_b, bd64, gnw, rof, rob, zr, gof, gob, zg, yc, yd,
                    geo, need_ctx, l)
        xs = _ffn(xs, mod, n2, w_up_b, conv_pack, w_down_b, geo, need_ctx, l)

    return xs
```

```python
import functools
import math

import jax
import jax.numpy as jnp
import numpy as np
from jax import lax
from jax.experimental import pallas as pl
from jax.experimental.pallas import tpu as pltpu

F32 = jnp.float32
BF16 = jnp.bfloat16

N_HEADS = 4
HEAD_W = 64
BRANCH_W = N_HEADS * HEAD_W
DIF_QK = 32
RET_CHUNK_LEN = 128
GLA_CHUNK_LEN = 128
GLA_SUB = 16
GLA_RANK_W = 16
GLA_TAU_INV = 1.0 / 16.0
GRID_WIDTH = 64
ROPE_BASE = 10000.0
EPS = 1e-6
D_FF_W = 2816
LOG2E = 1.4426950408889634

LANES = 128
VMEM_LIMIT_BYTES = 56 * 1024 * 1024

RET_OFF, GLA_OFF, GQA_OFF, DIF_OFF, PROJ_W = 0, 1024, 2176, 2688, 3456
GLA_GROUP_W = GQA_OFF - GLA_OFF
HALO = 16
SOFTMAX_ROWS = 128


def _cparams(sem):
    return pltpu.CompilerParams(dimension_semantics=sem, vmem_limit_bytes=VMEM_LIMIT_BYTES)


def _resident(shape, layer=None):
    nd = len(shape)
    if layer is None:
        return pl.BlockSpec(shape, lambda *_: (0,) * nd, pipeline_mode=pl.Buffered(1))
    return pl.BlockSpec((None,) + tuple(shape), lambda *_: (layer,) + (0,) * nd, pipeline_mode=pl.Buffered(1))


def _sample_groups(nb):
    half = nb // 2
    return [(0, nb)] if nb % 2 else [(0, half), (half, nb)]


def _dot(a, b):
    return jnp.dot(a.astype(BF16), b.astype(BF16), preferred_element_type=F32)


def _dot_nt(a, b):
    return lax.dot_general(a.astype(BF16), b.astype(BF16), (((1,), (1,)), ((), ())),
                           preferred_element_type=F32)


def _dot_tn(a, b):
    return lax.dot_general(a.astype(BF16), b.astype(BF16), (((0,), (0,)), ((), ())),
                           preferred_element_type=F32)


def _sigmoid(x):
    return 0.5 * jnp.tanh(0.5 * x) + 0.5


def _silu(x):
    return x * _sigmoid(x)


def _adaln(x, norm_w, shift, scale):
    h = x * lax.rsqrt(jnp.mean(x * x, axis=-1, keepdims=True) + EPS) * norm_w
    return h * (1.0 + scale) + shift


def _group_rms(x, bd, weight, group):
    ss = jnp.dot((x * x).astype(BF16), bd, preferred_element_type=F32)
    y = x * lax.rsqrt(ss * (1.0 / group) + EPS)
    return y if weight is None else y * weight


def _rope(x, cos, sin_signed, half):
    w = x.shape[1]
    reps = w // LANES
    if reps > 1:
        cos = jnp.concatenate([cos] * reps, axis=1)
        sin_signed = jnp.concatenate([sin_signed] * reps, axis=1)
    lane = lax.broadcasted_iota(jnp.int32, x.shape, 1)
    lower = (lane & (2 * half - 1)) < half
    partner = jnp.where(lower, pltpu.roll(x, w - half, 1), pltpu.roll(x, half, 1))
    return x * cos + partner * sin_signed


def _mod_kernel(c_ref, w_ref, b_ref, o_ref):
    o_ref[...] = _dot(_silu(c_ref[...]), w_ref[...]) + b_ref[...]


def _modulation(cc, w_mod, b_mod):
    n_layers, d, width = w_mod.shape
    tn = 1024
    return pl.pallas_call(
        _mod_kernel,
        grid=(n_layers, width // tn),
        in_specs=[pl.BlockSpec((8, d), lambda l, j: (0, 0)),
                  pl.BlockSpec((None, d, tn), lambda l, j: (l, 0, j)),
                  pl.BlockSpec((None, 1, tn), lambda l, j: (l, 0, j))],
        out_specs=pl.BlockSpec((None, 8, tn), lambda l, j: (l, 0, j)),
        out_shape=jax.ShapeDtypeStruct((n_layers, 8, width), F32),
        compiler_params=_cparams(("parallel", "parallel")),
        name="modulation",
    )(cc, w_mod, b_mod.reshape(n_layers, 1, width))


def _mod_spec(nb, d, n_lat_tiles, layer):
    return pl.BlockSpec((None, None, nb, 6, d),
                        lambda i, *_: (layer, jnp.where(i < n_lat_tiles, 0, 1), 0, 0, 0))


def _inproj_kernel(x_ref, mod_ref, n1_ref, w_ref, tab_ref, bd64_ref, bd32_ref, nw_ref,
                   zr_ref, zg_ref, cq_ref, ck_ref, cv_ref, dq_ref, dk_ref, dv_ref):
    nb, rt, d = x_ref.shape
    m = nb * rt
    hb = _adaln(x_ref[...], n1_ref[...], mod_ref[:, 0:1, :], mod_ref[:, 1:2, :]).astype(BF16).reshape(m, d)

    def table(lo):
        t = tab_ref[:, lo:lo + LANES]
        return jnp.concatenate([t] * nb, axis=0)

    r_cos, r_sin, g_cos, g_sin, d_cos, d_sin = (table(i * LANES) for i in range(6))

    def put(ref, lo, val):
        ref[:, :, lo:lo + val.shape[1]] = val.reshape(nb, rt, val.shape[1]).astype(ref.dtype)

    zr = jnp.dot(hb, w_ref[:, RET_OFF:GLA_OFF], preferred_element_type=F32)
    put(zr_ref, 0, _rope(zr[:, 0:256], r_cos, r_sin, 32))
    put(zr_ref, 256, _rope(zr[:, 256:512], r_cos, r_sin, 32) * (HEAD_W ** -0.5))
    put(zr_ref, 512, zr[:, 512:1024])

    zg = jnp.dot(hb, w_ref[:, GLA_OFF:GQA_OFF], preferred_element_type=F32)
    put(zg_ref, 0, zg[:, 0:256] * (HEAD_W ** -0.5))
    put(zg_ref, 256, zg[:, 256:GLA_GROUP_W])

    zc = jnp.dot(hb, w_ref[:, GQA_OFF:DIF_OFF], preferred_element_type=F32)
    bd64 = bd64_ref[...]
    q = _group_rms(zc[:, 0:256], bd64, nw_ref[0:1, :], HEAD_W)
    put(cq_ref, 0, _rope(q, g_cos, g_sin, 32) * (HEAD_W ** -0.5 * LOG2E))
    k = _group_rms(zc[:, 256:384], bd64[0:128, 0:128], nw_ref[1:2, 0:128], HEAD_W)
    put(ck_ref, 0, _rope(k, g_cos, g_sin, 32))
    put(cv_ref, 0, zc[:, 384:512])

    zd = jnp.dot(hb, w_ref[:, DIF_OFF:PROJ_W], preferred_element_type=F32)
    bd32 = bd32_ref[...]
    q = _group_rms(zd[:, 0:256], bd32, nw_ref[2:3, :], DIF_QK)
    put(dq_ref, 0, _rope(q, d_cos, d_sin, 16) * (DIF_QK ** -0.5 * LOG2E))
    k = _group_rms(zd[:, 256:512], bd32, nw_ref[3:4, :], DIF_QK)
    put(dk_ref, 0, _rope(k, d_cos, d_sin, 16))
    put(dv_ref, 0, zd[:, 512:768])


def _inproj(xs, mod, norm1, w_proj, table, bd64, bd32, qk_norms, geo, layer):
    nb, s, d = xs.shape
    rt = geo["rt"]
    row = lambda i: (0, i, 0)
    widths = (1024, GLA_GROUP_W, 256, 128, 128, 256, 256, 256)
    dtypes = (F32, F32, BF16, BF16, BF16, BF16, BF16, BF16)
    return pl.pallas_call(
        _inproj_kernel,
        grid=(s // rt,),
        in_specs=[pl.BlockSpec((nb, rt, d), row),
                  _mod_spec(nb, d, geo["seq"] // rt, layer),
                  _resident((1, d), layer),
                  _resident((d, PROJ_W), layer),
                  pl.BlockSpec((rt, 768), lambda i: (i, 0)),
                  _resident((256, 256)),
                  _resident((256, 256)),
                  _resident((8, 256), layer)],
        out_specs=[pl.BlockSpec((nb, rt, w), row) for w in widths],
        out_shape=[jax.ShapeDtypeStruct((nb, s, w), t) for w, t in zip(widths, dtypes)],
        compiler_params=_cparams(("parallel",)),
        name="inproj",
    )(xs, mod, norm1, w_proj, table, bd64, bd32, qk_norms)


def _scan_index_maps(geo, chunk):
    nl, nc = geo["seq"] // chunk, geo["ctx"] // chunk
    fwd = lambda s: (0, jnp.where(s < nc, nl + s, s - nc), 0)
    bwd = lambda s: (0, jnp.where(s < nc, nl + nc - 1 - s, nl - 1 - (s - nc)), 0)
    return fwd, bwd, nl + nc


def _ret_kernel(lgr_ref, lgl_ref, hm_ref, bd64_ref, zf_ref, zb_ref, of_ref, ob_ref, st_ref):
    nb = zf_ref.shape[0]
    c = RET_CHUNK_LEN

    @pl.when(pl.program_id(0) == 0)
    def _():
        st_ref[...] = jnp.zeros_like(st_ref)

    hm = hm_ref[...]
    bd64_f = bd64_ref[...].astype(F32)
    ii = lax.broadcasted_iota(jnp.int32, (N_HEADS * c, c), 0) & (c - 1)
    jj = lax.broadcasted_iota(jnp.int32, (N_HEADS * c, c), 1)
    ri = lax.broadcasted_iota(jnp.int32, (c, BRANCH_W), 0).astype(F32)
    for rev, z_ref, o_ref in ((False, zf_ref, of_ref), (True, zb_ref, ob_ref)):
        d = int(rev)
        dist = ((jj - ii) if rev else (ii - jj)).astype(F32)
        dec = jnp.where(dist >= 0, jnp.exp(lgr_ref[d] * jnp.maximum(dist, 0.0)), 0.0)
        lgl = lgl_ref[d:d + 1, :]
        q_dec = jnp.exp(lgl * ((c - ri) if rev else (ri + 1.0)))
        k_dec = jnp.exp(lgl * (ri if rev else (c - 1.0 - ri)))
        c_dec = jnp.exp(lgl * float(c))
        for b in range(nb):
            q = z_ref[b, :, 0:256]
            k = z_ref[b, :, 256:512]
            v = z_ref[b, :, 512:768]
            q4 = jnp.concatenate([q] * N_HEADS, axis=0) * hm
            o4 = _dot(_dot_nt(q4, k) * dec, v) * hm
            st = st_ref[d, b]
            o_ref[b] = o4[0:c] + o4[c:2 * c] + o4[2 * c:3 * c] + o4[3 * c:4 * c] + _dot(q, st) * q_dec
            st_ref[d, b] = st * c_dec + _dot_tn(k * k_dec, v) * bd64_f


def _retention(zr, lg_rows, lg_lane, head_mask, bd64, geo, layer):
    nb, s, _ = zr.shape
    c = RET_CHUNK_LEN
    fwd, bwd, steps = _scan_index_maps(geo, c)
    return pl.pallas_call(
        _ret_kernel,
        grid=(steps,),
        in_specs=[_resident((2, N_HEADS * c, c), layer), _resident((2, BRANCH_W), layer),
                  _resident((N_HEADS * c, BRANCH_W)),
                  _resident((256, 256)),
                  pl.BlockSpec((nb, c, 1024), fwd), pl.BlockSpec((nb, c, 1024), bwd)],
        out_specs=[pl.BlockSpec((nb, c, BRANCH_W), fwd), pl.BlockSpec((nb, c, BRANCH_W), bwd)],
        out_shape=[jax.ShapeDtypeStruct((nb, s, BRANCH_W), F32)] * 2,
        scratch_shapes=[pltpu.VMEM((2, nb, BRANCH_W, BRANCH_W), F32)],
        compiler_params=_cparams(("arbitrary",)),
        name="retention",
    )(lg_rows, lg_lane, head_mask, bd64, zr, zr)


GLA_FACTOR_MIN_LOG_DECAY = -40.0
GLA_FACTOR_MAX_KEY = 1e15


def _gla_prepare(z_ref, d, gu_ref, gb_ref, tri_ref):
    nb, c, _ = z_ref.shape
    m = nb * c
    q = z_ref[:, :, 0:256].reshape(m, BRANCH_W)
    k = z_ref[:, :, 256:512].reshape(m, BRANCH_W)
    v = z_ref[:, :, 512:768].reshape(m, BRANCH_W)
    logit = _dot(z_ref[:, :, 1024:GLA_GROUP_W].reshape(m, LANES), gu_ref[d]) + gb_ref[d:d + 1, :]
    g = (jnp.minimum(logit, 0.0) - jnp.log(1.0 + jnp.exp(-jnp.abs(logit)))) * GLA_TAU_INV
    g1 = g.astype(BF16)
    r1 = g - g1.astype(F32)
    g2 = r1.astype(BF16)
    g3 = (r1 - g2.astype(F32)).astype(BF16)
    tri = tri_ref[d]
    bcum = jnp.concatenate(
        [sum(jnp.dot(tri, t[b * c:(b + 1) * c], preferred_element_type=F32) for t in (g1, g2, g3))
         for b in range(nb)], axis=0)
    return q, k, v, bcum


def _gla_state_step(st_ref, d, b, q_state_b, kb, vb, bb, rev, bd64_f):
    c = kb.shape[0]
    b_last = bb[0:1, :] if rev else bb[c - 1:c, :]
    st = st_ref[d, b]
    o_b = _dot_nt(q_state_b, st)
    st_ref[d, b] = st * jnp.exp(b_last) + _dot_tn(vb, kb * jnp.exp(b_last - bb)) * bd64_f
    return o_b


def _gla_factored(d, rev, q, k, v, bcum, o_ref, st_ref, hm64_ref, causal_ref, bd64_f):
    nb, c, _ = o_ref.shape
    q_state = q * jnp.exp(bcum)
    k_inv = k * jnp.exp(-bcum)
    hm = hm64_ref[...]
    causal = causal_ref[d]
    outs = []
    for b in range(nb):
        rows = slice(b * c, (b + 1) * c)
        qb, vb = q_state[rows], v[rows]
        q4 = jnp.concatenate([qb] * N_HEADS, axis=0) * hm
        att = jnp.where(causal > 0.0, _dot_nt(q4, k_inv[rows]), 0.0)
        o4 = _dot(att, vb) * hm
        o_b = o4[0:c] + o4[c:2 * c] + o4[2 * c:3 * c] + o4[3 * c:4 * c]
        outs.append(o_b + _gla_state_step(st_ref, d, b, qb, k[rows], vb, bcum[rows], rev, bd64_f))
    o_ref[...] = jnp.concatenate(outs, axis=0).reshape(nb, c, BRANCH_W)


def _gla_pairwise(d, rev, q, k, v, bcum, o_ref, st_ref, w_ref, hm16_ref, bd64, bd64_f):
    nb, c, _ = o_ref.shape
    sub = GLA_SUB
    nsub = c // sub
    m = nb * c
    head_mask = hm16_ref[...]
    rin = lax.broadcasted_iota(jnp.int32, (m, BRANCH_W), 0) & (sub - 1)
    for dl in range(sub):
        if dl == 0:
            w = q * k
        else:
            shift = (m - dl) if rev else dl
            ks = pltpu.roll(k, shift, 0)
            bs = pltpu.roll(bcum, shift, 0)
            valid = (rin + dl <= sub - 1) if rev else (rin >= dl)
            w = jnp.where(valid, q * ks * jnp.exp(jnp.minimum(bcum - bs, 0.0)), 0.0)
        w_ref[dl * m:(dl + 1) * m, :] = w.astype(BF16)
    att = jnp.dot(w_ref[...], bd64, preferred_element_type=F32)
    o_acc = att[0:m] * v
    for dl in range(1, sub):
        o_acc = o_acc + att[dl * m:(dl + 1) * m] * pltpu.roll(v, (m - dl) if rev else dl, 0)

    q_state = q * jnp.exp(bcum)
    outs = []
    for b in range(nb):
        r0 = b * c
        qb, kb, vb, bb = q[r0:r0 + c], k[r0:r0 + c], v[r0:r0 + c], bcum[r0:r0 + c]
        rows = []
        for blk in range(nsub):
            a0, a1 = blk * sub, (blk + 1) * sub
            if rev and blk < nsub - 1:
                anchor, lo, hi = bb[a1:a1 + 1, :], a1, c
            elif (not rev) and blk > 0:
                anchor, lo, hi = bb[a0 - 1:a0, :], 0, a0
            else:
                rows.append(jnp.zeros((sub, BRANCH_W), F32))
                continue
            qa = qb[a0:a1] * jnp.exp(bb[a0:a1] - anchor)
            ka = kb[lo:hi] * jnp.exp(anchor - bb[lo:hi])
            q4 = jnp.concatenate([qa] * N_HEADS, axis=0) * head_mask
            o4 = _dot(_dot_nt(q4, ka), vb[lo:hi]) * head_mask
            rows.append(o4[0:sub] + o4[sub:2 * sub] + o4[2 * sub:3 * sub] + o4[3 * sub:4 * sub])
        o_b = _gla_state_step(st_ref, d, b, q_state[r0:r0 + c], kb, vb, bb, rev, bd64_f)
        outs.append(o_b + jnp.concatenate(rows, axis=0))
    o_ref[...] = (o_acc + jnp.concatenate(outs, axis=0)).reshape(nb, c, BRANCH_W)


def _gla_kernel(gu_ref, gb_ref, tri_ref, bd64_ref, hm16_ref, hm64_ref, causal_ref, zf_ref, zb_ref,
                of_ref, ob_ref, st_ref, w_ref):
    @pl.when(pl.program_id(0) == 0)
    def _():
        st_ref[...] = jnp.zeros_like(st_ref)

    bd64 = bd64_ref[...]
    bd64_f = bd64.astype(F32)
    fwd = _gla_prepare(zf_ref, 0, gu_ref, gb_ref, tri_ref)
    bwd = _gla_prepare(zb_ref, 1, gu_ref, gb_ref, tri_ref)
    min_decay = jnp.minimum(jnp.min(fwd[3]), jnp.min(bwd[3]))
    max_key = jnp.maximum(jnp.max(jnp.abs(fwd[1])), jnp.max(jnp.abs(bwd[1])))
    factor_ok = jnp.logical_and(min_decay > GLA_FACTOR_MIN_LOG_DECAY, max_key < GLA_FACTOR_MAX_KEY)

    @pl.when(factor_ok)
    def _():
        _gla_factored(0, False, *fwd, of_ref, st_ref, hm64_ref, causal_ref, bd64_f)
        _gla_factored(1, True, *bwd, ob_ref, st_ref, hm64_ref, causal_ref, bd64_f)

    @pl.when(jnp.logical_not(factor_ok))
    def _():
        _gla_pairwise(0, False, *fwd, of_ref, st_ref, w_ref, hm16_ref, bd64, bd64_f)
        _gla_pairwise(1, True, *bwd, ob_ref, st_ref, w_ref, hm16_ref, bd64, bd64_f)


def _gla(zg, gate_up_pad, gate_b, tri, bd64, hm16, hm64, causal, geo, layer):
    nb, s, _ = zg.shape
    c = GLA_CHUNK_LEN
    fwd, bwd, steps = _scan_index_maps(geo, c)
    return pl.pallas_call(
        _gla_kernel,
        grid=(steps,),
        in_specs=[_resident((2, LANES, BRANCH_W), layer), _resident((2, BRANCH_W), layer),
                  _resident((2, c, c)), _resident((256, 256)), _resident((N_HEADS * GLA_SUB, BRANCH_W)),
                  _resident((N_HEADS * c, BRANCH_W)), _resident((2, N_HEADS * c, c)),
                  pl.BlockSpec((nb, c, GLA_GROUP_W), fwd), pl.BlockSpec((nb, c, GLA_GROUP_W), bwd)],
        out_specs=[pl.BlockSpec((nb, c, BRANCH_W), fwd), pl.BlockSpec((nb, c, BRANCH_W), bwd)],
        out_shape=[jax.ShapeDtypeStruct((nb, s, BRANCH_W), F32)] * 2,
        scratch_shapes=[pltpu.VMEM((2, nb, BRANCH_W, BRANCH_W), F32),
                        pltpu.VMEM((GLA_SUB * nb * c, BRANCH_W), BF16)],
        compiler_params=_cparams(("arbitrary",)),
        name="gla",
    )(gate_up_pad, gate_b, tri, bd64, hm16, hm64, causal, zg, zg)


SOFTMAX_MIN_SUM = 2.0 ** -80


def _softmax_pv(qk_pairs, v, s_ref, p_ref, slot):
    tq = qk_pairs[0][0].shape[0]
    n_keys = v.shape[0]
    rows = len(qk_pairs) * tq
    for i, (q, k) in enumerate(qk_pairs):
        s_ref[i * tq:(i + 1) * tq, 0:n_keys] = _dot_nt(q, k)
    sums = []
    for r0 in range(0, rows, SOFTMAX_ROWS):
        s = s_ref[r0:r0 + SOFTMAX_ROWS, 0:n_keys]
        p = jnp.exp2(s - jnp.max(s, axis=-1, keepdims=True))
        sums.append(jnp.sum(p, axis=-1, keepdims=True))
        p_ref[slot, r0:r0 + SOFTMAX_ROWS, 0:n_keys] = p.astype(BF16)
    l = jnp.concatenate(sums, axis=0)
    o = jnp.dot(p_ref[slot, 0:rows, 0:n_keys], v, preferred_element_type=F32) / l
    return [o[i * tq:(i + 1) * tq] for i in range(len(qk_pairs))]


def _softmax_pv_bounded(qk_pairs, v, bounds, p_ref, slot, shared_keys, mix=None):
    tq = qk_pairs[0][0].shape[0]
    n_keys = v.shape[0]
    rows = len(qk_pairs) * tq
    if shared_keys:
        q_all = jnp.concatenate([q for q, _ in qk_pairs], axis=0)
        p = jnp.exp2(_dot_nt(q_all, qk_pairs[0][1]) - jnp.concatenate(bounds, axis=0))
        l = jnp.sum(p, axis=-1, keepdims=True)
        p_ref[slot, 0:rows, 0:n_keys] = p.astype(BF16)
    else:
        sums = []
        for i, (q, k) in enumerate(qk_pairs):
            p = jnp.exp2(_dot_nt(q, k) - bounds[i])
            sums.append(jnp.sum(p, axis=-1, keepdims=True))
            p_ref[slot, i * tq:(i + 1) * tq, 0:n_keys] = p.astype(BF16)
        l = jnp.concatenate(sums, axis=0)
        if mix is not None:
            l0, l1 = sums
            weight = (mix * l0 / l1).astype(BF16)
            p_ref[slot, 0:tq, 0:n_keys] = (p_ref[slot, 0:tq, 0:n_keys]
                                           - weight * p_ref[slot, tq:2 * tq, 0:n_keys])
            o = jnp.dot(p_ref[slot, 0:tq, 0:n_keys], v, preferred_element_type=F32) / l0
            return [o], jnp.min(l)
    o = jnp.dot(p_ref[slot, 0:rows, 0:n_keys], v, preferred_element_type=F32) / l
    return [o[i * tq:(i + 1) * tq] for i in range(len(qk_pairs))], jnp.min(l)


def _score_bounds(q_ref, key_max, bd):
    a = jnp.abs(q_ref[...].astype(F32)) * key_max
    return jnp.dot(a.astype(BF16), bd, preferred_element_type=F32)


def _attention_body(units, finish, q_ref, k_ref, bound, o_ref, s_ref, p_ref, mixes=None):
    def run(bounded):
        outs, l_min = [], None
        for u, (v, pairs) in enumerate(units):
            qk = [(q_ref[:, qs], k_ref[:, ks]) for qs, ks, _ in pairs]
            if bounded:
                shared = all(ks == pairs[0][1] for _, ks, _ in pairs)
                o, lm = _softmax_pv_bounded(qk, v, [bound[:, c:c + 1] for _, _, c in pairs], p_ref, u % 2, shared,
                                            None if mixes is None else mixes[u])
                l_min = lm if l_min is None else jnp.minimum(l_min, lm)
            else:
                o = _softmax_pv(qk, v, s_ref, p_ref, u % 2)
            outs.extend(o)
        o_ref[...] = finish(outs, bounded and mixes is not None).astype(o_ref.dtype)
        return l_min

    l_min = run(True)
    pl.when(jnp.logical_not(l_min >= SOFTMAX_MIN_SUM))(lambda: run(False))


def _key_max(k_ref, kmax_ref):
    @pl.when(pl.program_id(1) == 0)
    def _():
        kmax_ref[...] = jnp.max(jnp.abs(k_ref[...].astype(F32)), axis=0, keepdims=True)
    return kmax_ref[...]


def _gqa_kernel(bd64_ref, q_ref, k_ref, v_ref, *rest):
    o_ref, s_ref, p_ref, kmax_ref = rest[-4:]
    km = _key_max(k_ref, kmax_ref)
    km = jnp.concatenate([km[:, 0:HEAD_W]] * 2 + [km[:, HEAD_W:2 * HEAD_W]] * 2, axis=1)
    bound = _score_bounds(q_ref, km, bd64_ref[...])
    units = []
    for kv in range(2):
        cs = slice(kv * HEAD_W, (kv + 1) * HEAD_W)
        pairs = [(slice(h * HEAD_W, (h + 1) * HEAD_W), cs, h * HEAD_W) for h in (2 * kv, 2 * kv + 1)]
        units.append((v_ref[:, cs], pairs))
    _attention_body(units, lambda outs, _: jnp.concatenate(outs, axis=1), q_ref, k_ref, bound, o_ref, s_ref, p_ref)


def _dif_kernel(lam_ref, sub_ref, bd64_ref, bd32_ref, q_ref, k_ref, v_ref, *rest, out_scale):
    o_ref, s_ref, p_ref, kmax_ref = rest[-4:]
    bound = _score_bounds(q_ref, _key_max(k_ref, kmax_ref), bd32_ref[...])
    units = []
    for h in range(N_HEADS):
        pairs = []
        for mp in range(2):
            qs = slice((2 * h + mp) * DIF_QK, (2 * h + mp + 1) * DIF_QK)
            pairs.append((qs, qs, (2 * h + mp) * DIF_QK))
        units.append((v_ref[:, h * HEAD_W:(h + 1) * HEAD_W], pairs))

    def finish(outs, merged):
        heads = outs if merged else [outs[2 * h] - lam_ref[:, h * HEAD_W:(h + 1) * HEAD_W] * outs[2 * h + 1]
                                     for h in range(N_HEADS)]
        return _group_rms(jnp.concatenate(heads, axis=1), bd64_ref[...], sub_ref[...], HEAD_W) * out_scale

    _attention_body(units, finish, q_ref, k_ref, bound, o_ref, s_ref, p_ref, mixes=[lam_ref[:, 0:1]] * N_HEADS)


def _attention(body, name, pre_specs, pre_args, q, k, v, geo, need_ctx):
    nb, t, n_ctx, tq = geo["batch"], geo["seq"], geo["ctx"], geo["tq"]
    s = t + n_ctx
    kw, vw = k.shape[2], v.shape[2]
    keys = lambda n: pl.BlockSpec((None, n, kw), lambda b, i: (b, 0 if n == s else t // n_ctx, 0),
                                  pipeline_mode=pl.Buffered(1))
    vals = lambda n: pl.BlockSpec((None, n, vw), lambda b, i: (b, 0 if n == s else t // n_ctx, 0),
                                  pipeline_mode=pl.Buffered(1))

    def call(n_keys, q_tiles, q_off, prev):
        qmap = lambda b, i: (b, q_off + i, 0)
        alias = {} if prev is None else {len(pre_args) + 3: 0}
        extra = [] if prev is None else [pl.BlockSpec(memory_space=pl.ANY)]
        return pl.pallas_call(
            body, grid=(nb, q_tiles),
            in_specs=pre_specs + [pl.BlockSpec((None, tq, BRANCH_W), qmap), keys(n_keys), vals(n_keys)] + extra,
            out_specs=pl.BlockSpec((None, tq, BRANCH_W), qmap),
            out_shape=jax.ShapeDtypeStruct(q.shape, BF16),
            scratch_shapes=[pltpu.VMEM((2 * tq, n_keys), F32), pltpu.VMEM((2, 2 * tq, n_keys), BF16),
                            pltpu.VMEM((1, kw), F32)],
            input_output_aliases=alias,
            compiler_params=_cparams(("parallel", "arbitrary")),
            name=name,
        )(*pre_args, q, k, v, *([] if prev is None else [prev]))

    out = call(s, t // tq, 0, None)
    if need_ctx:
        out = call(n_ctx, n_ctx // tq, t // tq, out)
    return out


def _gqa(cq, ck, cv, bd64, geo, need_ctx):
    return _attention(_gqa_kernel, "gqa", [_resident((256, 256))], [bd64], cq, ck, cv, geo, need_ctx)


def _dif(dq, dk, dv, lam_lane, subln, bd64, bd32, geo, need_ctx, out_scale, layer):
    pre = [_resident((1, BRANCH_W), layer), _resident((1, BRANCH_W), layer), _resident((256, 256)),
           _resident((256, 256))]
    return _attention(functools.partial(_dif_kernel, out_scale=out_scale), "diffattn", pre,
                      [lam_lane, subln, bd64, bd32], dq, dk, dv, geo, need_ctx)


def _merge_kernel(x_ref, mod_ref, n1_ref, wg_ref, wb_ref, wo_ref, bd64_ref, gnw_ref,
                  rof_ref, rob_ref, rg_ref, gof_ref, gob_ref, gr_ref, yc_ref, yd_ref, o_ref):
    nb, rt, d = x_ref.shape
    m = nb * rt
    bd64 = bd64_ref[...]
    for g0, g1 in _sample_groups(nb):
        ng = g1 - g0
        m = ng * rt
        x = x_ref[g0:g1]
        hb = _adaln(x, n1_ref[...], mod_ref[g0:g1, 0:1, :], mod_ref[g0:g1, 1:2, :]).astype(BF16).reshape(m, d)
        flat = lambda ref: ref[g0:g1].reshape(m, BRANCH_W)
        ya = _group_rms(flat(rof_ref) + flat(rob_ref), bd64, None, HEAD_W) * _silu(flat(rg_ref))
        yb = _group_rms(flat(gof_ref) + flat(gob_ref), bd64, gnw_ref[...], HEAD_W) * _silu(flat(gr_ref))
        ys = (ya.astype(BF16), yb.astype(BF16), flat(yc_ref), flat(yd_ref))
        mix = None
        for i in range(4):
            gate = _sigmoid(jnp.dot(hb, wg_ref[:, i * d:(i + 1) * d], preferred_element_type=F32))
            term = gate * jnp.dot(ys[i], wb_ref[i], preferred_element_type=F32)
            mix = term if mix is None else mix + term
        o_ref[g0:g1] = x + mod_ref[g0:g1, 2:3, :] * _dot(mix, wo_ref[...]).reshape(ng, rt, d)


def _merge(xs, mod, norm1, w_gate, w_branch, w_out, bd64, gla_norm, rof, rob, zr, gof, gob, zg, yc, yd, geo,
           need_ctx, layer):
    nb, s, d = xs.shape
    rt = geo["rt_merge"]
    tiles = (s if need_ctx else geo["seq"]) // rt
    row = lambda i: (0, i, 0)
    col3 = lambda i: (0, i, 3)
    bw = BRANCH_W
    act = lambda imap: pl.BlockSpec((nb, rt, bw), imap)
    return pl.pallas_call(
        _merge_kernel,
        grid=(tiles,),
        in_specs=[pl.BlockSpec((nb, rt, d), row),
                  _mod_spec(nb, d, geo["seq"] // rt, layer),
                  _resident((1, d), layer), _resident((d, 4 * d), layer), _resident((4, bw, d), layer),
                  _resident((d, d), layer), _resident((256, 256)), _resident((1, bw), layer),
                  act(row), act(row), act(col3), act(row), act(row), act(col3), act(row), act(row)],
        out_specs=pl.BlockSpec((nb, rt, d), row),
        out_shape=jax.ShapeDtypeStruct((nb, s, d), F32),
        input_output_aliases={0: 0},
        compiler_params=_cparams(("parallel",)),
        name="merge",
    )(xs, mod, norm1, w_gate, w_branch, w_out, bd64, gla_norm, rof, rob, zr, gof, gob, zg, yc, yd)


FFN_CHUNKS = ((0, 1024), (1024, 2048), (2048, D_FF_W))


def _ffn_kernel(x_ref, xp_ref, xn_ref, mod_ref, n2_ref, wu_ref, cw_ref, wd_ref, o_ref, *, seq, total):
    nb, rt, d = x_ref.shape
    ext = rt + 2 * HALO
    first = pl.program_id(0) * rt
    keep_prev = jnp.where(jnp.logical_or(first == 0, first == seq), 0.0, 1.0)
    keep_next = jnp.where(jnp.logical_or(first + rt == seq, first + rt == total), 0.0, 1.0)
    for g0, g1 in _sample_groups(nb):
        ng = g1 - g0

        def hn(ref):
            return _adaln(ref[g0:g1], n2_ref[...], mod_ref[g0:g1, 3:4, :], mod_ref[g0:g1, 4:5, :])
        h_mid = hn(x_ref).astype(BF16)
        h_ext = jnp.concatenate([(hn(xp_ref) * keep_prev).astype(BF16), h_mid,
                                 (hn(xn_ref) * keep_next).astype(BF16)], axis=1).reshape(ng * ext, d)
        h_mid = h_mid.reshape(ng * rt, d)
        acc = None
        for f0, f1 in FFN_CHUNKS:
            tf = f1 - f0
            u = jnp.dot(h_ext, wu_ref[:, f0:f1], preferred_element_type=F32)
            centre = lambda a: a.reshape(ng, ext, tf)[:, HALO:HALO + rt, :]
            up = centre(pltpu.roll(u, 1, 0))
            mid = centre(u)
            dn = centre(pltpu.roll(u, ng * ext - 1, 0))
            a = (up * cw_ref[0:1, f0:f1] + mid * cw_ref[1:2, f0:f1] + dn * cw_ref[2:3, f0:f1]
                 + cw_ref[3:4, f0:f1])
            gate = jnp.dot(h_mid, wu_ref[:, D_FF_W + f0:D_FF_W + f1], preferred_element_type=F32)
            part = _dot((_silu(a) * gate.reshape(ng, rt, tf)).reshape(ng * rt, tf), wd_ref[f0:f1, :])
            acc = part if acc is None else acc + part
        o_ref[g0:g1] = x_ref[g0:g1] + mod_ref[g0:g1, 5:6, :] * acc.reshape(ng, rt, d)


def _ffn(xs, mod, norm2, w_up, conv_pack, w_down, geo, need_ctx, layer):
    nb, s, d = xs.shape
    rt, t = geo["rt_ffn"], geo["seq"]
    s_out = s if need_ctx else t
    hb = rt // HALO
    last_blk = s // HALO - 1
    row = lambda i: (0, i, 0)
    return pl.pallas_call(
        functools.partial(_ffn_kernel, seq=t, total=s),
        grid=(s_out // rt,),
        in_specs=[pl.BlockSpec((nb, rt, d), row),
                  pl.BlockSpec((nb, HALO, d), lambda i: (0, jnp.maximum(i * hb - 1, 0), 0)),
                  pl.BlockSpec((nb, HALO, d), lambda i: (0, jnp.minimum((i + 1) * hb, last_blk), 0)),
                  _mod_spec(nb, d, t // rt, layer),
                  _resident((1, d), layer),
                  _resident((d, 2 * D_FF_W), layer),
                  _resident((8, D_FF_W), layer),
                  _resident((D_FF_W, d), layer)],
        out_specs=pl.BlockSpec((nb, rt, d), row),
        out_shape=jax.ShapeDtypeStruct((nb, s_out, d), F32),
        compiler_params=_cparams(("parallel",)),
        name="convffn",
    )(xs, xs, xs, mod, norm2, w_up, conv_pack, w_down)


def _rope_tables(t, n_ctx):
    rows = t // GRID_WIDTH

    def axial(dim):
        axis_dim = dim // 2
        freqs = ROPE_BASE ** (-np.arange(0, axis_dim, 2, dtype=np.float64) / axis_dim)
        row = np.repeat(np.arange(rows, dtype=np.float64), GRID_WIDTH)
        col = np.tile(np.arange(GRID_WIDTH, dtype=np.float64), rows)
        ang = np.concatenate([row[:, None] * freqs, col[:, None] * freqs], axis=-1)
        return np.cos(ang), np.sin(ang)

    freqs = ROPE_BASE ** (-np.linspace(0.0, 1.0, HEAD_W // 2))
    ang = np.arange(t, dtype=np.float64)[:, None] * freqs
    parts = []
    for cos, sin in ((np.cos(ang), np.sin(ang)), axial(HEAD_W), axial(DIF_QK)):
        reps = LANES // (2 * cos.shape[1])
        parts.append(np.tile(cos, (1, 2 * reps)))
        parts.append(np.tile(np.concatenate([-sin, sin], axis=1), (1, reps)))
    table = np.concatenate(parts, axis=1)
    ident = np.tile(np.concatenate([np.ones((1, LANES)), np.zeros((1, LANES))], axis=1), (n_ctx, 3))
    return jnp.asarray(np.concatenate([table, ident], axis=0), dtype=F32)


def _block_diag_ones(width, group):
    idx = np.arange(width) // group
    return jnp.asarray(idx[:, None] == idx[None, :], dtype=BF16)


def _row_block_lane_mask(block, n_lanes, lane_group):
    n_groups = n_lanes // lane_group
    rows = np.arange(n_groups * block)[:, None] // block
    return jnp.asarray(rows == np.arange(n_lanes)[None, :] // lane_group, dtype=F32)


def kernel(x, c, ctx, c_ctx, w_mod, b_mod, norm1, norm2, w_in, ret_decay, gla_gate_up, gla_gate_b, gla_norm,
           gqa_qnorm, gqa_knorm, dif_qnorm, dif_knorm, dif_lambda, dif_subln, w_branch, w_out, w_up, conv_w,
           conv_b, w_down):
    nb, t, d = x.shape
    n_ctx = ctx.shape[1]
    depth = w_mod.shape[0]
    assert t % 256 == 0 and n_ctx % 256 == 0 and nb + 1 <= 8
    geo = dict(batch=nb, seq=t, ctx=n_ctx, rt=256, rt_merge=128, rt_ffn=256, tq=256)

    xs = jnp.concatenate([x, ctx], axis=1)
    cc = jnp.zeros((8, d), F32).at[0:nb].set(c).at[nb].set(c_ctx)
    mod_all = _modulation(cc, w_mod, b_mod).reshape(depth, 8, 6, d)

    table = _rope_tables(t, n_ctx)
    bd64 = _block_diag_ones(256, HEAD_W)
    bd32 = _block_diag_ones(256, DIF_QK)
    ti = np.arange(GLA_CHUNK_LEN)
    tri = jnp.asarray(np.stack([ti[:, None] >= ti[None, :], ti[:, None] <= ti[None, :]]), dtype=BF16)
    hm16 = _row_block_lane_mask(GLA_SUB, BRANCH_W, HEAD_W)
    hm64 = _row_block_lane_mask(GLA_CHUNK_LEN, BRANCH_W, HEAD_W)
    hm128 = _row_block_lane_mask(RET_CHUNK_LEN, BRANCH_W, HEAD_W)
    qi = np.arange(N_HEADS * GLA_CHUNK_LEN)[:, None] % GLA_CHUNK_LEN
    kj = np.arange(GLA_CHUNK_LEN)[None, :]
    causal = jnp.asarray(np.stack([kj <= qi, kj >= qi]), dtype=F32)

    n_mix = 2080
    n_gate0 = 3360
    w_proj = jnp.concatenate([w_in[:, :, :n_mix], jnp.zeros((depth, d, LANES - 2 * GLA_RANK_W), F32),
                              w_in[:, :, n_mix:n_gate0]], axis=2).astype(BF16)
    w_gate = w_in[:, :, n_gate0:].astype(BF16)
    w_branch_b, w_out_b, w_up_b, w_down_b = (a.astype(BF16) for a in (w_branch, w_out, w_up, w_down))
    qk_norms = jnp.stack([jnp.tile(gqa_qnorm, (1, 4)), jnp.tile(gqa_knorm, (1, 4)), jnp.tile(dif_qnorm, (1, 8)),
                          jnp.tile(dif_knorm, (1, 8))] + [jnp.zeros((depth, 256), F32)] * 4, axis=1)
    mod = jnp.stack([mod_all[:, 0:nb], jnp.broadcast_to(mod_all[:, nb:nb + 1], (depth, nb, 6, d))], axis=1)
    n1 = norm1[:, None, :]
    n2 = norm2[:, None, :]
    log_gamma = jax.nn.log_sigmoid(ret_decay.astype(F32))
    lg_rows = jnp.broadcast_to(jnp.repeat(log_gamma, RET_CHUNK_LEN, axis=2)[..., None],
                               (depth, 2, N_HEADS * RET_CHUNK_LEN, RET_CHUNK_LEN))
    lg_lane = jnp.repeat(log_gamma, HEAD_W, axis=2)
    pad_rows = lambda a, lo: jnp.pad(a, ((0, 0), (lo, LANES - GLA_RANK_W - lo), (0, 0)))
    gate_up = jnp.stack([pad_rows(gla_gate_up[:, 0], 0), pad_rows(gla_gate_up[:, 1], GLA_RANK_W)],
                        axis=1).astype(BF16)
    lam_init = [0.8 - 0.6 * math.exp(-0.3 * l) for l in range(depth)]
    lv = dif_lambda.astype(F32)
    lam = (jnp.exp(jnp.sum(lv[:, 0] * lv[:, 1], axis=-1)) - jnp.exp(jnp.sum(lv[:, 2] * lv[:, 3], axis=-1))
           + jnp.asarray(lam_init, F32))
    lam_lane = jnp.broadcast_to(lam[:, None, None], (depth, 1, BRANCH_W))
    subln = jnp.tile(dif_subln, (1, 4))[:, None, :]
    gnw = jnp.tile(gla_norm, (1, 4))[:, None, :]
    conv_pack = jnp.concatenate([conv_w, conv_b[:, None, :], jnp.zeros((depth, 4, D_FF_W), F32)], axis=1)

    for l in range(depth):
        need_ctx = l < depth - 1
        zr, zg, cq, ck, cv, dq, dk, dv = _inproj(xs, mod, n1, w_proj, table, bd64, bd32, qk_norms, geo, l)
        rof, rob = _retention(zr, lg_rows, lg_lane, hm128, bd64, geo, l)
        gof, gob = _gla(zg, gate_up, gla_gate_b, tri, bd64, hm16, hm64, causal, geo, l)
        yc = _gqa(cq, ck, cv, bd64, geo, need_ctx)
        yd = _dif(dq, dk, dv, lam_lane, subln, bd64, bd32, geo, need_ctx, 1.0 - lam_init[l], l)
        xs = _merge(xs, mod, n1, w_gate, w_branch_b, w_out_b, bd64, gnw, rof, rob, zr, gof, gob, zg, yc, yd,
                    geo, need_ctx, l)
        xs = _ffn(xs, mod, n2, w_up_b, conv_pack, w_down_b, geo, need_ctx, l)

    return xs
```

```python
import functools
import math

import jax
import jax.numpy as jnp
import numpy as np
from jax import lax
from jax.experimental import pallas as pl
from jax.experimental.pallas import tpu as pltpu

F32 = jnp.float32
BF16 = jnp.bfloat16

N_HEADS = 4
HEAD_W = 64
BRANCH_W = N_HEADS * HEAD_W
DIF_QK = 32
RET_CHUNK_LEN = 128
GLA_CHUNK_LEN = 128
GLA_SUB = 16
GLA_RANK_W = 16
GLA_TAU_INV = 1.0 / 16.0
GRID_WIDTH = 64
ROPE_BASE = 10000.0
EPS = 1e-6
D_FF_W = 2816
LOG2E = 1.4426950408889634

LANES = 128
VMEM_LIMIT_BYTES = 56 * 1024 * 1024

RET_OFF, GLA_OFF, GQA_OFF, DIF_OFF, PROJ_W = 0, 1024, 2176, 2688, 3456
GLA_GROUP_W = GQA_OFF - GLA_OFF
HALO = 16
SOFTMAX_ROWS = 128


def _cparams(sem):
    return pltpu.CompilerParams(dimension_semantics=sem, vmem_limit_bytes=VMEM_LIMIT_BYTES)


def _resident(shape, layer=None):
    nd = len(shape)
    if layer is None:
        return pl.BlockSpec(shape, lambda *_: (0,) * nd, pipeline_mode=pl.Buffered(1))
    return pl.BlockSpec((None,) + tuple(shape), lambda *_: (layer,) + (0,) * nd, pipeline_mode=pl.Buffered(1))


def _sample_groups(nb):
    half = nb // 2
    return [(0, nb)] if nb % 2 else [(0, half), (half, nb)]


def _dot(a, b):
    return jnp.dot(a.astype(BF16), b.astype(BF16), preferred_element_type=F32)


def _dot_nt(a, b):
    return lax.dot_general(a.astype(BF16), b.astype(BF16), (((1,), (1,)), ((), ())),
                           preferred_element_type=F32)


def _dot_tn(a, b):
    return lax.dot_general(a.astype(BF16), b.astype(BF16), (((0,), (0,)), ((), ())),
                           preferred_element_type=F32)


def _sigmoid(x):
    return 0.5 * jnp.tanh(0.5 * x) + 0.5


def _silu(x):
    return x * _sigmoid(x)


def _adaln(x, norm_w, shift, scale):
    h = x * lax.rsqrt(jnp.mean(x * x, axis=-1, keepdims=True) + EPS) * norm_w
    return h * (1.0 + scale) + shift


def _group_rms(x, bd, weight, group):
    ss = jnp.dot((x * x).astype(BF16), bd, preferred_element_type=F32)
    y = x * lax.rsqrt(ss * (1.0 / group) + EPS)
    return y if weight is None else y * weight


def _rope(x, cos, sin_signed, half):
    w = x.shape[1]
    reps = w // LANES
    if reps > 1:
        cos = jnp.concatenate([cos] * reps, axis=1)
        sin_signed = jnp.concatenate([sin_signed] * reps, axis=1)
    lane = lax.broadcasted_iota(jnp.int32, x.shape, 1)
    lower = (lane & (2 * half - 1)) < half
    partner = jnp.where(lower, pltpu.roll(x, w - half, 1), pltpu.roll(x, half, 1))
    return x * cos + partner * sin_signed


def _mod_kernel(c_ref, w_ref, b_ref, o_ref):
    o_ref[...] = _dot(_silu(c_ref[...]), w_ref[...]) + b_ref[...]


def _modulation(cc, w_mod, b_mod):
    n_layers, d, width = w_mod.shape
    tn = 1024
    return pl.pallas_call(
        _mod_kernel,
        grid=(n_layers, width // tn),
        in_specs=[pl.BlockSpec((8, d), lambda l, j: (0, 0)),
                  pl.BlockSpec((None, d, tn), lambda l, j: (l, 0, j)),
                  pl.BlockSpec((None, 1, tn), lambda l, j: (l, 0, j))],
        out_specs=pl.BlockSpec((None, 8, tn), lambda l, j: (l, 0, j)),
        out_shape=jax.ShapeDtypeStruct((n_layers, 8, width), F32),
        compiler_params=_cparams(("parallel", "parallel")),
        name="modulation",
    )(cc, w_mod, b_mod.reshape(n_layers, 1, width))


W_IN_CODES_END = 2080
W_IN_PAD = LANES - 2 * GLA_RANK_W


def _repack_kernel(w_ref, o_ref):
    rows, n_in = w_ref.shape
    o_ref[:, 0:W_IN_CODES_END] = w_ref[:, 0:W_IN_CODES_END].astype(BF16)
    o_ref[:, W_IN_CODES_END:W_IN_CODES_END + W_IN_PAD] = jnp.zeros((rows, W_IN_PAD), BF16)
    o_ref[:, W_IN_CODES_END + W_IN_PAD:n_in + W_IN_PAD] = w_ref[:, W_IN_CODES_END:n_in].astype(BF16)


def _repack_w_in(w_in):
    n_layers, d, n_in = w_in.shape
    tr = 256
    return pl.pallas_call(
        _repack_kernel,
        grid=(n_layers, d // tr),
        in_specs=[pl.BlockSpec((None, tr, n_in), lambda l, i: (l, i, 0))],
        out_specs=pl.BlockSpec((None, tr, n_in + W_IN_PAD), lambda l, i: (l, i, 0)),
        out_shape=jax.ShapeDtypeStruct((n_layers, d, n_in + W_IN_PAD), BF16),
        compiler_params=_cparams(("parallel", "parallel")),
        name="repack_w_in",
    )(w_in)


def _mod_spec(nb, d, n_lat_tiles, layer, tile_off=0):
    return pl.BlockSpec((None, None, nb, 6, d),
                        lambda i, *_: (layer, jnp.where(tile_off + i < n_lat_tiles, 0, 1), 0, 0, 0))


def _token_sources(x, ctx, xs, rt):
    if xs is not None:
        return [(xs, 0, xs.shape[1] // rt)]
    return [(x, 0, x.shape[1] // rt), (ctx, x.shape[1] // rt, ctx.shape[1] // rt)]


def _inproj_kernel(x_ref, mod_ref, n1_ref, w_ref, tab_ref, bd64_ref, bd32_ref, nw_ref, *rest):
    zr_ref, zg_ref, cq_ref, ck_ref, cv_ref, dq_ref, dk_ref, dv_ref = rest[-8:]
    nb, rt, d = x_ref.shape
    m = nb * rt
    hb = _adaln(x_ref[...], n1_ref[...], mod_ref[:, 0:1, :], mod_ref[:, 1:2, :]).astype(BF16).reshape(m, d)

    def table(lo):
        t = tab_ref[:, lo:lo + LANES]
        return jnp.concatenate([t] * nb, axis=0)

    r_cos, r_sin, g_cos, g_sin, d_cos, d_sin = (table(i * LANES) for i in range(6))

    def put(ref, lo, val):
        ref[:, :, lo:lo + val.shape[1]] = val.reshape(nb, rt, val.shape[1]).astype(ref.dtype)

    zr = jnp.dot(hb, w_ref[:, RET_OFF:GLA_OFF], preferred_element_type=F32)
    put(zr_ref, 0, _rope(zr[:, 0:256], r_cos, r_sin, 32))
    put(zr_ref, 256, _rope(zr[:, 256:512], r_cos, r_sin, 32) * (HEAD_W ** -0.5))
    put(zr_ref, 512, zr[:, 512:1024])

    zg = jnp.dot(hb, w_ref[:, GLA_OFF:GQA_OFF], preferred_element_type=F32)
    put(zg_ref, 0, zg[:, 0:256] * (HEAD_W ** -0.5))
    put(zg_ref, 256, zg[:, 256:GLA_GROUP_W])

    zc = jnp.dot(hb, w_ref[:, GQA_OFF:DIF_OFF], preferred_element_type=F32)
    bd64 = bd64_ref[...]
    q = _group_rms(zc[:, 0:256], bd64, nw_ref[0:1, :], HEAD_W)
    put(cq_ref, 0, _rope(q, g_cos, g_sin, 32) * (HEAD_W ** -0.5 * LOG2E))
    k = _group_rms(zc[:, 256:384], bd64[0:128, 0:128], nw_ref[1:2, 0:128], HEAD_W)
    put(ck_ref, 0, _rope(k, g_cos, g_sin, 32))
    put(cv_ref, 0, zc[:, 384:512])

    zd = jnp.dot(hb, w_ref[:, DIF_OFF:PROJ_W], preferred_element_type=F32)
    bd32 = bd32_ref[...]
    q = _group_rms(zd[:, 0:256], bd32, nw_ref[2:3, :], DIF_QK)
    put(dq_ref, 0, _rope(q, d_cos, d_sin, 16) * (DIF_QK ** -0.5 * LOG2E))
    k = _group_rms(zd[:, 256:512], bd32, nw_ref[3:4, :], DIF_QK)
    put(dk_ref, 0, _rope(k, d_cos, d_sin, 16))
    put(dv_ref, 0, zd[:, 512:768])


def _inproj(sources, mod, norm1, w_proj, table, bd64, bd32, qk_norms, geo, layer):
    nb, _, d = sources[0][0].shape
    s = geo["seq"] + geo["ctx"]
    rt = geo["rt"]
    widths = (1024, GLA_GROUP_W, 256, 128, 128, 256, 256, 256)
    dtypes = (F32, F32, BF16, BF16, BF16, BF16, BF16, BF16)
    outs = None
    for src, off, n_tiles in sources:
        prev = [] if outs is None else list(outs)
        n_in = 8
        outs = pl.pallas_call(
            _inproj_kernel,
            grid=(n_tiles,),
            in_specs=[pl.BlockSpec((nb, rt, d), lambda i: (0, i, 0)),
                      _mod_spec(nb, d, geo["seq"] // rt, layer, off),
                      _resident((1, d), layer),
                      _resident((d, PROJ_W), layer),
                      pl.BlockSpec((rt, 768), lambda i, off=off: (off + i, 0)),
                      _resident((256, 256)),
                      _resident((256, 256)),
                      _resident((8, 256), layer)] + [pl.BlockSpec(memory_space=pl.ANY)] * len(prev),
            out_specs=[pl.BlockSpec((nb, rt, w), lambda i, off=off: (0, off + i, 0)) for w in widths],
            out_shape=[jax.ShapeDtypeStruct((nb, s, w), t) for w, t in zip(widths, dtypes)],
            input_output_aliases={n_in + k: k for k in range(len(prev))},
            compiler_params=_cparams(("parallel",)),
            name="inproj",
        )(src, mod, norm1, w_proj, table, bd64, bd32, qk_norms, *prev)
    return outs


def _scan_index_maps(geo, chunk):
    nl, nc = geo["seq"] // chunk, geo["ctx"] // chunk
    fwd = lambda s: (0, jnp.where(s < nc, nl + s, s - nc), 0)
    bwd = lambda s: (0, jnp.where(s < nc, nl + nc - 1 - s, nl - 1 - (s - nc)), 0)
    return fwd, bwd, nl + nc


def _ret_kernel(lgr_ref, lgl_ref, hm_ref, bd64_ref, zf_ref, zb_ref, of_ref, ob_ref, st_ref, dec_ref):
    nb = zf_ref.shape[0]
    c = RET_CHUNK_LEN

    @pl.when(pl.program_id(0) == 0)
    def _():
        st_ref[...] = jnp.zeros_like(st_ref)
        ii = lax.broadcasted_iota(jnp.int32, (N_HEADS * c, c), 0) & (c - 1)
        jj = lax.broadcasted_iota(jnp.int32, (N_HEADS * c, c), 1)
        for d in range(2):
            dist = ((jj - ii) if d else (ii - jj)).astype(F32)
            dec_ref[d] = jnp.where(dist >= 0, jnp.exp(lgr_ref[d] * jnp.maximum(dist, 0.0)), 0.0)

    hm = hm_ref[...]
    bd64_f = bd64_ref[...].astype(F32)
    ri = lax.broadcasted_iota(jnp.int32, (c, BRANCH_W), 0).astype(F32)
    for rev, z_ref, o_ref in ((False, zf_ref, of_ref), (True, zb_ref, ob_ref)):
        d = int(rev)
        dec = dec_ref[d]
        lgl = lgl_ref[d:d + 1, :]
        q_dec = jnp.exp(lgl * ((c - ri) if rev else (ri + 1.0)))
        k_dec = jnp.exp(lgl * (ri if rev else (c - 1.0 - ri)))
        c_dec = jnp.exp(lgl * float(c))
        for b in range(nb):
            q = z_ref[b, :, 0:256]
            k = z_ref[b, :, 256:512]
            v = z_ref[b, :, 512:768]
            q4 = jnp.concatenate([q] * N_HEADS, axis=0) * hm
            o4 = _dot(_dot_nt(q4, k) * dec, v) * hm
            st = st_ref[d, b]
            o_ref[b] = o4[0:c] + o4[c:2 * c] + o4[2 * c:3 * c] + o4[3 * c:4 * c] + _dot(q, st) * q_dec
            st_ref[d, b] = st * c_dec + _dot_tn(k * k_dec, v) * bd64_f


def _retention(zr, lg_rows, lg_lane, head_mask, bd64, geo, layer):
    nb, s, _ = zr.shape
    c = RET_CHUNK_LEN
    fwd, bwd, steps = _scan_index_maps(geo, c)
    return pl.pallas_call(
        _ret_kernel,
        grid=(steps,),
        in_specs=[_resident((2, N_HEADS * c, c), layer), _resident((2, BRANCH_W), layer),
                  _resident((N_HEADS * c, BRANCH_W)),
                  _resident((256, 256)),
                  pl.BlockSpec((nb, c, 1024), fwd), pl.BlockSpec((nb, c, 1024), bwd)],
        out_specs=[pl.BlockSpec((nb, c, BRANCH_W), fwd), pl.BlockSpec((nb, c, BRANCH_W), bwd)],
        out_shape=[jax.ShapeDtypeStruct((nb, s, BRANCH_W), F32)] * 2,
        scratch_shapes=[pltpu.VMEM((2, nb, BRANCH_W, BRANCH_W), F32), pltpu.VMEM((2, N_HEADS * c, c), F32)],
        compiler_params=_cparams(("arbitrary",)),
        name="retention",
    )(lg_rows, lg_lane, head_mask, bd64, zr, zr)


GLA_FACTOR_MIN_LOG_DECAY = -40.0
GLA_FACTOR_MAX_KEY = 1e15


def _gla_prepare(z_ref, d, gu_ref, gb_ref, tri_ref):
    nb, c, _ = z_ref.shape
    m = nb * c
    q = z_ref[:, :, 0:256].reshape(m, BRANCH_W)
    k = z_ref[:, :, 256:512].reshape(m, BRANCH_W)
    v = z_ref[:, :, 512:768].reshape(m, BRANCH_W)
    logit = _dot(z_ref[:, :, 1024:GLA_GROUP_W].reshape(m, LANES), gu_ref[d]) + gb_ref[d:d + 1, :]
    g = (jnp.minimum(logit, 0.0) - jnp.log(1.0 + jnp.exp(-jnp.abs(logit)))) * GLA_TAU_INV
    g1 = g.astype(BF16)
    r1 = g - g1.astype(F32)
    g2 = r1.astype(BF16)
    g3 = (r1 - g2.astype(F32)).astype(BF16)
    tri = tri_ref[d]
    bcum = jnp.concatenate(
        [sum(jnp.dot(tri, t[b * c:(b + 1) * c], preferred_element_type=F32) for t in (g1, g2, g3))
         for b in range(nb)], axis=0)
    return q, k, v, bcum


def _gla_state_step(st_ref, d, b, q_state_b, kb, vb, bb, rev, bd64_f):
    c = kb.shape[0]
    b_last = bb[0:1, :] if rev else bb[c - 1:c, :]
    st = st_ref[d, b]
    o_b = _dot_nt(q_state_b, st)
    st_ref[d, b] = st * jnp.exp(b_last) + _dot_tn(vb, kb * jnp.exp(b_last - bb)) * bd64_f
    return o_b


def _gla_factored(d, rev, q, k, v, bcum, o_ref, st_ref, hm64_ref, causal_ref, bd64_f):
    nb, c, _ = o_ref.shape
    q_state = q * jnp.exp(bcum)
    k_inv = k * jnp.exp(-bcum)
    hm = hm64_ref[...]
    causal = causal_ref[d]
    outs = []
    for b in range(nb):
        rows = slice(b * c, (b + 1) * c)
        qb, vb = q_state[rows], v[rows]
        q4 = jnp.concatenate([qb] * N_HEADS, axis=0) * hm
        att = jnp.where(causal > 0.0, _dot_nt(q4, k_inv[rows]), 0.0)
        o4 = _dot(att, vb) * hm
        o_b = o4[0:c] + o4[c:2 * c] + o4[2 * c:3 * c] + o4[3 * c:4 * c]
        outs.append(o_b + _gla_state_step(st_ref, d, b, qb, k[rows], vb, bcum[rows], rev, bd64_f))
    o_ref[...] = jnp.concatenate(outs, axis=0).reshape(nb, c, BRANCH_W)


def _gla_pairwise(d, rev, q, k, v, bcum, o_ref, st_ref, w_ref, hm16_ref, bd64, bd64_f):
    nb, c, _ = o_ref.shape
    sub = GLA_SUB
    nsub = c // sub
    m = nb * c
    head_mask = hm16_ref[...]
    rin = lax.broadcasted_iota(jnp.int32, (m, BRANCH_W), 0) & (sub - 1)
    for dl in range(sub):
        if dl == 0:
            w = q * k
        else:
            shift = (m - dl) if rev else dl
            ks = pltpu.roll(k, shift, 0)
            bs = pltpu.roll(bcum, shift, 0)
            valid = (rin + dl <= sub - 1) if rev else (rin >= dl)
            w = jnp.where(valid, q * ks * jnp.exp(jnp.minimum(bcum - bs, 0.0)), 0.0)
        w_ref[dl * m:(dl + 1) * m, :] = w.astype(BF16)
    att = jnp.dot(w_ref[...], bd64, preferred_element_type=F32)
    o_acc = att[0:m] * v
    for dl in range(1, sub):
        o_acc = o_acc + att[dl * m:(dl + 1) * m] * pltpu.roll(v, (m - dl) if rev else dl, 0)

    q_state = q * jnp.exp(bcum)
    outs = []
    for b in range(nb):
        r0 = b * c
        qb, kb, vb, bb = q[r0:r0 + c], k[r0:r0 + c], v[r0:r0 + c], bcum[r0:r0 + c]
        rows = []
        for blk in range(nsub):
            a0, a1 = blk * sub, (blk + 1) * sub
            if rev and blk < nsub - 1:
                anchor, lo, hi = bb[a1:a1 + 1, :], a1, c
            elif (not rev) and blk > 0:
                anchor, lo, hi = bb[a0 - 1:a0, :], 0, a0
            else:
                rows.append(jnp.zeros((sub, BRANCH_W), F32))
                continue
            qa = qb[a0:a1] * jnp.exp(bb[a0:a1] - anchor)
            ka = kb[lo:hi] * jnp.exp(anchor - bb[lo:hi])
            q4 = jnp.concatenate([qa] * N_HEADS, axis=0) * head_mask
            o4 = _dot(_dot_nt(q4, ka), vb[lo:hi]) * head_mask
            rows.append(o4[0:sub] + o4[sub:2 * sub] + o4[2 * sub:3 * sub] + o4[3 * sub:4 * sub])
        o_b = _gla_state_step(st_ref, d, b, q_state[r0:r0 + c], kb, vb, bb, rev, bd64_f)
        outs.append(o_b + jnp.concatenate(rows, axis=0))
    o_ref[...] = (o_acc + jnp.concatenate(outs, axis=0)).reshape(nb, c, BRANCH_W)


def _gla_kernel(gu_ref, gb_ref, tri_ref, bd64_ref, hm16_ref, hm64_ref, causal_ref, zf_ref, zb_ref,
                of_ref, ob_ref, st_ref, w_ref):
    @pl.when(pl.program_id(0) == 0)
    def _():
        st_ref[...] = jnp.zeros_like(st_ref)

    bd64 = bd64_ref[...]
    bd64_f = bd64.astype(F32)
    fwd = _gla_prepare(zf_ref, 0, gu_ref, gb_ref, tri_ref)
    bwd = _gla_prepare(zb_ref, 1, gu_ref, gb_ref, tri_ref)
    min_decay = jnp.minimum(jnp.min(fwd[3]), jnp.min(bwd[3]))
    max_key = jnp.maximum(jnp.max(jnp.abs(fwd[1])), jnp.max(jnp.abs(bwd[1])))
    factor_ok = jnp.logical_and(min_decay > GLA_FACTOR_MIN_LOG_DECAY, max_key < GLA_FACTOR_MAX_KEY)

    @pl.when(factor_ok)
    def _():
        _gla_factored(0, False, *fwd, of_ref, st_ref, hm64_ref, causal_ref, bd64_f)
        _gla_factored(1, True, *bwd, ob_ref, st_ref, hm64_ref, causal_ref, bd64_f)

    @pl.when(jnp.logical_not(factor_ok))
    def _():
        _gla_pairwise(0, False, *fwd, of_ref, st_ref, w_ref, hm16_ref, bd64, bd64_f)
        _gla_pairwise(1, True, *bwd, ob_ref, st_ref, w_ref, hm16_ref, bd64, bd64_f)


def _gla(zg, gate_up_pad, gate_b, tri, bd64, hm16, hm64, causal, geo, layer):
    nb, s, _ = zg.shape
    c = GLA_CHUNK_LEN
    fwd, bwd, steps = _scan_index_maps(geo, c)
    return pl.pallas_call(
        _gla_kernel,
        grid=(steps,),
        in_specs=[_resident((2, LANES, BRANCH_W), layer), _resident((2, BRANCH_W), layer),
                  _resident((2, c, c)), _resident((256, 256)), _resident((N_HEADS * GLA_SUB, BRANCH_W)),
                  _resident((N_HEADS * c, BRANCH_W)), _resident((2, N_HEADS * c, c)),
                  pl.BlockSpec((nb, c, GLA_GROUP_W), fwd), pl.BlockSpec((nb, c, GLA_GROUP_W), bwd)],
        out_specs=[pl.BlockSpec((nb, c, BRANCH_W), fwd), pl.BlockSpec((nb, c, BRANCH_W), bwd)],
        out_shape=[jax.ShapeDtypeStruct((nb, s, BRANCH_W), F32)] * 2,
        scratch_shapes=[pltpu.VMEM((2, nb, BRANCH_W, BRANCH_W), F32),
                        pltpu.VMEM((GLA_SUB * nb * c, BRANCH_W), BF16)],
        compiler_params=_cparams(("arbitrary",)),
        name="gla",
    )(gate_up_pad, gate_b, tri, bd64, hm16, hm64, causal, zg, zg)


SOFTMAX_MIN_SUM = 2.0 ** -80


def _softmax_pv(qk_pairs, v, s_ref, p_ref, slot):
    tq = qk_pairs[0][0].shape[0]
    n_keys = v.shape[0]
    rows = len(qk_pairs) * tq
    for i, (q, k) in enumerate(qk_pairs):
        s_ref[i * tq:(i + 1) * tq, 0:n_keys] = _dot_nt(q, k)
    sums = []
    for r0 in range(0, rows, SOFTMAX_ROWS):
        s = s_ref[r0:r0 + SOFTMAX_ROWS, 0:n_keys]
        p = jnp.exp2(s - jnp.max(s, axis=-1, keepdims=True))
        sums.append(jnp.sum(p, axis=-1, keepdims=True))
        p_ref[slot, r0:r0 + SOFTMAX_ROWS, 0:n_keys] = p.astype(BF16)
    l = jnp.concatenate(sums, axis=0)
    o = jnp.dot(p_ref[slot, 0:rows, 0:n_keys], v, preferred_element_type=F32) / l
    return [o[i * tq:(i + 1) * tq] for i in range(len(qk_pairs))]


def _softmax_pv_bounded(qk_pairs, v, bounds, p_ref, slot, shared_keys, mix=None):
    tq = qk_pairs[0][0].shape[0]
    n_keys = v.shape[0]
    rows = len(qk_pairs) * tq
    if shared_keys:
        q_all = jnp.concatenate([q for q, _ in qk_pairs], axis=0)
        p = jnp.exp2(_dot_nt(q_all, qk_pairs[0][1]) - jnp.concatenate(bounds, axis=0))
        l = jnp.sum(p, axis=-1, keepdims=True)
        p_ref[slot, 0:rows, 0:n_keys] = p.astype(BF16)
    else:
        sums = []
        for i, (q, k) in enumerate(qk_pairs):
            p = jnp.exp2(_dot_nt(q, k) - bounds[i])
            sums.append(jnp.sum(p, axis=-1, keepdims=True))
            p_ref[slot, i * tq:(i + 1) * tq, 0:n_keys] = p.astype(BF16)
        l = jnp.concatenate(sums, axis=0)
        if mix is not None:
            l0, l1 = sums
            weight = (mix * l0 / l1).astype(BF16)
            p_ref[slot, 0:tq, 0:n_keys] = (p_ref[slot, 0:tq, 0:n_keys]
                                           - weight * p_ref[slot, tq:2 * tq, 0:n_keys])
            o = jnp.dot(p_ref[slot, 0:tq, 0:n_keys], v, preferred_element_type=F32) / l0
            return [o], jnp.min(l)
    o = jnp.dot(p_ref[slot, 0:rows, 0:n_keys], v, preferred_element_type=F32) / l
    return [o[i * tq:(i + 1) * tq] for i in range(len(qk_pairs))], jnp.min(l)


def _score_bounds(q_ref, key_max, bd):
    a = jnp.abs(q_ref[...].astype(F32)) * key_max
    return jnp.dot(a.astype(BF16), bd, preferred_element_type=F32)


def _attention_body(units, finish, q_ref, k_ref, bound, o_ref, s_ref, p_ref, mixes=None):
    def run(bounded):
        outs, l_min = [], None
        for u, (v, pairs) in enumerate(units):
            qk = [(q_ref[:, qs], k_ref[:, ks]) for qs, ks, _ in pairs]
            if bounded:
                shared = all(ks == pairs[0][1] for _, ks, _ in pairs)
                o, lm = _softmax_pv_bounded(qk, v, [bound[:, c:c + 1] for _, _, c in pairs], p_ref, u % 2, shared,
                                            None if mixes is None else mixes[u])
                l_min = lm if l_min is None else jnp.minimum(l_min, lm)
            else:
                o = _softmax_pv(qk, v, s_ref, p_ref, u % 2)
            outs.extend(o)
        o_ref[...] = finish(outs, bounded and mixes is not None).astype(o_ref.dtype)
        return l_min

    l_min = run(True)
    pl.when(jnp.logical_not(l_min >= SOFTMAX_MIN_SUM))(lambda: run(False))


def _key_max(k_ref, kmax_ref):
    @pl.when(pl.program_id(1) == 0)
    def _():
        kmax_ref[...] = jnp.max(jnp.abs(k_ref[...].astype(F32)), axis=0, keepdims=True)
    return kmax_ref[...]


def _gqa_kernel(bd64_ref, q_ref, k_ref, v_ref, *rest):
    o_ref, s_ref, p_ref, kmax_ref = rest[-4:]
    km = _key_max(k_ref, kmax_ref)
    km = jnp.concatenate([km[:, 0:HEAD_W]] * 2 + [km[:, HEAD_W:2 * HEAD_W]] * 2, axis=1)
    bound = _score_bounds(q_ref, km, bd64_ref[...])
    units = []
    for kv in range(2):
        cs = slice(kv * HEAD_W, (kv + 1) * HEAD_W)
        pairs = [(slice(h * HEAD_W, (h + 1) * HEAD_W), cs, h * HEAD_W) for h in (2 * kv, 2 * kv + 1)]
        units.append((v_ref[:, cs], pairs))
    _attention_body(units, lambda outs, _: jnp.concatenate(outs, axis=1), q_ref, k_ref, bound, o_ref, s_ref, p_ref)


def _dif_kernel(lam_ref, sub_ref, bd64_ref, bd32_ref, q_ref, k_ref, v_ref, *rest, out_scale):
    o_ref, s_ref, p_ref, kmax_ref = rest[-4:]
    bound = _score_bounds(q_ref, _key_max(k_ref, kmax_ref), bd32_ref[...])
    units = []
    for h in range(N_HEADS):
        pairs = []
        for mp in range(2):
            qs = slice((2 * h + mp) * DIF_QK, (2 * h + mp + 1) * DIF_QK)
            pairs.append((qs, qs, (2 * h + mp) * DIF_QK))
        units.append((v_ref[:, h * HEAD_W:(h + 1) * HEAD_W], pairs))

    def finish(outs, merged):
        heads = outs if merged else [outs[2 * h] - lam_ref[:, h * HEAD_W:(h + 1) * HEAD_W] * outs[2 * h + 1]
                                     for h in range(N_HEADS)]
        return _group_rms(jnp.concatenate(heads, axis=1), bd64_ref[...], sub_ref[...], HEAD_W) * out_scale

    _attention_body(units, finish, q_ref, k_ref, bound, o_ref, s_ref, p_ref, mixes=[lam_ref[:, 0:1]] * N_HEADS)


def _attention(body, name, pre_specs, pre_args, q, k, v, geo, need_ctx):
    nb, t, n_ctx, tq = geo["batch"], geo["seq"], geo["ctx"], geo["tq"]
    s = t + n_ctx
    kw, vw = k.shape[2], v.shape[2]
    keys = lambda n: pl.BlockSpec((None, n, kw), lambda b, i: (b, 0 if n == s else t // n_ctx, 0),
                                  pipeline_mode=pl.Buffered(1))
    vals = lambda n: pl.BlockSpec((None, n, vw), lambda b, i: (b, 0 if n == s else t // n_ctx, 0),
                                  pipeline_mode=pl.Buffered(1))

    def call(n_keys, q_tiles, q_off, prev):
        qmap = lambda b, i: (b, q_off + i, 0)
        alias = {} if prev is None else {len(pre_args) + 3: 0}
        extra = [] if prev is None else [pl.BlockSpec(memory_space=pl.ANY)]
        return pl.pallas_call(
            body, grid=(nb, q_tiles),
            in_specs=pre_specs + [pl.BlockSpec((None, tq, BRANCH_W), qmap), keys(n_keys), vals(n_keys)] + extra,
            out_specs=pl.BlockSpec((None, tq, BRANCH_W), qmap),
            out_shape=jax.ShapeDtypeStruct(q.shape, BF16),
            scratch_shapes=[pltpu.VMEM((2 * tq, n_keys), F32), pltpu.VMEM((2, 2 * tq, n_keys), BF16),
                            pltpu.VMEM((1, kw), F32)],
            input_output_aliases=alias,
            compiler_params=_cparams(("parallel", "arbitrary")),
            name=name,
        )(*pre_args, q, k, v, *([] if prev is None else [prev]))

    out = call(s, t // tq, 0, None)
    if need_ctx:
        out = call(n_ctx, n_ctx // tq, t // tq, out)
    return out


def _gqa(cq, ck, cv, bd64, geo, need_ctx):
    return _attention(_gqa_kernel, "gqa", [_resident((256, 256))], [bd64], cq, ck, cv, geo, need_ctx)


def _dif(dq, dk, dv, lam_lane, subln, bd64, bd32, geo, need_ctx, out_scale, layer):
    pre = [_resident((1, BRANCH_W), layer), _resident((1, BRANCH_W), layer), _resident((256, 256)),
           _resident((256, 256))]
    return _attention(functools.partial(_dif_kernel, out_scale=out_scale), "diffattn", pre,
                      [lam_lane, subln, bd64, bd32], dq, dk, dv, geo, need_ctx)


def _merge_kernel(x_ref, mod_ref, n1_ref, wg_ref, wb_ref, wo_ref, bd64_ref, gnw_ref,
                  rof_ref, rob_ref, rg_ref, gof_ref, gob_ref, gr_ref, yc_ref, yd_ref, *rest):
    o_ref = rest[-1]
    nb, rt, d = x_ref.shape
    m = nb * rt
    bd64 = bd64_ref[...]
    for g0, g1 in _sample_groups(nb):
        ng = g1 - g0
        m = ng * rt
        x = x_ref[g0:g1]
        hb = _adaln(x, n1_ref[...], mod_ref[g0:g1, 0:1, :], mod_ref[g0:g1, 1:2, :]).astype(BF16).reshape(m, d)
        flat = lambda ref: ref[g0:g1].reshape(m, BRANCH_W)
        ya = _group_rms(flat(rof_ref) + flat(rob_ref), bd64, None, HEAD_W) * _silu(flat(rg_ref))
        yb = _group_rms(flat(gof_ref) + flat(gob_ref), bd64, gnw_ref[...], HEAD_W) * _silu(flat(gr_ref))
        ys = (ya.astype(BF16), yb.astype(BF16), flat(yc_ref), flat(yd_ref))
        mix = None
        for i in range(4):
            gate = _sigmoid(jnp.dot(hb, wg_ref[:, PROJ_W + i * d:PROJ_W + (i + 1) * d], preferred_element_type=F32))
            term = gate * jnp.dot(ys[i], wb_ref[i], preferred_element_type=F32)
            mix = term if mix is None else mix + term
        o_ref[g0:g1] = x + mod_ref[g0:g1, 2:3, :] * _dot(mix, wo_ref[...]).reshape(ng, rt, d)


def _merge(sources, mod, norm1, w_all, w_branch, w_out, bd64, gla_norm, rof, rob, zr, gof, gob, zg, yc, yd, geo,
           need_ctx, layer):
    nb, _, d = sources[0][0].shape
    s = geo["seq"] + geo["ctx"]
    rt = geo["rt_merge"]
    bw = BRANCH_W
    n_in = 16
    out = None
    for src, off, n_tiles in sources:
        joint = src.shape[1] == s
        if joint and not need_ctx:
            n_tiles = geo["seq"] // rt
        row = lambda i, off=off: (0, off + i, 0)
        col3 = lambda i, off=off: (0, off + i, 3)
        act = lambda imap: pl.BlockSpec((nb, rt, bw), imap)
        prev = [] if out is None else [out]
        out = pl.pallas_call(
            _merge_kernel,
            grid=(n_tiles,),
            in_specs=[pl.BlockSpec((nb, rt, d), lambda i: (0, i, 0)),
                      _mod_spec(nb, d, geo["seq"] // rt, layer, off),
                      _resident((1, d), layer), _resident((d, w_all.shape[2]), layer),
                      _resident((4, bw, d), layer), _resident((d, d), layer), _resident((256, 256)),
                      _resident((1, bw), layer),
                      act(row), act(row), act(col3), act(row), act(row), act(col3), act(row), act(row)]
                     + [pl.BlockSpec(memory_space=pl.ANY)] * len(prev),
            out_specs=pl.BlockSpec((nb, rt, d), row),
            out_shape=jax.ShapeDtypeStruct((nb, s, d), F32),
            input_output_aliases={0: 0} if joint else ({n_in: 0} if prev else {}),
            compiler_params=_cparams(("parallel",)),
            name="merge",
        )(src, mod, norm1, w_all, w_branch, w_out, bd64, gla_norm, rof, rob, zr, gof, gob, zg, yc, yd, *prev)
    return out


FFN_CHUNKS = ((0, 1024), (1024, 2048), (2048, D_FF_W))


def _ffn_kernel(x_ref, xp_ref, xn_ref, mod_ref, n2_ref, wu_ref, cw_ref, wd_ref, o_ref, *, seq, total):
    nb, rt, d = x_ref.shape
    ext = rt + 2 * HALO
    first = pl.program_id(0) * rt
    keep_prev = jnp.where(jnp.logical_or(first == 0, first == seq), 0.0, 1.0)
    keep_next = jnp.where(jnp.logical_or(first + rt == seq, first + rt == total), 0.0, 1.0)
    for g0, g1 in _sample_groups(nb):
        ng = g1 - g0

        def hn(ref):
            return _adaln(ref[g0:g1], n2_ref[...], mod_ref[g0:g1, 3:4, :], mod_ref[g0:g1, 4:5, :])
        h_mid = hn(x_ref).astype(BF16)
        h_ext = jnp.concatenate([(hn(xp_ref) * keep_prev).astype(BF16), h_mid,
                                 (hn(xn_ref) * keep_next).astype(BF16)], axis=1).reshape(ng * ext, d)
        h_mid = h_mid.reshape(ng * rt, d)
        acc = None
        for f0, f1 in FFN_CHUNKS:
            tf = f1 - f0
            u = jnp.dot(h_ext, wu_ref[:, f0:f1], preferred_element_type=F32)
            centre = lambda a: a.reshape(ng, ext, tf)[:, HALO:HALO + rt, :]
            up = centre(pltpu.roll(u, 1, 0))
            mid = centre(u)
            dn = centre(pltpu.roll(u, ng * ext - 1, 0))
            a = (up * cw_ref[0:1, f0:f1] + mid * cw_ref[1:2, f0:f1] + dn * cw_ref[2:3, f0:f1]
                 + cw_ref[3:4, f0:f1])
            gate = jnp.dot(h_mid, wu_ref[:, D_FF_W + f0:D_FF_W + f1], preferred_element_type=F32)
            part = _dot((_silu(a) * gate.reshape(ng, rt, tf)).reshape(ng * rt, tf), wd_ref[f0:f1, :])
            acc = part if acc is None else acc + part
        o_ref[g0:g1] = x_ref[g0:g1] + mod_ref[g0:g1, 5:6, :] * acc.reshape(ng, rt, d)


def _ffn(xs, mod, norm2, w_up, conv_pack, w_down, geo, need_ctx, layer):
    nb, s, d = xs.shape
    rt, t = geo["rt_ffn"], geo["seq"]
    s_out = s if need_ctx else t
    hb = rt // HALO
    last_blk = s // HALO - 1
    row = lambda i: (0, i, 0)
    return pl.pallas_call(
        functools.partial(_ffn_kernel, seq=t, total=s),
        grid=(s_out // rt,),
        in_specs=[pl.BlockSpec((nb, rt, d), row),
                  pl.BlockSpec((nb, HALO, d), lambda i: (0, jnp.maximum(i * hb - 1, 0), 0)),
                  pl.BlockSpec((nb, HALO, d), lambda i: (0, jnp.minimum((i + 1) * hb, last_blk), 0)),
                  _mod_spec(nb, d, t // rt, layer),
                  _resident((1, d), layer),
                  _resident((d, 2 * D_FF_W), layer),
                  _resident((8, D_FF_W), layer),
                  _resident((D_FF_W, d), layer)],
        out_specs=pl.BlockSpec((nb, rt, d), row),
        out_shape=jax.ShapeDtypeStruct((nb, s_out, d), F32),
        compiler_params=_cparams(("parallel",)),
        name="convffn",
    )(xs, xs, xs, mod, norm2, w_up, conv_pack, w_down)


def _rope_tables(t, n_ctx):
    rows = t // GRID_WIDTH

    def axial(dim):
        axis_dim = dim // 2
        freqs = ROPE_BASE ** (-np.arange(0, axis_dim, 2, dtype=np.float64) / axis_dim)
        row = np.repeat(np.arange(rows, dtype=np.float64), GRID_WIDTH)
        col = np.tile(np.arange(GRID_WIDTH, dtype=np.float64), rows)
        ang = np.concatenate([row[:, None] * freqs, col[:, None] * freqs], axis=-1)
        return np.cos(ang), np.sin(ang)

    freqs = ROPE_BASE ** (-np.linspace(0.0, 1.0, HEAD_W // 2))
    ang = np.arange(t, dtype=np.float64)[:, None] * freqs
    parts = []
    for cos, sin in ((np.cos(ang), np.sin(ang)), axial(HEAD_W), axial(DIF_QK)):
        reps = LANES // (2 * cos.shape[1])
        parts.append(np.tile(cos, (1, 2 * reps)))
        parts.append(np.tile(np.concatenate([-sin, sin], axis=1), (1, reps)))
    table = np.concatenate(parts, axis=1)
    ident = np.tile(np.concatenate([np.ones((1, LANES)), np.zeros((1, LANES))], axis=1), (n_ctx, 3))
    return jnp.asarray(np.concatenate([table, ident], axis=0), dtype=F32)


def _block_diag_ones(width, group):
    idx = np.arange(width) // group
    return jnp.asarray(idx[:, None] == idx[None, :], dtype=BF16)


def _row_block_lane_mask(block, n_lanes, lane_group):
    n_groups = n_lanes // lane_group
    rows = np.arange(n_groups * block)[:, None] // block
    return jnp.asarray(rows == np.arange(n_lanes)[None, :] // lane_group, dtype=F32)


def kernel(x, c, ctx, c_ctx, w_mod, b_mod, norm1, norm2, w_in, ret_decay, gla_gate_up, gla_gate_b, gla_norm,
           gqa_qnorm, gqa_knorm, dif_qnorm, dif_knorm, dif_lambda, dif_subln, w_branch, w_out, w_up, conv_w,
           conv_b, w_down):
    nb, t, d = x.shape
    n_ctx = ctx.shape[1]
    depth = w_mod.shape[0]
    assert t % 256 == 0 and n_ctx % 256 == 0 and nb + 1 <= 8
    geo = dict(batch=nb, seq=t, ctx=n_ctx, rt=256, rt_merge=128, rt_ffn=256, tq=256)

    xs = None
    cc = jnp.zeros((8, d), F32).at[0:nb].set(c).at[nb].set(c_ctx)
    mod_all = _modulation(cc, w_mod, b_mod).reshape(depth, 8, 6, d)

    table = _rope_tables(t, n_ctx)
    bd64 = _block_diag_ones(256, HEAD_W)
    bd32 = _block_diag_ones(256, DIF_QK)
    ti = np.arange(GLA_CHUNK_LEN)
    tri = jnp.asarray(np.stack([ti[:, None] >= ti[None, :], ti[:, None] <= ti[None, :]]), dtype=BF16)
    hm16 = _row_block_lane_mask(GLA_SUB, BRANCH_W, HEAD_W)
    hm64 = _row_block_lane_mask(GLA_CHUNK_LEN, BRANCH_W, HEAD_W)
    hm128 = _row_block_lane_mask(RET_CHUNK_LEN, BRANCH_W, HEAD_W)
    qi = np.arange(N_HEADS * GLA_CHUNK_LEN)[:, None] % GLA_CHUNK_LEN
    kj = np.arange(GLA_CHUNK_LEN)[None, :]
    causal = jnp.asarray(np.stack([kj <= qi, kj >= qi]), dtype=F32)

    w_all = _repack_w_in(w_in)
    w_branch_b, w_out_b, w_up_b, w_down_b = (a.astype(BF16) for a in (w_branch, w_out, w_up, w_down))
    qk_norms = jnp.stack([jnp.tile(gqa_qnorm, (1, 4)), jnp.tile(gqa_knorm, (1, 4)), jnp.tile(dif_qnorm, (1, 8)),
                          jnp.tile(dif_knorm, (1, 8))] + [jnp.zeros((depth, 256), F32)] * 4, axis=1)
    mod = jnp.stack([mod_all[:, 0:nb], jnp.broadcast_to(mod_all[:, nb:nb + 1], (depth, nb, 6, d))], axis=1)
    n1 = norm1[:, None, :]
    n2 = norm2[:, None, :]
    log_gamma = jax.nn.log_sigmoid(ret_decay.astype(F32))
    lg_rows = jnp.broadcast_to(jnp.repeat(log_gamma, RET_CHUNK_LEN, axis=2)[..., None],
                               (depth, 2, N_HEADS * RET_CHUNK_LEN, RET_CHUNK_LEN))
    lg_lane = jnp.repeat(log_gamma, HEAD_W, axis=2)
    pad_rows = lambda a, lo: jnp.pad(a, ((0, 0), (lo, LANES - GLA_RANK_W - lo), (0, 0)))
    gate_up = jnp.stack([pad_rows(gla_gate_up[:, 0], 0), pad_rows(gla_gate_up[:, 1], GLA_RANK_W)],
                        axis=1).astype(BF16)
    lam_init = [0.8 - 0.6 * math.exp(-0.3 * l) for l in range(depth)]
    lv = dif_lambda.astype(F32)
    lam = (jnp.exp(jnp.sum(lv[:, 0] * lv[:, 1], axis=-1)) - jnp.exp(jnp.sum(lv[:, 2] * lv[:, 3], axis=-1))
           + jnp.asarray(lam_init, F32))
    lam_lane = jnp.broadcast_to(lam[:, None, None], (depth, 1, BRANCH_W))
    subln = jnp.tile(dif_subln, (1, 4))[:, None, :]
    gnw = jnp.tile(gla_norm, (1, 4))[:, None, :]
    conv_pack = jnp.concatenate([conv_w, conv_b[:, None, :], jnp.zeros((depth, 4, D_FF_W), F32)], axis=1)

    for l in range(depth):
        need_ctx = l < depth - 1
        zr, zg, cq, ck, cv, dq, dk, dv = _inproj(_token_sources(x, ctx, xs, geo["rt"]), mod, n1, w_all, table,
                                                 bd64, bd32, qk_norms, geo, l)
        rof, rob = _retention(zr, lg_rows, lg_lane, hm128, bd64, geo, l)
        gof, gob = _gla(zg, gate_up, gla_gate_b, tri, bd64, hm16, hm64, causal, geo, l)
        yc = _gqa(cq, ck, cv, bd64, geo, need_ctx)
        yd = _dif(dq, dk, dv, lam_lane, subln, bd64, bd32, geo, need_ctx, 1.0 - lam_init[l], l)
        xs = _merge(_token_sources(x, ctx, xs, geo["rt_merge"]), mod, n1, w_all, w_branch_b, w_out_b, bd64, gnw, rof, rob, zr, gof, gob, zg, yc, yd,
                    geo, need_ctx, l)
        xs = _ffn(xs, mod, n2, w_up_b, conv_pack, w_down_b, geo, need_ctx, l)

    return xs
```

```python
import functools
import math

import jax
import jax.numpy as jnp
import numpy as np
from jax import lax
from jax.experimental import pallas as pl
from jax.experimental.pallas import tpu as pltpu

F32 = jnp.float32
BF16 = jnp.bfloat16

N_HEADS = 4
HEAD_W = 64
BRANCH_W = N_HEADS * HEAD_W
DIF_QK = 32
RET_CHUNK_LEN = 128
GLA_CHUNK_LEN = 128
GLA_SUB = 16
GLA_RANK_W = 16
GLA_TAU_INV = 1.0 / 16.0
GRID_WIDTH = 64
ROPE_BASE = 10000.0
EPS = 1e-6
D_FF_W = 2816
LOG2E = 1.4426950408889634

LANES = 128
VMEM_LIMIT_BYTES = 56 * 1024 * 1024

RET_OFF, GLA_OFF, CODES_OFF, GQA_OFF, DIF_OFF, GATES_OFF = 0, 1024, 2048, 2080, 2592, 3360
GLA_GROUP_W = 1024 + LANES
HALO = 16
SOFTMAX_ROWS = 128


def _cparams(sem):
    return pltpu.CompilerParams(dimension_semantics=sem, vmem_limit_bytes=VMEM_LIMIT_BYTES)


def _resident(shape, layer=None):
    nd = len(shape)
    if layer is None:
        return pl.BlockSpec(shape, lambda *_: (0,) * nd, pipeline_mode=pl.Buffered(1))
    return pl.BlockSpec((None,) + tuple(shape), lambda *_: (layer,) + (0,) * nd, pipeline_mode=pl.Buffered(1))


def _sample_groups(nb):
    half = nb // 2
    return [(0, nb)] if nb % 2 else [(0, half), (half, nb)]


def _dot(a, b):
    return jnp.dot(a.astype(BF16), b.astype(BF16), preferred_element_type=F32)


def _dot_nt(a, b):
    return lax.dot_general(a.astype(BF16), b.astype(BF16), (((1,), (1,)), ((), ())),
                           preferred_element_type=F32)


def _dot_tn(a, b):
    return lax.dot_general(a.astype(BF16), b.astype(BF16), (((0,), (0,)), ((), ())),
                           preferred_element_type=F32)


def _sigmoid(x):
    return 0.5 * jnp.tanh(0.5 * x) + 0.5


def _silu(x):
    return x * _sigmoid(x)


def _adaln(x, norm_w, shift, scale):
    h = x * lax.rsqrt(jnp.mean(x * x, axis=-1, keepdims=True) + EPS) * norm_w
    return h * (1.0 + scale) + shift


def _group_rms(x, bd, weight, group):
    ss = jnp.dot((x * x).astype(BF16), bd, preferred_element_type=F32)
    y = x * lax.rsqrt(ss * (1.0 / group) + EPS)
    return y if weight is None else y * weight


def _rope(x, cos, sin_signed, half):
    w = x.shape[1]
    reps = w // LANES
    if reps > 1:
        cos = jnp.concatenate([cos] * reps, axis=1)
        sin_signed = jnp.concatenate([sin_signed] * reps, axis=1)
    lane = lax.broadcasted_iota(jnp.int32, x.shape, 1)
    lower = (lane & (2 * half - 1)) < half
    partner = jnp.where(lower, pltpu.roll(x, w - half, 1), pltpu.roll(x, half, 1))
    return x * cos + partner * sin_signed


def _mod_kernel(c_ref, w_ref, b_ref, o_ref):
    o_ref[...] = _dot(_silu(c_ref[...]), w_ref[...]) + b_ref[...]


def _modulation(cc, w_mod, b_mod):
    n_layers, d, width = w_mod.shape
    tn = 1024
    return pl.pallas_call(
        _mod_kernel,
        grid=(n_layers, width // tn),
        in_specs=[pl.BlockSpec((8, d), lambda l, j: (0, 0)),
                  pl.BlockSpec((None, d, tn), lambda l, j: (l, 0, j)),
                  pl.BlockSpec((None, 1, tn), lambda l, j: (l, 0, j))],
        out_specs=pl.BlockSpec((None, 8, tn), lambda l, j: (l, 0, j)),
        out_shape=jax.ShapeDtypeStruct((n_layers, 8, width), F32),
        compiler_params=_cparams(("parallel", "parallel")),
        name="modulation",
    )(cc, w_mod, b_mod.reshape(n_layers, 1, width))


def _mod_spec(nb, d, n_lat_tiles, layer, tile_off=0):
    return pl.BlockSpec((None, None, nb, 6, d),
                        lambda i, *_: (layer, jnp.where(tile_off + i < n_lat_tiles, 0, 1), 0, 0, 0))


def _token_sources(x, ctx, xs, rt):
    if xs is not None:
        return [(xs, 0, xs.shape[1] // rt)]
    return [(x, 0, x.shape[1] // rt), (ctx, x.shape[1] // rt, ctx.shape[1] // rt)]


def _inproj_kernel(x_ref, mod_ref, n1_ref, w_ref, tab_ref, bd64_ref, bd32_ref, nw_ref, *rest):
    zr_ref, zg_ref, cq_ref, ck_ref, cv_ref, dq_ref, dk_ref, dv_ref = rest[-8:]
    nb, rt, d = x_ref.shape
    m = nb * rt
    hb = _adaln(x_ref[...], n1_ref[...], mod_ref[:, 0:1, :], mod_ref[:, 1:2, :]).astype(BF16).reshape(m, d)

    def table(lo):
        t = tab_ref[:, lo:lo + LANES]
        return jnp.concatenate([t] * nb, axis=0)

    r_cos, r_sin, g_cos, g_sin, d_cos, d_sin = (table(i * LANES) for i in range(6))

    def put(ref, lo, val):
        ref[:, :, lo:lo + val.shape[1]] = val.reshape(nb, rt, val.shape[1]).astype(ref.dtype)

    zr = _dot_nt(hb, w_ref[RET_OFF:GLA_OFF, :])
    put(zr_ref, 0, _rope(zr[:, 0:256], r_cos, r_sin, 32))
    put(zr_ref, 256, _rope(zr[:, 256:512], r_cos, r_sin, 32) * (HEAD_W ** -0.5))
    put(zr_ref, 512, zr[:, 512:1024])

    zg = _dot_nt(hb, w_ref[GLA_OFF:CODES_OFF, :])
    put(zg_ref, 0, zg[:, 0:256] * (HEAD_W ** -0.5))
    put(zg_ref, 256, zg[:, 256:1024])
    codes = _dot_nt(hb, w_ref[CODES_OFF:GQA_OFF, :])
    put(zg_ref, 1024, jnp.concatenate([codes, jnp.zeros((m, LANES - 2 * GLA_RANK_W), F32)], axis=1))

    zc = _dot_nt(hb, w_ref[GQA_OFF:DIF_OFF, :])
    bd64 = bd64_ref[...]
    q = _group_rms(zc[:, 0:256], bd64, nw_ref[0:1, :], HEAD_W)
    put(cq_ref, 0, _rope(q, g_cos, g_sin, 32) * (HEAD_W ** -0.5 * LOG2E))
    k = _group_rms(zc[:, 256:384], bd64[0:128, 0:128], nw_ref[1:2, 0:128], HEAD_W)
    put(ck_ref, 0, _rope(k, g_cos, g_sin, 32))
    put(cv_ref, 0, zc[:, 384:512])

    zd = _dot_nt(hb, w_ref[DIF_OFF:GATES_OFF, :])
    bd32 = bd32_ref[...]
    q = _group_rms(zd[:, 0:256], bd32, nw_ref[2:3, :], DIF_QK)
    put(dq_ref, 0, _rope(q, d_cos, d_sin, 16) * (DIF_QK ** -0.5 * LOG2E))
    k = _group_rms(zd[:, 256:512], bd32, nw_ref[3:4, :], DIF_QK)
    put(dk_ref, 0, _rope(k, d_cos, d_sin, 16))
    put(dv_ref, 0, zd[:, 512:768])


def _inproj(sources, mod, norm1, w_proj, table, bd64, bd32, qk_norms, geo, layer):
    nb, _, d = sources[0][0].shape
    s = geo["seq"] + geo["ctx"]
    rt = geo["rt"]
    widths = (1024, GLA_GROUP_W, 256, 128, 128, 256, 256, 256)
    dtypes = (F32, F32, BF16, BF16, BF16, BF16, BF16, BF16)
    outs = None
    for src, off, n_tiles in sources:
        prev = [] if outs is None else list(outs)
        n_in = 8
        outs = pl.pallas_call(
            _inproj_kernel,
            grid=(n_tiles,),
            in_specs=[pl.BlockSpec((nb, rt, d), lambda i: (0, i, 0)),
                      _mod_spec(nb, d, geo["seq"] // rt, layer, off),
                      _resident((1, d), layer),
                      _resident((GATES_OFF, d), layer),
                      pl.BlockSpec((rt, 768), lambda i, off=off: (off + i, 0)),
                      _resident((256, 256)),
                      _resident((256, 256)),
                      _resident((8, 256), layer)] + [pl.BlockSpec(memory_space=pl.ANY)] * len(prev),
            out_specs=[pl.BlockSpec((nb, rt, w), lambda i, off=off: (0, off + i, 0)) for w in widths],
            out_shape=[jax.ShapeDtypeStruct((nb, s, w), t) for w, t in zip(widths, dtypes)],
            input_output_aliases={n_in + k: k for k in range(len(prev))},
            compiler_params=_cparams(("parallel",)),
            name="inproj",
        )(src, mod, norm1, w_proj, table, bd64, bd32, qk_norms, *prev)
    return outs


def _scan_index_maps(geo, chunk):
    nl, nc = geo["seq"] // chunk, geo["ctx"] // chunk
    fwd = lambda s: (0, jnp.where(s < nc, nl + s, s - nc), 0)
    bwd = lambda s: (0, jnp.where(s < nc, nl + nc - 1 - s, nl - 1 - (s - nc)), 0)
    return fwd, bwd, nl + nc


def _ret_kernel(lgr_ref, lgl_ref, hm_ref, bd64_ref, zf_ref, zb_ref, of_ref, ob_ref, st_ref, dec_ref):
    nb = zf_ref.shape[0]
    c = RET_CHUNK_LEN

    @pl.when(pl.program_id(0) == 0)
    def _():
        st_ref[...] = jnp.zeros_like(st_ref)
        ii = lax.broadcasted_iota(jnp.int32, (N_HEADS * c, c), 0) & (c - 1)
        jj = lax.broadcasted_iota(jnp.int32, (N_HEADS * c, c), 1)
        for d in range(2):
            dist = ((jj - ii) if d else (ii - jj)).astype(F32)
            dec_ref[d] = jnp.where(dist >= 0, jnp.exp(lgr_ref[d] * jnp.maximum(dist, 0.0)), 0.0)

    hm = hm_ref[...]
    bd64_f = bd64_ref[...].astype(F32)
    ri = lax.broadcasted_iota(jnp.int32, (c, BRANCH_W), 0).astype(F32)
    for rev, z_ref, o_ref in ((False, zf_ref, of_ref), (True, zb_ref, ob_ref)):
        d = int(rev)
        dec = dec_ref[d]
        lgl = lgl_ref[d:d + 1, :]
        q_dec = jnp.exp(lgl * ((c - ri) if rev else (ri + 1.0)))
        k_dec = jnp.exp(lgl * (ri if rev else (c - 1.0 - ri)))
        c_dec = jnp.exp(lgl * float(c))
        for b in range(nb):
            q = z_ref[b, :, 0:256]
            k = z_ref[b, :, 256:512]
            v = z_ref[b, :, 512:768]
            q4 = jnp.concatenate([q] * N_HEADS, axis=0) * hm
            o4 = _dot(_dot_nt(q4, k) * dec, v) * hm
            st = st_ref[d, b]
            o_ref[b] = o4[0:c] + o4[c:2 * c] + o4[2 * c:3 * c] + o4[3 * c:4 * c] + _dot(q, st) * q_dec
            st_ref[d, b] = st * c_dec + _dot_tn(k * k_dec, v) * bd64_f


def _retention(zr, lg_rows, lg_lane, head_mask, bd64, geo, layer):
    nb, s, _ = zr.shape
    c = RET_CHUNK_LEN
    fwd, bwd, steps = _scan_index_maps(geo, c)
    return pl.pallas_call(
        _ret_kernel,
        grid=(steps,),
        in_specs=[_resident((2, N_HEADS * c, c), layer), _resident((2, BRANCH_W), layer),
                  _resident((N_HEADS * c, BRANCH_W)),
                  _resident((256, 256)),
                  pl.BlockSpec((nb, c, 1024), fwd), pl.BlockSpec((nb, c, 1024), bwd)],
        out_specs=[pl.BlockSpec((nb, c, BRANCH_W), fwd), pl.BlockSpec((nb, c, BRANCH_W), bwd)],
        out_shape=[jax.ShapeDtypeStruct((nb, s, BRANCH_W), F32)] * 2,
        scratch_shapes=[pltpu.VMEM((2, nb, BRANCH_W, BRANCH_W), F32), pltpu.VMEM((2, N_HEADS * c, c), F32)],
        compiler_params=_cparams(("arbitrary",)),
        name="retention",
    )(lg_rows, lg_lane, head_mask, bd64, zr, zr)


GLA_FACTOR_MIN_LOG_DECAY = -40.0
GLA_FACTOR_MAX_KEY = 1e15


def _gla_prepare(z_ref, d, gu_ref, gb_ref, tri_ref):
    nb, c, _ = z_ref.shape
    m = nb * c
    q = z_ref[:, :, 0:256].reshape(m, BRANCH_W)
    k = z_ref[:, :, 256:512].reshape(m, BRANCH_W)
    v = z_ref[:, :, 512:768].reshape(m, BRANCH_W)
    logit = _dot(z_ref[:, :, 1024:GLA_GROUP_W].reshape(m, LANES), gu_ref[d]) + gb_ref[d:d + 1, :]
    g = (jnp.minimum(logit, 0.0) - jnp.log(1.0 + jnp.exp(-jnp.abs(logit)))) * GLA_TAU_INV
    g1 = g.astype(BF16)
    r1 = g - g1.astype(F32)
    g2 = r1.astype(BF16)
    g3 = (r1 - g2.astype(F32)).astype(BF16)
    tri = tri_ref[d]
    bcum = jnp.concatenate(
        [sum(jnp.dot(tri, t[b * c:(b + 1) * c], preferred_element_type=F32) for t in (g1, g2, g3))
         for b in range(nb)], axis=0)
    return q, k, v, bcum


def _gla_state_step(st_ref, d, b, q_state_b, kb, vb, bb, rev, bd64_f):
    c = kb.shape[0]
    b_last = bb[0:1, :] if rev else bb[c - 1:c, :]
    st = st_ref[d, b]
    o_b = _dot_nt(q_state_b, st)
    st_ref[d, b] = st * jnp.exp(b_last) + _dot_tn(vb, kb * jnp.exp(b_last - bb)) * bd64_f
    return o_b


def _gla_factored(d, rev, q, k, v, bcum, o_ref, st_ref, hm64_ref, causal_ref, bd64_f):
    nb, c, _ = o_ref.shape
    q_state = q * jnp.exp(bcum)
    k_inv = k * jnp.exp(-bcum)
    hm = hm64_ref[...]
    causal = causal_ref[d]
    outs = []
    for b in range(nb):
        rows = slice(b * c, (b + 1) * c)
        qb, vb = q_state[rows], v[rows]
        q4 = jnp.concatenate([qb] * N_HEADS, axis=0) * hm
        att = jnp.where(causal > 0.0, _dot_nt(q4, k_inv[rows]), 0.0)
        o4 = _dot(att, vb) * hm
        o_b = o4[0:c] + o4[c:2 * c] + o4[2 * c:3 * c] + o4[3 * c:4 * c]
        outs.append(o_b + _gla_state_step(st_ref, d, b, qb, k[rows], vb, bcum[rows], rev, bd64_f))
    o_ref[...] = jnp.concatenate(outs, axis=0).reshape(nb, c, BRANCH_W)


def _gla_pairwise(d, rev, q, k, v, bcum, o_ref, st_ref, w_ref, hm16_ref, bd64, bd64_f):
    nb, c, _ = o_ref.shape
    sub = GLA_SUB
    nsub = c // sub
    m = nb * c
    head_mask = hm16_ref[...]
    rin = lax.broadcasted_iota(jnp.int32, (m, BRANCH_W), 0) & (sub - 1)
    for dl in range(sub):
        if dl == 0:
            w = q * k
        else:
            shift = (m - dl) if rev else dl
            ks = pltpu.roll(k, shift, 0)
            bs = pltpu.roll(bcum, shift, 0)
            valid = (rin + dl <= sub - 1) if rev else (rin >= dl)
            w = jnp.where(valid, q * ks * jnp.exp(jnp.minimum(bcum - bs, 0.0)), 0.0)
        w_ref[dl * m:(dl + 1) * m, :] = w.astype(BF16)
    att = jnp.dot(w_ref[...], bd64, preferred_element_type=F32)
    o_acc = att[0:m] * v
    for dl in range(1, sub):
        o_acc = o_acc + att[dl * m:(dl + 1) * m] * pltpu.roll(v, (m - dl) if rev else dl, 0)

    q_state = q * jnp.exp(bcum)
    outs = []
    for b in range(nb):
        r0 = b * c
        qb, kb, vb, bb = q[r0:r0 + c], k[r0:r0 + c], v[r0:r0 + c], bcum[r0:r0 + c]
        rows = []
        for blk in range(nsub):
            a0, a1 = blk * sub, (blk + 1) * sub
            if rev and blk < nsub - 1:
                anchor, lo, hi = bb[a1:a1 + 1, :], a1, c
            elif (not rev) and blk > 0:
                anchor, lo, hi = bb[a0 - 1:a0, :], 0, a0
            else:
                rows.append(jnp.zeros((sub, BRANCH_W), F32))
                continue
            qa = qb[a0:a1] * jnp.exp(bb[a0:a1] - anchor)
            ka = kb[lo:hi] * jnp.exp(anchor - bb[lo:hi])
            q4 = jnp.concatenate([qa] * N_HEADS, axis=0) * head_mask
            o4 = _dot(_dot_nt(q4, ka), vb[lo:hi]) * head_mask
            rows.append(o4[0:sub] + o4[sub:2 * sub] + o4[2 * sub:3 * sub] + o4[3 * sub:4 * sub])
        o_b = _gla_state_step(st_ref, d, b, q_state[r0:r0 + c], kb, vb, bb, rev, bd64_f)
        outs.append(o_b + jnp.concatenate(rows, axis=0))
    o_ref[...] = (o_acc + jnp.concatenate(outs, axis=0)).reshape(nb, c, BRANCH_W)


def _gla_kernel(gu_ref, gb_ref, tri_ref, bd64_ref, hm16_ref, hm64_ref, causal_ref, zf_ref, zb_ref,
                of_ref, ob_ref, st_ref, w_ref):
    @pl.when(pl.program_id(0) == 0)
    def _():
        st_ref[...] = jnp.zeros_like(st_ref)

    bd64 = bd64_ref[...]
    bd64_f = bd64.astype(F32)
    fwd = _gla_prepare(zf_ref, 0, gu_ref, gb_ref, tri_ref)
    bwd = _gla_prepare(zb_ref, 1, gu_ref, gb_ref, tri_ref)
    min_decay = jnp.minimum(jnp.min(fwd[3]), jnp.min(bwd[3]))
    max_key = jnp.maximum(jnp.max(jnp.abs(fwd[1])), jnp.max(jnp.abs(bwd[1])))
    factor_ok = jnp.logical_and(min_decay > GLA_FACTOR_MIN_LOG_DECAY, max_key < GLA_FACTOR_MAX_KEY)

    @pl.when(factor_ok)
    def _():
        _gla_factored(0, False, *fwd, of_ref, st_ref, hm64_ref, causal_ref, bd64_f)
        _gla_factored(1, True, *bwd, ob_ref, st_ref, hm64_ref, causal_ref, bd64_f)

    @pl.when(jnp.logical_not(factor_ok))
    def _():
        _gla_pairwise(0, False, *fwd, of_ref, st_ref, w_ref, hm16_ref, bd64, bd64_f)
        _gla_pairwise(1, True, *bwd, ob_ref, st_ref, w_ref, hm16_ref, bd64, bd64_f)


def _gla(zg, gate_up_pad, gate_b, tri, bd64, hm16, hm64, causal, geo, layer):
    nb, s, _ = zg.shape
    c = GLA_CHUNK_LEN
    fwd, bwd, steps = _scan_index_maps(geo, c)
    return pl.pallas_call(
        _gla_kernel,
        grid=(steps,),
        in_specs=[_resident((2, LANES, BRANCH_W), layer), _resident((2, BRANCH_W), layer),
                  _resident((2, c, c)), _resident((256, 256)), _resident((N_HEADS * GLA_SUB, BRANCH_W)),
                  _resident((N_HEADS * c, BRANCH_W)), _resident((2, N_HEADS * c, c)),
                  pl.BlockSpec((nb, c, GLA_GROUP_W), fwd), pl.BlockSpec((nb, c, GLA_GROUP_W), bwd)],
        out_specs=[pl.BlockSpec((nb, c, BRANCH_W), fwd), pl.BlockSpec((nb, c, BRANCH_W), bwd)],
        out_shape=[jax.ShapeDtypeStruct((nb, s, BRANCH_W), F32)] * 2,
        scratch_shapes=[pltpu.VMEM((2, nb, BRANCH_W, BRANCH_W), F32),
                        pltpu.VMEM((GLA_SUB * nb * c, BRANCH_W), BF16)],
        compiler_params=_cparams(("arbitrary",)),
        name="gla",
    )(gate_up_pad, gate_b, tri, bd64, hm16, hm64, causal, zg, zg)


SOFTMAX_MIN_SUM = 2.0 ** -80


def _softmax_pv(qk_pairs, v, s_ref, p_ref, slot):
    tq = qk_pairs[0][0].shape[0]
    n_keys = v.shape[0]
    rows = len(qk_pairs) * tq
    for i, (q, k) in enumerate(qk_pairs):
        s_ref[i * tq:(i + 1) * tq, 0:n_keys] = _dot_nt(q, k)
    sums = []
    for r0 in range(0, rows, SOFTMAX_ROWS):
        s = s_ref[r0:r0 + SOFTMAX_ROWS, 0:n_keys]
        p = jnp.exp2(s - jnp.max(s, axis=-1, keepdims=True))
        sums.append(jnp.sum(p, axis=-1, keepdims=True))
        p_ref[slot, r0:r0 + SOFTMAX_ROWS, 0:n_keys] = p.astype(BF16)
    l = jnp.concatenate(sums, axis=0)
    o = jnp.dot(p_ref[slot, 0:rows, 0:n_keys], v, preferred_element_type=F32) / l
    return [o[i * tq:(i + 1) * tq] for i in range(len(qk_pairs))]


def _softmax_pv_bounded(qk_pairs, v, bounds, p_ref, slot, shared_keys, mix=None):
    tq = qk_pairs[0][0].shape[0]
    n_keys = v.shape[0]
    rows = len(qk_pairs) * tq
    if shared_keys:
        q_all = jnp.concatenate([q for q, _ in qk_pairs], axis=0)
        p = jnp.exp2(_dot_nt(q_all, qk_pairs[0][1]) - jnp.concatenate(bounds, axis=0))
        l = jnp.sum(p, axis=-1, keepdims=True)
        p_ref[slot, 0:rows, 0:n_keys] = p.astype(BF16)
    else:
        sums = []
        for i, (q, k) in enumerate(qk_pairs):
            p = jnp.exp2(_dot_nt(q, k) - bounds[i])
            sums.append(jnp.sum(p, axis=-1, keepdims=True))
            p_ref[slot, i * tq:(i + 1) * tq, 0:n_keys] = p.astype(BF16)
        l = jnp.concatenate(sums, axis=0)
        if mix is not None:
            l0, l1 = sums
            weight = (mix * l0 / l1).astype(BF16)
            p_ref[slot, 0:tq, 0:n_keys] = (p_ref[slot, 0:tq, 0:n_keys]
                                           - weight * p_ref[slot, tq:2 * tq, 0:n_keys])
            o = jnp.dot(p_ref[slot, 0:tq, 0:n_keys], v, preferred_element_type=F32) / l0
            return [o], jnp.min(l)
    o = jnp.dot(p_ref[slot, 0:rows, 0:n_keys], v, preferred_element_type=F32) / l
    return [o[i * tq:(i + 1) * tq] for i in range(len(qk_pairs))], jnp.min(l)


def _score_bounds(q_ref, key_max, bd):
    a = jnp.abs(q_ref[...].astype(F32)) * key_max
    return jnp.dot(a.astype(BF16), bd, preferred_element_type=F32)


def _attention_body(units, finish, q_ref, k_ref, bound, o_ref, s_ref, p_ref, mixes=None):
    def run(bounded):
        outs, l_min = [], None
        for u, (v, pairs) in enumerate(units):
            qk = [(q_ref[:, qs], k_ref[:, ks]) for qs, ks, _ in pairs]
            if bounded:
                shared = all(ks == pairs[0][1] for _, ks, _ in pairs)
                o, lm = _softmax_pv_bounded(qk, v, [bound[:, c:c + 1] for _, _, c in pairs], p_ref, u % 2, shared,
                                            None if mixes is None else mixes[u])
                l_min = lm if l_min is None else jnp.minimum(l_min, lm)
            else:
                o = _softmax_pv(qk, v, s_ref, p_ref, u % 2)
            outs.extend(o)
        o_ref[...] = finish(outs, bounded and mixes is not None).astype(o_ref.dtype)
        return l_min

    l_min = run(True)
    pl.when(jnp.logical_not(l_min >= SOFTMAX_MIN_SUM))(lambda: run(False))


def _key_max(k_ref, kmax_ref):
    @pl.when(pl.program_id(1) == 0)
    def _():
        kmax_ref[...] = jnp.max(jnp.abs(k_ref[...].astype(F32)), axis=0, keepdims=True)
    return kmax_ref[...]


def _gqa_kernel(bd64_ref, q_ref, k_ref, v_ref, *rest):
    o_ref, s_ref, p_ref, kmax_ref = rest[-4:]
    km = _key_max(k_ref, kmax_ref)
    km = jnp.concatenate([km[:, 0:HEAD_W]] * 2 + [km[:, HEAD_W:2 * HEAD_W]] * 2, axis=1)
    bound = _score_bounds(q_ref, km, bd64_ref[...])
    units = []
    for kv in range(2):
        cs = slice(kv * HEAD_W, (kv + 1) * HEAD_W)
        pairs = [(slice(h * HEAD_W, (h + 1) * HEAD_W), cs, h * HEAD_W) for h in (2 * kv, 2 * kv + 1)]
        units.append((v_ref[:, cs], pairs))
    _attention_body(units, lambda outs, _: jnp.concatenate(outs, axis=1), q_ref, k_ref, bound, o_ref, s_ref, p_ref)


def _dif_kernel(lam_ref, sub_ref, bd64_ref, bd32_ref, q_ref, k_ref, v_ref, *rest, out_scale):
    o_ref, s_ref, p_ref, kmax_ref = rest[-4:]
    bound = _score_bounds(q_ref, _key_max(k_ref, kmax_ref), bd32_ref[...])
    units = []
    for h in range(N_HEADS):
        pairs = []
        for mp in range(2):
            qs = slice((2 * h + mp) * DIF_QK, (2 * h + mp + 1) * DIF_QK)
            pairs.append((qs, qs, (2 * h + mp) * DIF_QK))
        units.append((v_ref[:, h * HEAD_W:(h + 1) * HEAD_W], pairs))

    def finish(outs, merged):
        heads = outs if merged else [outs[2 * h] - lam_ref[:, h * HEAD_W:(h + 1) * HEAD_W] * outs[2 * h + 1]
                                     for h in range(N_HEADS)]
        return _group_rms(jnp.concatenate(heads, axis=1), bd64_ref[...], sub_ref[...], HEAD_W) * out_scale

    _attention_body(units, finish, q_ref, k_ref, bound, o_ref, s_ref, p_ref, mixes=[lam_ref[:, 0:1]] * N_HEADS)


def _attention(body, name, pre_specs, pre_args, q, k, v, geo, need_ctx):
    nb, t, n_ctx, tq = geo["batch"], geo["seq"], geo["ctx"], geo["tq"]
    s = t + n_ctx
    kw, vw = k.shape[2], v.shape[2]
    keys = lambda n: pl.BlockSpec((None, n, kw), lambda b, i: (b, 0 if n == s else t // n_ctx, 0),
                                  pipeline_mode=pl.Buffered(1))
    vals = lambda n: pl.BlockSpec((None, n, vw), lambda b, i: (b, 0 if n == s else t // n_ctx, 0),
                                  pipeline_mode=pl.Buffered(1))

    def call(n_keys, q_tiles, q_off, prev):
        qmap = lambda b, i: (b, q_off + i, 0)
        alias = {} if prev is None else {len(pre_args) + 3: 0}
        extra = [] if prev is None else [pl.BlockSpec(memory_space=pl.ANY)]
        return pl.pallas_call(
            body, grid=(nb, q_tiles),
            in_specs=pre_specs + [pl.BlockSpec((None, tq, BRANCH_W), qmap), keys(n_keys), vals(n_keys)] + extra,
            out_specs=pl.BlockSpec((None, tq, BRANCH_W), qmap),
            out_shape=jax.ShapeDtypeStruct(q.shape, BF16),
            scratch_shapes=[pltpu.VMEM((2 * tq, n_keys), F32), pltpu.VMEM((2, 2 * tq, n_keys), BF16),
                            pltpu.VMEM((1, kw), F32)],
            input_output_aliases=alias,
            compiler_params=_cparams(("parallel", "arbitrary")),
            name=name,
        )(*pre_args, q, k, v, *([] if prev is None else [prev]))

    out = call(s, t // tq, 0, None)
    if need_ctx:
        out = call(n_ctx, n_ctx // tq, t // tq, out)
    return out


def _gqa(cq, ck, cv, bd64, geo, need_ctx):
    return _attention(_gqa_kernel, "gqa", [_resident((256, 256))], [bd64], cq, ck, cv, geo, need_ctx)


def _dif(dq, dk, dv, lam_lane, subln, bd64, bd32, geo, need_ctx, out_scale, layer):
    pre = [_resident((1, BRANCH_W), layer), _resident((1, BRANCH_W), layer), _resident((256, 256)),
           _resident((256, 256))]
    return _attention(functools.partial(_dif_kernel, out_scale=out_scale), "diffattn", pre,
                      [lam_lane, subln, bd64, bd32], dq, dk, dv, geo, need_ctx)


def _merge_kernel(x_ref, mod_ref, n1_ref, wg_ref, wb_ref, wo_ref, bd64_ref, gnw_ref,
                  rof_ref, rob_ref, rg_ref, gof_ref, gob_ref, gr_ref, yc_ref, yd_ref, *rest):
    o_ref = rest[-1]
    nb, rt, d = x_ref.shape
    m = nb * rt
    bd64 = bd64_ref[...]
    for g0, g1 in _sample_groups(nb):
        ng = g1 - g0
        m = ng * rt
        x = x_ref[g0:g1]
        hb = _adaln(x, n1_ref[...], mod_ref[g0:g1, 0:1, :], mod_ref[g0:g1, 1:2, :]).astype(BF16).reshape(m, d)
        flat = lambda ref: ref[g0:g1].reshape(m, BRANCH_W)
        ya = _group_rms(flat(rof_ref) + flat(rob_ref), bd64, None, HEAD_W) * _silu(flat(rg_ref))
        yb = _group_rms(flat(gof_ref) + flat(gob_ref), bd64, gnw_ref[...], HEAD_W) * _silu(flat(gr_ref))
        ys = (ya.astype(BF16), yb.astype(BF16), flat(yc_ref), flat(yd_ref))
        mix = None
        for i in range(4):
            gate = _sigmoid(_dot_nt(hb, wg_ref[GATES_OFF + i * d:GATES_OFF + (i + 1) * d, :]))
            term = gate * jnp.dot(ys[i], wb_ref[i], preferred_element_type=F32)
            mix = term if mix is None else mix + term
        o_ref[g0:g1] = x + mod_ref[g0:g1, 2:3, :] * _dot(mix, wo_ref[...]).reshape(ng, rt, d)


def _merge(sources, mod, norm1, w_all, w_branch, w_out, bd64, gla_norm, rof, rob, zr, gof, gob, zg, yc, yd, geo,
           need_ctx, layer):
    nb, _, d = sources[0][0].shape
    s = geo["seq"] + geo["ctx"]
    rt = geo["rt_merge"]
    bw = BRANCH_W
    n_in = 16
    out = None
    for src, off, n_tiles in sources:
        joint = src.shape[1] == s
        if joint and not need_ctx:
            n_tiles = geo["seq"] // rt
        row = lambda i, off=off: (0, off + i, 0)
        col3 = lambda i, off=off: (0, off + i, 3)
        act = lambda imap: pl.BlockSpec((nb, rt, bw), imap)
        prev = [] if out is None else [out]
        out = pl.pallas_call(
            _merge_kernel,
            grid=(n_tiles,),
            in_specs=[pl.BlockSpec((nb, rt, d), lambda i: (0, i, 0)),
                      _mod_spec(nb, d, geo["seq"] // rt, layer, off),
                      _resident((1, d), layer), _resident((w_all.shape[1], d), layer),
                      _resident((4, bw, d), layer), _resident((d, d), layer), _resident((256, 256)),
                      _resident((1, bw), layer),
                      act(row), act(row), act(col3), act(row), act(row), act(col3), act(row), act(row)]
                     + [pl.BlockSpec(memory_space=pl.ANY)] * len(prev),
            out_specs=pl.BlockSpec((nb, rt, d), row),
            out_shape=jax.ShapeDtypeStruct((nb, s, d), F32),
            input_output_aliases={0: 0} if joint else ({n_in: 0} if prev else {}),
            compiler_params=_cparams(("parallel",)),
            name="merge",
        )(src, mod, norm1, w_all, w_branch, w_out, bd64, gla_norm, rof, rob, zr, gof, gob, zg, yc, yd, *prev)
    return out


FFN_CHUNKS = ((0, 1024), (1024, 2048), (2048, D_FF_W))


def _ffn_kernel(x_ref, xp_ref, xn_ref, mod_ref, n2_ref, wu_ref, cw_ref, wd_ref, o_ref, *, seq, total):
    nb, rt, d = x_ref.shape
    ext = rt + 2 * HALO
    first = pl.program_id(0) * rt
    keep_prev = jnp.where(jnp.logical_or(first == 0, first == seq), 0.0, 1.0)
    keep_next = jnp.where(jnp.logical_or(first + rt == seq, first + rt == total), 0.0, 1.0)
    for g0, g1 in _sample_groups(nb):
        ng = g1 - g0

        def hn(ref):
            return _adaln(ref[g0:g1], n2_ref[...], mod_ref[g0:g1, 3:4, :], mod_ref[g0:g1, 4:5, :])
        h_mid = hn(x_ref).astype(BF16)
        h_ext = jnp.concatenate([(hn(xp_ref) * keep_prev).astype(BF16), h_mid,
                                 (hn(xn_ref) * keep_next).astype(BF16)], axis=1).reshape(ng * ext, d)
        h_mid = h_mid.reshape(ng * rt, d)
        acc = None
        for f0, f1 in FFN_CHUNKS:
            tf = f1 - f0
            u = jnp.dot(h_ext, wu_ref[:, f0:f1], preferred_element_type=F32)
            centre = lambda a: a.reshape(ng, ext, tf)[:, HALO:HALO + rt, :]
            up = centre(pltpu.roll(u, 1, 0))
            mid = centre(u)
            dn = centre(pltpu.roll(u, ng * ext - 1, 0))
            a = (up * cw_ref[0:1, f0:f1] + mid * cw_ref[1:2, f0:f1] + dn * cw_ref[2:3, f0:f1]
                 + cw_ref[3:4, f0:f1])
            gate = jnp.dot(h_mid, wu_ref[:, D_FF_W + f0:D_FF_W + f1], preferred_element_type=F32)
            part = _dot((_silu(a) * gate.reshape(ng, rt, tf)).reshape(ng * rt, tf), wd_ref[f0:f1, :])
            acc = part if acc is None else acc + part
        o_ref[g0:g1] = x_ref[g0:g1] + mod_ref[g0:g1, 5:6, :] * acc.reshape(ng, rt, d)


def _ffn(xs, mod, norm2, w_up, conv_pack, w_down, geo, need_ctx, layer):
    nb, s, d = xs.shape
    rt, t = geo["rt_ffn"], geo["seq"]
    s_out = s if need_ctx else t
    hb = rt // HALO
    last_blk = s // HALO - 1
    row = lambda i: (0, i, 0)
    return pl.pallas_call(
        functools.partial(_ffn_kernel, seq=t, total=s),
        grid=(s_out // rt,),
        in_specs=[pl.BlockSpec((nb, rt, d), row),
                  pl.BlockSpec((nb, HALO, d), lambda i: (0, jnp.maximum(i * hb - 1, 0), 0)),
                  pl.BlockSpec((nb, HALO, d), lambda i: (0, jnp.minimum((i + 1) * hb, last_blk), 0)),
                  _mod_spec(nb, d, t // rt, layer),
                  _resident((1, d), layer),
                  _resident((d, 2 * D_FF_W), layer),
                  _resident((8, D_FF_W), layer),
                  _resident((D_FF_W, d), layer)],
        out_specs=pl.BlockSpec((nb, rt, d), row),
        out_shape=jax.ShapeDtypeStruct((nb, s_out, d), F32),
        compiler_params=_cparams(("parallel",)),
        name="convffn",
    )(xs, xs, xs, mod, norm2, w_up, conv_pack, w_down)


def _rope_tables(t, n_ctx):
    rows = t // GRID_WIDTH

    def axial(dim):
        axis_dim = dim // 2
        freqs = ROPE_BASE ** (-np.arange(0, axis_dim, 2, dtype=np.float64) / axis_dim)
        row = np.repeat(np.arange(rows, dtype=np.float64), GRID_WIDTH)
        col = np.tile(np.arange(GRID_WIDTH, dtype=np.float64), rows)
        ang = np.concatenate([row[:, None] * freqs, col[:, None] * freqs], axis=-1)
        return np.cos(ang), np.sin(ang)

    freqs = ROPE_BASE ** (-np.linspace(0.0, 1.0, HEAD_W // 2))
    ang = np.arange(t, dtype=np.float64)[:, None] * freqs
    parts = []
    for cos, sin in ((np.cos(ang), np.sin(ang)), axial(HEAD_W), axial(DIF_QK)):
        reps = LANES // (2 * cos.shape[1])
        parts.append(np.tile(cos, (1, 2 * reps)))
        parts.append(np.tile(np.concatenate([-sin, sin], axis=1), (1, reps)))
    table = np.concatenate(parts, axis=1)
    ident = np.tile(np.concatenate([np.ones((1, LANES)), np.zeros((1, LANES))], axis=1), (n_ctx, 3))
    return jnp.asarray(np.concatenate([table, ident], axis=0), dtype=F32)


def _block_diag_ones(width, group):
    idx = np.arange(width) // group
    return jnp.asarray(idx[:, None] == idx[None, :], dtype=BF16)


def _row_block_lane_mask(block, n_lanes, lane_group):
    n_groups = n_lanes // lane_group
    rows = np.arange(n_groups * block)[:, None] // block
    return jnp.asarray(rows == np.arange(n_lanes)[None, :] // lane_group, dtype=F32)


def kernel(x, c, ctx, c_ctx, w_mod, b_mod, norm1, norm2, w_in, ret_decay, gla_gate_up, gla_gate_b, gla_norm,
           gqa_qnorm, gqa_knorm, dif_qnorm, dif_knorm, dif_lambda, dif_subln, w_branch, w_out, w_up, conv_w,
           conv_b, w_down):
    nb, t, d = x.shape
    n_ctx = ctx.shape[1]
    depth = w_mod.shape[0]
    assert t % 256 == 0 and n_ctx % 256 == 0 and nb + 1 <= 8
    geo = dict(batch=nb, seq=t, ctx=n_ctx, rt=256, rt_merge=128, rt_ffn=256, tq=256)

    xs = None
    cc = jnp.zeros((8, d), F32).at[0:nb].set(c).at[nb].set(c_ctx)
    mod_all = _modulation(cc, w_mod, b_mod).reshape(depth, 8, 6, d)

    table = _rope_tables(t, n_ctx)
    bd64 = _block_diag_ones(256, HEAD_W)
    bd32 = _block_diag_ones(256, DIF_QK)
    ti = np.arange(GLA_CHUNK_LEN)
    tri = jnp.asarray(np.stack([ti[:, None] >= ti[None, :], ti[:, None] <= ti[None, :]]), dtype=BF16)
    hm16 = _row_block_lane_mask(GLA_SUB, BRANCH_W, HEAD_W)
    hm64 = _row_block_lane_mask(GLA_CHUNK_LEN, BRANCH_W, HEAD_W)
    hm128 = _row_block_lane_mask(RET_CHUNK_LEN, BRANCH_W, HEAD_W)
    qi = np.arange(N_HEADS * GLA_CHUNK_LEN)[:, None] % GLA_CHUNK_LEN
    kj = np.arange(GLA_CHUNK_LEN)[None, :]
    causal = jnp.asarray(np.stack([kj <= qi, kj >= qi]), dtype=F32)

    w_all = jnp.swapaxes(w_in, 1, 2).astype(BF16)
    w_branch_b, w_out_b, w_up_b, w_down_b = (a.astype(BF16) for a in (w_branch, w_out, w_up, w_down))
    qk_norms = jnp.stack([jnp.tile(gqa_qnorm, (1, 4)), jnp.tile(gqa_knorm, (1, 4)), jnp.tile(dif_qnorm, (1, 8)),
                          jnp.tile(dif_knorm, (1, 8))] + [jnp.zeros((depth, 256), F32)] * 4, axis=1)
    mod = jnp.stack([mod_all[:, 0:nb], jnp.broadcast_to(mod_all[:, nb:nb + 1], (depth, nb, 6, d))], axis=1)
    n1 = norm1[:, None, :]
    n2 = norm2[:, None, :]
    log_gamma = jax.nn.log_sigmoid(ret_decay.astype(F32))
    lg_rows = jnp.broadcast_to(jnp.repeat(log_gamma, RET_CHUNK_LEN, axis=2)[..., None],
                               (depth, 2, N_HEADS * RET_CHUNK_LEN, RET_CHUNK_LEN))
    lg_lane = jnp.repeat(log_gamma, HEAD_W, axis=2)
    pad_rows = lambda a, lo: jnp.pad(a, ((0, 0), (lo, LANES - GLA_RANK_W - lo), (0, 0)))
    gate_up = jnp.stack([pad_rows(gla_gate_up[:, 0], 0), pad_rows(gla_gate_up[:, 1], GLA_RANK_W)],
                        axis=1).astype(BF16)
    lam_init = [0.8 - 0.6 * math.exp(-0.3 * l) for l in range(depth)]
    lv = dif_lambda.astype(F32)
    lam = (jnp.exp(jnp.sum(lv[:, 0] * lv[:, 1], axis=-1)) - jnp.exp(jnp.sum(lv[:, 2] * lv[:, 3], axis=-1))
           + jnp.asarray(lam_init, F32))
    lam_lane = jnp.broadcast_to(lam[:, None, None], (depth, 1, BRANCH_W))
    subln = jnp.tile(dif_subln, (1, 4))[:, None, :]
    gnw = jnp.tile(gla_norm, (1, 4))[:, None, :]
    conv_pack = jnp.concatenate([conv_w, conv_b[:, None, :], jnp.zeros((depth, 4, D_FF_W), F32)], axis=1)

    for l in range(depth):
        need_ctx = l < depth - 1
        zr, zg, cq, ck, cv, dq, dk, dv = _inproj(_token_sources(x, ctx, xs, geo["rt"]), mod, n1, w_all, table,
                                                 bd64, bd32, qk_norms, geo, l)
        rof, rob = _retention(zr, lg_rows, lg_lane, hm128, bd64, geo, l)
        gof, gob = _gla(zg, gate_up, gla_gate_b, tri, bd64, hm16, hm64, causal, geo, l)
        yc = _gqa(cq, ck, cv, bd64, geo, need_ctx)
        yd = _dif(dq, dk, dv, lam_lane, subln, bd64, bd32, geo, need_ctx, 1.0 - lam_init[l], l)
        xs = _merge(_token_sources(x, ctx, xs, geo["rt_merge"]), mod, n1, w_all, w_branch_b, w_out_b, bd64, gnw, rof, rob, zr, gof, gob, zg, yc, yd,
                    geo, need_ctx, l)
        xs = _ffn(xs, mod, n2, w_up_b, conv_pack, w_down_b, geo, need_ctx, l)

    return xs
```

```python
import functools
import math

import jax
import jax.numpy as jnp
import numpy as np
from jax import lax
from jax.experimental import pallas as pl
from jax.experimental.pallas import tpu as pltpu

F32 = jnp.float32
BF16 = jnp.bfloat16

N_HEADS = 4
HEAD_W = 64
BRANCH_W = N_HEADS * HEAD_W
DIF_QK = 32
RET_CHUNK_LEN = 256
GLA_CHUNK_LEN = 128
GLA_SUB = 16
GLA_RANK_W = 16
GLA_TAU_INV = 1.0 / 16.0
GRID_WIDTH = 64
ROPE_BASE = 10000.0
EPS = 1e-6
D_FF_W = 2816
LOG2E = 1.4426950408889634

LANES = 128
VMEM_LIMIT_BYTES = 56 * 1024 * 1024

RET_OFF, GLA_OFF, CODES_OFF, GQA_OFF, DIF_OFF, GATES_OFF = 0, 1024, 2048, 2080, 2592, 3360
GLA_GROUP_W = 1024 + LANES
HALO = 16
SOFTMAX_ROWS = 128


def _cparams(sem):
    return pltpu.CompilerParams(dimension_semantics=sem, vmem_limit_bytes=VMEM_LIMIT_BYTES)


def _resident(shape, layer=None):
    nd = len(shape)
    if layer is None:
        return pl.BlockSpec(shape, lambda *_: (0,) * nd, pipeline_mode=pl.Buffered(1))
    return pl.BlockSpec((None,) + tuple(shape), lambda *_: (layer,) + (0,) * nd, pipeline_mode=pl.Buffered(1))


def _sample_groups(nb):
    half = nb // 2
    return [(0, nb)] if nb % 2 else [(0, half), (half, nb)]


def _dot(a, b):
    return jnp.dot(a.astype(BF16), b.astype(BF16), preferred_element_type=F32)


def _dot_nt(a, b):
    return lax.dot_general(a.astype(BF16), b.astype(BF16), (((1,), (1,)), ((), ())),
                           preferred_element_type=F32)


def _dot_tn(a, b):
    return lax.dot_general(a.astype(BF16), b.astype(BF16), (((0,), (0,)), ((), ())),
                           preferred_element_type=F32)


def _sigmoid(x):
    return 0.5 * jnp.tanh(0.5 * x) + 0.5


def _silu(x):
    return x * _sigmoid(x)


def _adaln(x, norm_w, shift, scale):
    h = x * lax.rsqrt(jnp.mean(x * x, axis=-1, keepdims=True) + EPS) * norm_w
    return h * (1.0 + scale) + shift


def _group_rms(x, bd, weight, group):
    ss = jnp.dot((x * x).astype(BF16), bd, preferred_element_type=F32)
    y = x * lax.rsqrt(ss * (1.0 / group) + EPS)
    return y if weight is None else y * weight


def _rope(x, cos, sin_signed, half):
    w = x.shape[1]
    reps = w // LANES
    if reps > 1:
        cos = jnp.concatenate([cos] * reps, axis=1)
        sin_signed = jnp.concatenate([sin_signed] * reps, axis=1)
    lane = lax.broadcasted_iota(jnp.int32, x.shape, 1)
    lower = (lane & (2 * half - 1)) < half
    partner = jnp.where(lower, pltpu.roll(x, w - half, 1), pltpu.roll(x, half, 1))
    return x * cos + partner * sin_signed


def _mod_kernel(c_ref, w_ref, b_ref, o_ref):
    o_ref[...] = _dot(_silu(c_ref[...]), w_ref[...]) + b_ref[...]


def _modulation(cc, w_mod, b_mod):
    n_layers, d, width = w_mod.shape
    tn = 1024
    return pl.pallas_call(
        _mod_kernel,
        grid=(n_layers, width // tn),
        in_specs=[pl.BlockSpec((8, d), lambda l, j: (0, 0)),
                  pl.BlockSpec((None, d, tn), lambda l, j: (l, 0, j)),
                  pl.BlockSpec((None, 1, tn), lambda l, j: (l, 0, j))],
        out_specs=pl.BlockSpec((None, 8, tn), lambda l, j: (l, 0, j)),
        out_shape=jax.ShapeDtypeStruct((n_layers, 8, width), F32),
        compiler_params=_cparams(("parallel", "parallel")),
        name="modulation",
    )(cc, w_mod, b_mod.reshape(n_layers, 1, width))


def _mod_spec(nb, d, n_lat_tiles, layer, tile_off=0):
    return pl.BlockSpec((None, None, nb, 6, d),
                        lambda i, *_: (layer, jnp.where(tile_off + i < n_lat_tiles, 0, 1), 0, 0, 0))


def _token_sources(x, ctx, xs, rt):
    if xs is not None:
        return [(xs, 0, xs.shape[1] // rt)]
    return [(x, 0, x.shape[1] // rt), (ctx, x.shape[1] // rt, ctx.shape[1] // rt)]


def _inproj_kernel(x_ref, mod_ref, n1_ref, w_ref, tab_ref, bd64_ref, bd32_ref, nw_ref, *rest):
    zr_ref, zg_ref, cq_ref, ck_ref, cv_ref, dq_ref, dk_ref, dv_ref = rest[-8:]
    nb, rt, d = x_ref.shape
    m = nb * rt
    hb = _adaln(x_ref[...], n1_ref[...], mod_ref[:, 0:1, :], mod_ref[:, 1:2, :]).astype(BF16).reshape(m, d)

    def table(lo):
        t = tab_ref[:, lo:lo + LANES]
        return jnp.concatenate([t] * nb, axis=0)

    r_cos, r_sin, g_cos, g_sin, d_cos, d_sin = (table(i * LANES) for i in range(6))

    def put(ref, lo, val):
        ref[:, :, lo:lo + val.shape[1]] = val.reshape(nb, rt, val.shape[1]).astype(ref.dtype)

    zr = _dot_nt(hb, w_ref[RET_OFF:GLA_OFF, :])
    put(zr_ref, 0, _rope(zr[:, 0:256], r_cos, r_sin, 32))
    put(zr_ref, 256, _rope(zr[:, 256:512], r_cos, r_sin, 32) * (HEAD_W ** -0.5))
    put(zr_ref, 512, zr[:, 512:1024])

    zg = _dot_nt(hb, w_ref[GLA_OFF:CODES_OFF, :])
    put(zg_ref, 0, zg[:, 0:256] * (HEAD_W ** -0.5))
    put(zg_ref, 256, zg[:, 256:1024])
    codes = _dot_nt(hb, w_ref[CODES_OFF:GQA_OFF, :])
    put(zg_ref, 1024, jnp.concatenate([codes, jnp.zeros((m, LANES - 2 * GLA_RANK_W), F32)], axis=1))

    zc = _dot_nt(hb, w_ref[GQA_OFF:DIF_OFF, :])
    bd64 = bd64_ref[...]
    q = _group_rms(zc[:, 0:256], bd64, nw_ref[0:1, :], HEAD_W)
    put(cq_ref, 0, _rope(q, g_cos, g_sin, 32) * (HEAD_W ** -0.5 * LOG2E))
    k = _group_rms(zc[:, 256:384], bd64[0:128, 0:128], nw_ref[1:2, 0:128], HEAD_W)
    put(ck_ref, 0, _rope(k, g_cos, g_sin, 32))
    put(cv_ref, 0, zc[:, 384:512])

    zd = _dot_nt(hb, w_ref[DIF_OFF:GATES_OFF, :])
    bd32 = bd32_ref[...]
    q = _group_rms(zd[:, 0:256], bd32, nw_ref[2:3, :], DIF_QK)
    put(dq_ref, 0, _rope(q, d_cos, d_sin, 16) * (DIF_QK ** -0.5 * LOG2E))
    k = _group_rms(zd[:, 256:512], bd32, nw_ref[3:4, :], DIF_QK)
    put(dk_ref, 0, _rope(k, d_cos, d_sin, 16))
    put(dv_ref, 0, zd[:, 512:768])


def _inproj(sources, mod, norm1, w_proj, table, bd64, bd32, qk_norms, geo, layer):
    nb, _, d = sources[0][0].shape
    s = geo["seq"] + geo["ctx"]
    rt = geo["rt"]
    widths = (1024, GLA_GROUP_W, 256, 128, 128, 256, 256, 256)
    dtypes = (F32, F32, BF16, BF16, BF16, BF16, BF16, BF16)
    outs = None
    for src, off, n_tiles in sources:
        prev = [] if outs is None else list(outs)
        n_in = 8
        outs = pl.pallas_call(
            _inproj_kernel,
            grid=(n_tiles,),
            in_specs=[pl.BlockSpec((nb, rt, d), lambda i: (0, i, 0)),
                      _mod_spec(nb, d, geo["seq"] // rt, layer, off),
                      _resident((1, d), layer),
                      _resident((GATES_OFF, d), layer),
                      pl.BlockSpec((rt, 768), lambda i, off=off: (off + i, 0)),
                      _resident((256, 256)),
                      _resident((256, 256)),
                      _resident((8, 256), layer)] + [pl.BlockSpec(memory_space=pl.ANY)] * len(prev),
            out_specs=[pl.BlockSpec((nb, rt, w), lambda i, off=off: (0, off + i, 0)) for w in widths],
            out_shape=[jax.ShapeDtypeStruct((nb, s, w), t) for w, t in zip(widths, dtypes)],
            input_output_aliases={n_in + k: k for k in range(len(prev))},
            compiler_params=_cparams(("parallel",)),
            name="inproj",
        )(src, mod, norm1, w_proj, table, bd64, bd32, qk_norms, *prev)
    return outs


def _scan_index_maps(geo, chunk):
    nl, nc = geo["seq"] // chunk, geo["ctx"] // chunk
    fwd = lambda s: (0, jnp.where(s < nc, nl + s, s - nc), 0)
    bwd = lambda s: (0, jnp.where(s < nc, nl + nc - 1 - s, nl - 1 - (s - nc)), 0)
    return fwd, bwd, nl + nc


def _ret_kernel(lgr_ref, lgl_ref, hm_ref, bd64_ref, zf_ref, zb_ref, of_ref, ob_ref, st_ref, dec_ref):
    nb = zf_ref.shape[0]
    c = RET_CHUNK_LEN

    @pl.when(pl.program_id(0) == 0)
    def _():
        st_ref[...] = jnp.zeros_like(st_ref)
        ii = lax.broadcasted_iota(jnp.int32, (N_HEADS * c, c), 0) & (c - 1)
        jj = lax.broadcasted_iota(jnp.int32, (N_HEADS * c, c), 1)
        for d in range(2):
            dist = ((jj - ii) if d else (ii - jj)).astype(F32)
            dec_ref[d] = jnp.where(dist >= 0, jnp.exp(lgr_ref[d] * jnp.maximum(dist, 0.0)), 0.0)

    hm = hm_ref[...]
    bd64_f = bd64_ref[...].astype(F32)
    ri = lax.broadcasted_iota(jnp.int32, (c, BRANCH_W), 0).astype(F32)
    for rev, z_ref, o_ref in ((False, zf_ref, of_ref), (True, zb_ref, ob_ref)):
        d = int(rev)
        dec = dec_ref[d]
        lgl = lgl_ref[d:d + 1, :]
        q_dec = jnp.exp(lgl * ((c - ri) if rev else (ri + 1.0)))
        k_dec = jnp.exp(lgl * (ri if rev else (c - 1.0 - ri)))
        c_dec = jnp.exp(lgl * float(c))
        for b in range(nb):
            q = z_ref[b, :, 0:256]
            k = z_ref[b, :, 256:512]
            v = z_ref[b, :, 512:768]
            q4 = jnp.concatenate([q] * N_HEADS, axis=0) * hm
            o4 = _dot(_dot_nt(q4, k) * dec, v) * hm
            st = st_ref[d, b]
            o_ref[b] = o4[0:c] + o4[c:2 * c] + o4[2 * c:3 * c] + o4[3 * c:4 * c] + _dot(q, st) * q_dec
            st_ref[d, b] = st * c_dec + _dot_tn(k * k_dec, v) * bd64_f


def _retention(zr, lg_rows, lg_lane, head_mask, bd64, geo, layer):
    nb, s, _ = zr.shape
    c = RET_CHUNK_LEN
    fwd, bwd, steps = _scan_index_maps(geo, c)
    return pl.pallas_call(
        _ret_kernel,
        grid=(steps,),
        in_specs=[_resident((2, N_HEADS * c, c), layer), _resident((2, BRANCH_W), layer),
                  _resident((N_HEADS * c, BRANCH_W)),
                  _resident((256, 256)),
                  pl.BlockSpec((nb, c, 1024), fwd), pl.BlockSpec((nb, c, 1024), bwd)],
        out_specs=[pl.BlockSpec((nb, c, BRANCH_W), fwd), pl.BlockSpec((nb, c, BRANCH_W), bwd)],
        out_shape=[jax.ShapeDtypeStruct((nb, s, BRANCH_W), F32)] * 2,
        scratch_shapes=[pltpu.VMEM((2, nb, BRANCH_W, BRANCH_W), F32), pltpu.VMEM((2, N_HEADS * c, c), F32)],
        compiler_params=_cparams(("arbitrary",)),
        name="retention",
    )(lg_rows, lg_lane, head_mask, bd64, zr, zr)


GLA_FACTOR_MIN_LOG_DECAY = -40.0
GLA_FACTOR_MAX_KEY = 1e15


def _gla_prepare(z_ref, d, gu_ref, gb_ref, tri_ref):
    nb, c, _ = z_ref.shape
    m = nb * c
    q = z_ref[:, :, 0:256].reshape(m, BRANCH_W)
    k = z_ref[:, :, 256:512].reshape(m, BRANCH_W)
    v = z_ref[:, :, 512:768].reshape(m, BRANCH_W)
    logit = _dot(z_ref[:, :, 1024:GLA_GROUP_W].reshape(m, LANES), gu_ref[d]) + gb_ref[d:d + 1, :]
    g = (jnp.minimum(logit, 0.0) - jnp.log(1.0 + jnp.exp(-jnp.abs(logit)))) * GLA_TAU_INV
    g1 = g.astype(BF16)
    r1 = g - g1.astype(F32)
    g2 = r1.astype(BF16)
    g3 = (r1 - g2.astype(F32)).astype(BF16)
    tri = tri_ref[d]
    bcum = jnp.concatenate(
        [sum(jnp.dot(tri, t[b * c:(b + 1) * c], preferred_element_type=F32) for t in (g1, g2, g3))
         for b in range(nb)], axis=0)
    return q, k, v, bcum


def _gla_state_step(st_ref, d, b, q_state_b, kb, vb, bb, rev, bd64_f):
    c = kb.shape[0]
    b_last = bb[0:1, :] if rev else bb[c - 1:c, :]
    st = st_ref[d, b]
    o_b = _dot_nt(q_state_b, st)
    st_ref[d, b] = st * jnp.exp(b_last) + _dot_tn(vb, kb * jnp.exp(b_last - bb)) * bd64_f
    return o_b


def _gla_factored(d, rev, q, k, v, bcum, o_ref, st_ref, hm_gla_ref, causal_ref, bd64_f):
    nb, c, _ = o_ref.shape
    q_state = q * jnp.exp(bcum)
    k_inv = k * jnp.exp(-bcum)
    hm = hm_gla_ref[...]
    causal = causal_ref[d]
    outs = []
    for b in range(nb):
        rows = slice(b * c, (b + 1) * c)
        qb, vb = q_state[rows], v[rows]
        q4 = jnp.concatenate([qb] * N_HEADS, axis=0) * hm
        att = jnp.where(causal > 0.0, _dot_nt(q4, k_inv[rows]), 0.0)
        o4 = _dot(att, vb) * hm
        o_b = o4[0:c] + o4[c:2 * c] + o4[2 * c:3 * c] + o4[3 * c:4 * c]
        outs.append(o_b + _gla_state_step(st_ref, d, b, qb, k[rows], vb, bcum[rows], rev, bd64_f))
    o_ref[...] = jnp.concatenate(outs, axis=0).reshape(nb, c, BRANCH_W)


def _gla_pairwise(d, rev, q, k, v, bcum, o_ref, st_ref, w_ref, hm16_ref, bd64, bd64_f):
    nb, c, _ = o_ref.shape
    sub = GLA_SUB
    nsub = c // sub
    m = nb * c
    head_mask = hm16_ref[...]
    rin = lax.broadcasted_iota(jnp.int32, (m, BRANCH_W), 0) & (sub - 1)
    for dl in range(sub):
        if dl == 0:
            w = q * k
        else:
            shift = (m - dl) if rev else dl
            ks = pltpu.roll(k, shift, 0)
            bs = pltpu.roll(bcum, shift, 0)
            valid = (rin + dl <= sub - 1) if rev else (rin >= dl)
            w = jnp.where(valid, q * ks * jnp.exp(jnp.minimum(bcum - bs, 0.0)), 0.0)
        w_ref[dl * m:(dl + 1) * m, :] = w.astype(BF16)
    att = jnp.dot(w_ref[...], bd64, preferred_element_type=F32)
    o_acc = att[0:m] * v
    for dl in range(1, sub):
        o_acc = o_acc + att[dl * m:(dl + 1) * m] * pltpu.roll(v, (m - dl) if rev else dl, 0)

    q_state = q * jnp.exp(bcum)
    outs = []
    for b in range(nb):
        r0 = b * c
        qb, kb, vb, bb = q[r0:r0 + c], k[r0:r0 + c], v[r0:r0 + c], bcum[r0:r0 + c]
        rows = []
        for blk in range(nsub):
            a0, a1 = blk * sub, (blk + 1) * sub
            if rev and blk < nsub - 1:
                anchor, lo, hi = bb[a1:a1 + 1, :], a1, c
            elif (not rev) and blk > 0:
                anchor, lo, hi = bb[a0 - 1:a0, :], 0, a0
            else:
                rows.append(jnp.zeros((sub, BRANCH_W), F32))
                continue
            qa = qb[a0:a1] * jnp.exp(bb[a0:a1] - anchor)
            ka = kb[lo:hi] * jnp.exp(anchor - bb[lo:hi])
            q4 = jnp.concatenate([qa] * N_HEADS, axis=0) * head_mask
            o4 = _dot(_dot_nt(q4, ka), vb[lo:hi]) * head_mask
            rows.append(o4[0:sub] + o4[sub:2 * sub] + o4[2 * sub:3 * sub] + o4[3 * sub:4 * sub])
        o_b = _gla_state_step(st_ref, d, b, q_state[r0:r0 + c], kb, vb, bb, rev, bd64_f)
        outs.append(o_b + jnp.concatenate(rows, axis=0))
    o_ref[...] = (o_acc + jnp.concatenate(outs, axis=0)).reshape(nb, c, BRANCH_W)


def _gla_kernel(gu_ref, gb_ref, tri_ref, bd64_ref, hm16_ref, hm_gla_ref, causal_ref, zf_ref, zb_ref,
                of_ref, ob_ref, st_ref, w_ref):
    @pl.when(pl.program_id(0) == 0)
    def _():
        st_ref[...] = jnp.zeros_like(st_ref)

    bd64 = bd64_ref[...]
    bd64_f = bd64.astype(F32)
    fwd = _gla_prepare(zf_ref, 0, gu_ref, gb_ref, tri_ref)
    bwd = _gla_prepare(zb_ref, 1, gu_ref, gb_ref, tri_ref)
    min_decay = jnp.minimum(jnp.min(fwd[3]), jnp.min(bwd[3]))
    max_key = jnp.maximum(jnp.max(jnp.abs(fwd[1])), jnp.max(jnp.abs(bwd[1])))
    factor_ok = jnp.logical_and(min_decay > GLA_FACTOR_MIN_LOG_DECAY, max_key < GLA_FACTOR_MAX_KEY)

    @pl.when(factor_ok)
    def _():
        _gla_factored(0, False, *fwd, of_ref, st_ref, hm_gla_ref, causal_ref, bd64_f)
        _gla_factored(1, True, *bwd, ob_ref, st_ref, hm_gla_ref, causal_ref, bd64_f)

    @pl.when(jnp.logical_not(factor_ok))
    def _():
        _gla_pairwise(0, False, *fwd, of_ref, st_ref, w_ref, hm16_ref, bd64, bd64_f)
        _gla_pairwise(1, True, *bwd, ob_ref, st_ref, w_ref, hm16_ref, bd64, bd64_f)


def _gla(zg, gate_up_pad, gate_b, tri, bd64, hm16, hm_gla, causal, geo, layer):
    nb, s, _ = zg.shape
    c = GLA_CHUNK_LEN
    fwd, bwd, steps = _scan_index_maps(geo, c)
    return pl.pallas_call(
        _gla_kernel,
        grid=(steps,),
        in_specs=[_resident((2, LANES, BRANCH_W), layer), _resident((2, BRANCH_W), layer),
                  _resident((2, c, c)), _resident((256, 256)), _resident((N_HEADS * GLA_SUB, BRANCH_W)),
                  _resident((N_HEADS * c, BRANCH_W)), _resident((2, N_HEADS * c, c)),
                  pl.BlockSpec((nb, c, GLA_GROUP_W), fwd), pl.BlockSpec((nb, c, GLA_GROUP_W), bwd)],
        out_specs=[pl.BlockSpec((nb, c, BRANCH_W), fwd), pl.BlockSpec((nb, c, BRANCH_W), bwd)],
        out_shape=[jax.ShapeDtypeStruct((nb, s, BRANCH_W), F32)] * 2,
        scratch_shapes=[pltpu.VMEM((2, nb, BRANCH_W, BRANCH_W), F32),
                        pltpu.VMEM((GLA_SUB * nb * c, BRANCH_W), BF16)],
        compiler_params=_cparams(("arbitrary",)),
        name="gla",
    )(gate_up_pad, gate_b, tri, bd64, hm16, hm_gla, causal, zg, zg)


SOFTMAX_MIN_SUM = 2.0 ** -80


def _softmax_pv(qk_pairs, v, s_ref, p_ref, slot):
    tq = qk_pairs[0][0].shape[0]
    n_keys = v.shape[0]
    rows = len(qk_pairs) * tq
    for i, (q, k) in enumerate(qk_pairs):
        s_ref[i * tq:(i + 1) * tq, 0:n_keys] = _dot_nt(q, k)
    sums = []
    for r0 in range(0, rows, SOFTMAX_ROWS):
        s = s_ref[r0:r0 + SOFTMAX_ROWS, 0:n_keys]
        p = jnp.exp2(s - jnp.max(s, axis=-1, keepdims=True))
        sums.append(jnp.sum(p, axis=-1, keepdims=True))
        p_ref[slot, r0:r0 + SOFTMAX_ROWS, 0:n_keys] = p.astype(BF16)
    l = jnp.concatenate(sums, axis=0)
    o = jnp.dot(p_ref[slot, 0:rows, 0:n_keys], v, preferred_element_type=F32) / l
    return [o[i * tq:(i + 1) * tq] for i in range(len(qk_pairs))]


def _softmax_pv_bounded(qk_pairs, v, bounds, p_ref, slot, shared_keys, mix=None):
    tq = qk_pairs[0][0].shape[0]
    n_keys = v.shape[0]
    rows = len(qk_pairs) * tq
    if shared_keys:
        q_all = jnp.concatenate([q for q, _ in qk_pairs], axis=0)
        p = jnp.exp2(_dot_nt(q_all, qk_pairs[0][1]) - jnp.concatenate(bounds, axis=0))
        l = jnp.sum(p, axis=-1, keepdims=True)
        p_ref[slot, 0:rows, 0:n_keys] = p.astype(BF16)
    else:
        sums = []
        for i, (q, k) in enumerate(qk_pairs):
            p = jnp.exp2(_dot_nt(q, k) - bounds[i])
            sums.append(jnp.sum(p, axis=-1, keepdims=True))
            p_ref[slot, i * tq:(i + 1) * tq, 0:n_keys] = p.astype(BF16)
        l = jnp.concatenate(sums, axis=0)
        if mix is not None:
            l0, l1 = sums
            weight = (mix * l0 / l1).astype(BF16)
            p_ref[slot, 0:tq, 0:n_keys] = (p_ref[slot, 0:tq, 0:n_keys]
                                           - weight * p_ref[slot, tq:2 * tq, 0:n_keys])
            o = jnp.dot(p_ref[slot, 0:tq, 0:n_keys], v, preferred_element_type=F32) / l0
            return [o], jnp.min(l)
    o = jnp.dot(p_ref[slot, 0:rows, 0:n_keys], v, preferred_element_type=F32) / l
    return [o[i * tq:(i + 1) * tq] for i in range(len(qk_pairs))], jnp.min(l)


def _score_bounds(q_ref, key_max, bd):
    a = jnp.abs(q_ref[...].astype(F32)) * key_max
    return jnp.dot(a.astype(BF16), bd, preferred_element_type=F32)


def _attention_body(units, finish, q_ref, k_ref, bound, o_ref, s_ref, p_ref, mixes=None):
    def run(bounded):
        outs, l_min = [], None
        for u, (v, pairs) in enumerate(units):
            qk = [(q_ref[:, qs], k_ref[:, ks]) for qs, ks, _ in pairs]
            if bounded:
                shared = all(ks == pairs[0][1] for _, ks, _ in pairs)
                o, lm = _softmax_pv_bounded(qk, v, [bound[:, c:c + 1] for _, _, c in pairs], p_ref, u % 2, shared,
                                            None if mixes is None else mixes[u])
                l_min = lm if l_min is None else jnp.minimum(l_min, lm)
            else:
                o = _softmax_pv(qk, v, s_ref, p_ref, u % 2)
            outs.extend(o)
        o_ref[...] = finish(outs, bounded and mixes is not None).astype(o_ref.dtype)
        return l_min

    l_min = run(True)
    pl.when(jnp.logical_not(l_min >= SOFTMAX_MIN_SUM))(lambda: run(False))


def _key_max(k_ref, kmax_ref):
    @pl.when(pl.program_id(1) == 0)
    def _():
        kmax_ref[...] = jnp.max(jnp.abs(k_ref[...].astype(F32)), axis=0, keepdims=True)
    return kmax_ref[...]


def _gqa_kernel(bd64_ref, q_ref, k_ref, v_ref, *rest):
    o_ref, s_ref, p_ref, kmax_ref = rest[-4:]
    km = _key_max(k_ref, kmax_ref)
    km = jnp.concatenate([km[:, 0:HEAD_W]] * 2 + [km[:, HEAD_W:2 * HEAD_W]] * 2, axis=1)
    bound = _score_bounds(q_ref, km, bd64_ref[...])
    units = []
    for kv in range(2):
        cs = slice(kv * HEAD_W, (kv + 1) * HEAD_W)
        pairs = [(slice(h * HEAD_W, (h + 1) * HEAD_W), cs, h * HEAD_W) for h in (2 * kv, 2 * kv + 1)]
        units.append((v_ref[:, cs], pairs))
    _attention_body(units, lambda outs, _: jnp.concatenate(outs, axis=1), q_ref, k_ref, bound, o_ref, s_ref, p_ref)


def _dif_kernel(lam_ref, sub_ref, bd64_ref, bd32_ref, q_ref, k_ref, v_ref, *rest, out_scale):
    o_ref, s_ref, p_ref, kmax_ref = rest[-4:]
    bound = _score_bounds(q_ref, _key_max(k_ref, kmax_ref), bd32_ref[...])
    units = []
    for h in range(N_HEADS):
        pairs = []
        for mp in range(2):
            qs = slice((2 * h + mp) * DIF_QK, (2 * h + mp + 1) * DIF_QK)
            pairs.append((qs, qs, (2 * h + mp) * DIF_QK))
        units.append((v_ref[:, h * HEAD_W:(h + 1) * HEAD_W], pairs))

    def finish(outs, merged):
        heads = outs if merged else [outs[2 * h] - lam_ref[:, h * HEAD_W:(h + 1) * HEAD_W] * outs[2 * h + 1]
                                     for h in range(N_HEADS)]
        return _group_rms(jnp.concatenate(heads, axis=1), bd64_ref[...], sub_ref[...], HEAD_W) * out_scale

    _attention_body(units, finish, q_ref, k_ref, bound, o_ref, s_ref, p_ref, mixes=[lam_ref[:, 0:1]] * N_HEADS)


def _attention(body, name, pre_specs, pre_args, q, k, v, geo, need_ctx):
    nb, t, n_ctx, tq = geo["batch"], geo["seq"], geo["ctx"], geo["tq"]
    s = t + n_ctx
    kw, vw = k.shape[2], v.shape[2]
    keys = lambda n: pl.BlockSpec((None, n, kw), lambda b, i: (b, 0 if n == s else t // n_ctx, 0),
                                  pipeline_mode=pl.Buffered(1))
    vals = lambda n: pl.BlockSpec((None, n, vw), lambda b, i: (b, 0 if n == s else t // n_ctx, 0),
                                  pipeline_mode=pl.Buffered(1))

    def call(n_keys, q_tiles, q_off, prev):
        qmap = lambda b, i: (b, q_off + i, 0)
        alias = {} if prev is None else {len(pre_args) + 3: 0}
        extra = [] if prev is None else [pl.BlockSpec(memory_space=pl.ANY)]
        return pl.pallas_call(
            body, grid=(nb, q_tiles),
            in_specs=pre_specs + [pl.BlockSpec((None, tq, BRANCH_W), qmap), keys(n_keys), vals(n_keys)] + extra,
            out_specs=pl.BlockSpec((None, tq, BRANCH_W), qmap),
            out_shape=jax.ShapeDtypeStruct(q.shape, BF16),
            scratch_shapes=[pltpu.VMEM((2 * tq, n_keys), F32), pltpu.VMEM((2, 2 * tq, n_keys), BF16),
                            pltpu.VMEM((1, kw), F32)],
            input_output_aliases=alias,
            compiler_params=_cparams(("parallel", "arbitrary")),
            name=name,
        )(*pre_args, q, k, v, *([] if prev is None else [prev]))

    out = call(s, t // tq, 0, None)
    if need_ctx:
        out = call(n_ctx, n_ctx // tq, t // tq, out)
    return out


def _gqa(cq, ck, cv, bd64, geo, need_ctx):
    return _attention(_gqa_kernel, "gqa", [_resident((256, 256))], [bd64], cq, ck, cv, geo, need_ctx)


def _dif(dq, dk, dv, lam_lane, subln, bd64, bd32, geo, need_ctx, out_scale, layer):
    pre = [_resident((1, BRANCH_W), layer), _resident((1, BRANCH_W), layer), _resident((256, 256)),
           _resident((256, 256))]
    return _attention(functools.partial(_dif_kernel, out_scale=out_scale), "diffattn", pre,
                      [lam_lane, subln, bd64, bd32], dq, dk, dv, geo, need_ctx)


def _merge_kernel(x_ref, mod_ref, n1_ref, wg_ref, wb_ref, wo_ref, bd64_ref, gnw_ref,
                  rof_ref, rob_ref, rg_ref, gof_ref, gob_ref, gr_ref, yc_ref, yd_ref, *rest):
    o_ref = rest[-1]
    nb, rt, d = x_ref.shape
    m = nb * rt
    bd64 = bd64_ref[...]
    for g0, g1 in _sample_groups(nb):
        ng = g1 - g0
        m = ng * rt
        x = x_ref[g0:g1]
        hb = _adaln(x, n1_ref[...], mod_ref[g0:g1, 0:1, :], mod_ref[g0:g1, 1:2, :]).astype(BF16).reshape(m, d)
        flat = lambda ref: ref[g0:g1].reshape(m, BRANCH_W)
        ya = _group_rms(flat(rof_ref) + flat(rob_ref), bd64, None, HEAD_W) * _silu(flat(rg_ref))
        yb = _group_rms(flat(gof_ref) + flat(gob_ref), bd64, gnw_ref[...], HEAD_W) * _silu(flat(gr_ref))
        ys = (ya.astype(BF16), yb.astype(BF16), flat(yc_ref), flat(yd_ref))
        mix = None
        for i in range(4):
            gate = _sigmoid(_dot_nt(hb, wg_ref[GATES_OFF + i * d:GATES_OFF + (i + 1) * d, :]))
            term = gate * jnp.dot(ys[i], wb_ref[i], preferred_element_type=F32)
            mix = term if mix is None else mix + term
        o_ref[g0:g1] = x + mod_ref[g0:g1, 2:3, :] * _dot(mix, wo_ref[...]).reshape(ng, rt, d)


def _merge(sources, mod, norm1, w_all, w_branch, w_out, bd64, gla_norm, rof, rob, zr, gof, gob, zg, yc, yd, geo,
           need_ctx, layer):
    nb, _, d = sources[0][0].shape
    s = geo["seq"] + geo["ctx"]
    rt = geo["rt_merge"]
    bw = BRANCH_W
    n_in = 16
    out = None
    for src, off, n_tiles in sources:
        joint = src.shape[1] == s
        if joint and not need_ctx:
            n_tiles = geo["seq"] // rt
        row = lambda i, off=off: (0, off + i, 0)
        col3 = lambda i, off=off: (0, off + i, 3)
        act = lambda imap: pl.BlockSpec((nb, rt, bw), imap)
        prev = [] if out is None else [out]
        out = pl.pallas_call(
            _merge_kernel,
            grid=(n_tiles,),
            in_specs=[pl.BlockSpec((nb, rt, d), lambda i: (0, i, 0)),
                      _mod_spec(nb, d, geo["seq"] // rt, layer, off),
                      _resident((1, d), layer), _resident((w_all.shape[1], d), layer),
                      _resident((4, bw, d), layer), _resident((d, d), layer), _resident((256, 256)),
                      _resident((1, bw), layer),
                      act(row), act(row), act(col3), act(row), act(row), act(col3), act(row), act(row)]
                     + [pl.BlockSpec(memory_space=pl.ANY)] * len(prev),
            out_specs=pl.BlockSpec((nb, rt, d), row),
            out_shape=jax.ShapeDtypeStruct((nb, s, d), F32),
            input_output_aliases={0: 0} if joint else ({n_in: 0} if prev else {}),
            compiler_params=_cparams(("parallel",)),
            name="merge",
        )(src, mod, norm1, w_all, w_branch, w_out, bd64, gla_norm, rof, rob, zr, gof, gob, zg, yc, yd, *prev)
    return out


FFN_CHUNKS = ((0, 1024), (1024, 2048), (2048, D_FF_W))


def _ffn_kernel(x_ref, xp_ref, xn_ref, mod_ref, n2_ref, wu_ref, cw_ref, wd_ref, o_ref, *, seq, total):
    nb, rt, d = x_ref.shape
    ext = rt + 2 * HALO
    first = pl.program_id(0) * rt
    keep_prev = jnp.where(jnp.logical_or(first == 0, first == seq), 0.0, 1.0)
    keep_next = jnp.where(jnp.logical_or(first + rt == seq, first + rt == total), 0.0, 1.0)
    for g0, g1 in _sample_groups(nb):
        ng = g1 - g0

        def hn(ref):
            return _adaln(ref[g0:g1], n2_ref[...], mod_ref[g0:g1, 3:4, :], mod_ref[g0:g1, 4:5, :])
        h_mid = hn(x_ref).astype(BF16)
        h_ext = jnp.concatenate([(hn(xp_ref) * keep_prev).astype(BF16), h_mid,
                                 (hn(xn_ref) * keep_next).astype(BF16)], axis=1).reshape(ng * ext, d)
        h_mid = h_mid.reshape(ng * rt, d)
        acc = None
        for f0, f1 in FFN_CHUNKS:
            tf = f1 - f0
            u = jnp.dot(h_ext, wu_ref[:, f0:f1], preferred_element_type=F32)
            centre = lambda a: a.reshape(ng, ext, tf)[:, HALO:HALO + rt, :]
            up = centre(pltpu.roll(u, 1, 0))
            mid = centre(u)
            dn = centre(pltpu.roll(u, ng * ext - 1, 0))
            a = (up * cw_ref[0:1, f0:f1] + mid * cw_ref[1:2, f0:f1] + dn * cw_ref[2:3, f0:f1]
                 + cw_ref[3:4, f0:f1])
            gate = jnp.dot(h_mid, wu_ref[:, D_FF_W + f0:D_FF_W + f1], preferred_element_type=F32)
            part = _dot((_silu(a) * gate.reshape(ng, rt, tf)).reshape(ng * rt, tf), wd_ref[f0:f1, :])
            acc = part if acc is None else acc + part
        o_ref[g0:g1] = x_ref[g0:g1] + mod_ref[g0:g1, 5:6, :] * acc.reshape(ng, rt, d)


def _ffn(xs, mod, norm2, w_up, conv_pack, w_down, geo, need_ctx, layer):
    nb, s, d = xs.shape
    rt, t = geo["rt_ffn"], geo["seq"]
    s_out = s if need_ctx else t
    hb = rt // HALO
    last_blk = s // HALO - 1
    row = lambda i: (0, i, 0)
    return pl.pallas_call(
        functools.partial(_ffn_kernel, seq=t, total=s),
        grid=(s_out // rt,),
        in_specs=[pl.BlockSpec((nb, rt, d), row),
                  pl.BlockSpec((nb, HALO, d), lambda i: (0, jnp.maximum(i * hb - 1, 0), 0)),
                  pl.BlockSpec((nb, HALO, d), lambda i: (0, jnp.minimum((i + 1) * hb, last_blk), 0)),
                  _mod_spec(nb, d, t // rt, layer),
                  _resident((1, d), layer),
                  _resident((d, 2 * D_FF_W), layer),
                  _resident((8, D_FF_W), layer),
                  _resident((D_FF_W, d), layer)],
        out_specs=pl.BlockSpec((nb, rt, d), row),
        out_shape=jax.ShapeDtypeStruct((nb, s_out, d), F32),
        compiler_params=_cparams(("parallel",)),
        name="convffn",
    )(xs, xs, xs, mod, norm2, w_up, conv_pack, w_down)


def _rope_tables(t, n_ctx):
    rows = t // GRID_WIDTH

    def axial(dim):
        axis_dim = dim // 2
        freqs = ROPE_BASE ** (-np.arange(0, axis_dim, 2, dtype=np.float64) / axis_dim)
        row = np.repeat(np.arange(rows, dtype=np.float64), GRID_WIDTH)
        col = np.tile(np.arange(GRID_WIDTH, dtype=np.float64), rows)
        ang = np.concatenate([row[:, None] * freqs, col[:, None] * freqs], axis=-1)
        return np.cos(ang), np.sin(ang)

    freqs = ROPE_BASE ** (-np.linspace(0.0, 1.0, HEAD_W // 2))
    ang = np.arange(t, dtype=np.float64)[:, None] * freqs
    parts = []
    for cos, sin in ((np.cos(ang), np.sin(ang)), axial(HEAD_W), axial(DIF_QK)):
        reps = LANES // (2 * cos.shape[1])
        parts.append(np.tile(cos, (1, 2 * reps)))
        parts.append(np.tile(np.concatenate([-sin, sin], axis=1), (1, reps)))
    table = np.concatenate(parts, axis=1)
    ident = np.tile(np.concatenate([np.ones((1, LANES)), np.zeros((1, LANES))], axis=1), (n_ctx, 3))
    return jnp.asarray(np.concatenate([table, ident], axis=0), dtype=F32)


def _block_diag_ones(width, group):
    idx = np.arange(width) // group
    return jnp.asarray(idx[:, None] == idx[None, :], dtype=BF16)


def _row_block_lane_mask(block, n_lanes, lane_group):
    n_groups = n_lanes // lane_group
    rows = np.arange(n_groups * block)[:, None] // block
    return jnp.asarray(rows == np.arange(n_lanes)[None, :] // lane_group, dtype=F32)


def kernel(x, c, ctx, c_ctx, w_mod, b_mod, norm1, norm2, w_in, ret_decay, gla_gate_up, gla_gate_b, gla_norm,
           gqa_qnorm, gqa_knorm, dif_qnorm, dif_knorm, dif_lambda, dif_subln, w_branch, w_out, w_up, conv_w,
           conv_b, w_down):
    nb, t, d = x.shape
    n_ctx = ctx.shape[1]
    depth = w_mod.shape[0]
    assert t % 256 == 0 and n_ctx % 256 == 0 and nb + 1 <= 8
    geo = dict(batch=nb, seq=t, ctx=n_ctx, rt=256, rt_merge=128, rt_ffn=256, tq=256)

    xs = None
    cc = jnp.zeros((8, d), F32).at[0:nb].set(c).at[nb].set(c_ctx)
    mod_all = _modulation(cc, w_mod, b_mod).reshape(depth, 8, 6, d)

    table = _rope_tables(t, n_ctx)
    bd64 = _block_diag_ones(256, HEAD_W)
    bd32 = _block_diag_ones(256, DIF_QK)
    ti = np.arange(GLA_CHUNK_LEN)
    tri = jnp.asarray(np.stack([ti[:, None] >= ti[None, :], ti[:, None] <= ti[None, :]]), dtype=BF16)
    hm16 = _row_block_lane_mask(GLA_SUB, BRANCH_W, HEAD_W)
    hm_gla = _row_block_lane_mask(GLA_CHUNK_LEN, BRANCH_W, HEAD_W)
    hm_ret = _row_block_lane_mask(RET_CHUNK_LEN, BRANCH_W, HEAD_W)
    qi = np.arange(N_HEADS * GLA_CHUNK_LEN)[:, None] % GLA_CHUNK_LEN
    kj = np.arange(GLA_CHUNK_LEN)[None, :]
    causal = jnp.asarray(np.stack([kj <= qi, kj >= qi]), dtype=F32)

    w_all = jnp.swapaxes(w_in, 1, 2).astype(BF16)
    w_branch_b, w_out_b, w_up_b, w_down_b = (a.astype(BF16) for a in (w_branch, w_out, w_up, w_down))
    qk_norms = jnp.stack([jnp.tile(gqa_qnorm, (1, 4)), jnp.tile(gqa_knorm, (1, 4)), jnp.tile(dif_qnorm, (1, 8)),
                          jnp.tile(dif_knorm, (1, 8))] + [jnp.zeros((depth, 256), F32)] * 4, axis=1)
    mod = jnp.stack([mod_all[:, 0:nb], jnp.broadcast_to(mod_all[:, nb:nb + 1], (depth, nb, 6, d))], axis=1)
    n1 = norm1[:, None, :]
    n2 = norm2[:, None, :]
    log_gamma = jax.nn.log_sigmoid(ret_decay.astype(F32))
    lg_rows = jnp.broadcast_to(jnp.repeat(log_gamma, RET_CHUNK_LEN, axis=2)[..., None],
                               (depth, 2, N_HEADS * RET_CHUNK_LEN, RET_CHUNK_LEN))
    lg_lane = jnp.repeat(log_gamma, HEAD_W, axis=2)
    pad_rows = lambda a, lo: jnp.pad(a, ((0, 0), (lo, LANES - GLA_RANK_W - lo), (0, 0)))
    gate_up = jnp.stack([pad_rows(gla_gate_up[:, 0], 0), pad_rows(gla_gate_up[:, 1], GLA_RANK_W)],
                        axis=1).astype(BF16)
    lam_init = [0.8 - 0.6 * math.exp(-0.3 * l) for l in range(depth)]
    lv = dif_lambda.astype(F32)
    lam = (jnp.exp(jnp.sum(lv[:, 0] * lv[:, 1], axis=-1)) - jnp.exp(jnp.sum(lv[:, 2] * lv[:, 3], axis=-1))
           + jnp.asarray(lam_init, F32))
    lam_lane = jnp.broadcast_to(lam[:, None, None], (depth, 1, BRANCH_W))
    subln = jnp.tile(dif_subln, (1, 4))[:, None, :]
    gnw = jnp.tile(gla_norm, (1, 4))[:, None, :]
    conv_pack = jnp.concatenate([conv_w, conv_b[:, None, :], jnp.zeros((depth, 4, D_FF_W), F32)], axis=1)

    for l in range(depth):
        need_ctx = l < depth - 1
        zr, zg, cq, ck, cv, dq, dk, dv = _inproj(_token_sources(x, ctx, xs, geo["rt"]), mod, n1, w_all, table,
                                                 bd64, bd32, qk_norms, geo, l)
        rof, rob = _retention(zr, lg_rows, lg_lane, hm_ret, bd64, geo, l)
        gof, gob = _gla(zg, gate_up, gla_gate_b, tri, bd64, hm16, hm_gla, causal, geo, l)
        yc = _gqa(cq, ck, cv, bd64, geo, need_ctx)
        yd = _dif(dq, dk, dv, lam_lane, subln, bd64, bd32, geo, need_ctx, 1.0 - lam_init[l], l)
        xs = _merge(_token_sources(x, ctx, xs, geo["rt_merge"]), mod, n1, w_all, w_branch_b, w_out_b, bd64, gnw, rof, rob, zr, gof, gob, zg, yc, yd,
                    geo, need_ctx, l)
        xs = _ffn(xs, mod, n2, w_up_b, conv_pack, w_down_b, geo, need_ctx, l)

    return xs
```
